```python
import jax, jax.numpy as jnp
from jax import lax
import numpy as np

D_MODEL = 1024
BATCH = 2
SEQ = 8192
DEPTH = 1

HEAD_DIM = 64
ROT_DIM = HEAD_DIM // 4
ROPE_THETA = 500000.0
N_HEADS_A = 8
MOBA_BLOCK = 256
MOBA_TOPK = 3
MOBA_CHUNK = 64
DIL_PAIRS = ((128, 1), (512, 4), (2048, 16))
HEADS_PER_GROUP_B = 4
N_HEADS_B = HEADS_PER_GROUP_B * len(DIL_PAIRS)
WIDTH_A = N_HEADS_A * HEAD_DIM
WIDTH_B = N_HEADS_B * HEAD_DIM
WIDTH_B_OUT = HEADS_PER_GROUP_B * HEAD_DIM
IN_COLS = 3 * WIDTH_A + 3 * WIDTH_B + 2 * D_MODEL
N_EXPERTS = 32
TOP_K = 4
D_FF = D_MODEL
SWIGLU_LIMIT = 7.0
SWIGLU_ALPHA = 1.702
EXPERT_BLOCK = 128
NORM_EPS = 1e-5

kernel_name = "moba_dilated_gated_moe_hybrid"


def rmsnorm(x, g):
    xf = x.astype(jnp.float32)
    y = xf * lax.rsqrt(jnp.mean(xf * xf, axis=-1, keepdims=True) + NORM_EPS)
    return (y * g.astype(jnp.float32)).astype(x.dtype)


def partial_rope(x, pos):
    half = ROT_DIM // 2
    inv = jnp.float32(ROPE_THETA) ** (-jnp.arange(half, dtype=jnp.float32) / half)
    ang = pos.astype(jnp.float32)[:, None] * inv[None, :]
    cos = jnp.cos(ang)[None, :, None, :]
    sin = jnp.sin(ang)[None, :, None, :]
    xr = x[..., :ROT_DIM].astype(jnp.float32)
    x1, x2 = xr[..., :half], xr[..., half:]
    rot = jnp.concatenate([x1 * cos - x2 * sin, x2 * cos + x1 * sin], axis=-1).astype(x.dtype)
    return jnp.concatenate([rot, x[..., ROT_DIM:]], axis=-1)


def attend(s, mask, v, eq):
    s = jnp.where(mask, s, -jnp.inf)
    m = jnp.max(s, axis=-1, keepdims=True)
    m = jnp.where(jnp.isfinite(m), m, 0.0)
    p = jnp.exp(s - m)
    l = jnp.sum(p, axis=-1, keepdims=True)
    o = jnp.einsum(eq, p, v.astype(jnp.float32)) / jnp.maximum(l, 1.0)
    lse = jnp.where(l[..., 0] > 0, m[..., 0] + jnp.log(jnp.maximum(l[..., 0], 1.0)), -jnp.inf)
    return o, lse


def moba_attention(q, k, v):
    B_, S_, H, Dh = q.shape
    nb = -(-S_ // MOBA_BLOCK)
    S_p = nb * MOBA_BLOCK
    pad = ((0, 0), (0, S_p - S_), (0, 0), (0, 0))
    q, k, v = [jnp.pad(t, pad).transpose(0, 2, 1, 3) for t in (q, k, v)]
    scale = HEAD_DIM ** -0.5
    qb = q.reshape(B_, H, nb, MOBA_BLOCK, Dh)
    kb = k.reshape(B_, H, nb, MOBA_BLOCK, Dh)
    vb = v.reshape(B_, H, nb, MOBA_BLOCK, Dh)
    s_own = jnp.einsum('bhnqd,bhnkd->bhnqk', qb, kb).astype(jnp.float32) * scale
    causal = jnp.tril(jnp.ones((MOBA_BLOCK, MOBA_BLOCK), dtype=bool))
    o_own, lse_own = attend(s_own, causal, vb, 'bhnqk,bhnkd->bhnqd')
    o_own = o_own.reshape(B_, H, S_p, Dh)
    lse_own = lse_own.reshape(B_, H, S_p)
    k_mean = jnp.mean(kb.astype(jnp.float32), axis=3)
    gate = jnp.einsum('bhsd,bhnd->bhsn', q.astype(jnp.float32), k_mean)
    q_blk = jnp.arange(S_p) // MOBA_BLOCK
    past = jnp.arange(nb)[None, :] < q_blk[:, None]
    gate = jnp.where(past, gate, -jnp.inf)
    k_sel = min(MOBA_TOPK, nb)
    g_val, g_idx = lax.top_k(gate, k_sel)
    g_ok = jnp.isfinite(g_val)
    nc = S_p // MOBA_CHUNK

    def to_chunks(t):
        return jnp.moveaxis(t.reshape(B_, H, nc, MOBA_CHUNK, *t.shape[3:]), 2, 0)

    gather_blocks = jax.vmap(jax.vmap(lambda blocks, idx: blocks[idx]))

    def past_chunk(args):
        qc, ic, okc = args
        kg = gather_blocks(kb, ic).reshape(B_, H, MOBA_CHUNK, k_sel * MOBA_BLOCK, Dh)
        vg = gather_blocks(vb, ic).reshape(B_, H, MOBA_CHUNK, k_sel * MOBA_BLOCK, Dh)
        s = jnp.einsum('bhqd,bhqnd->bhqn', qc, kg).astype(jnp.float32) * scale
        mask = jnp.repeat(okc, MOBA_BLOCK, axis=-1)
        return attend(s, mask, vg, 'bhqn,bhqnd->bhqd')

    o_past, lse_past = lax.map(past_chunk, (to_chunks(q), to_chunks(g_idx), to_chunks(g_ok)))
    o_past = jnp.moveaxis(o_past, 0, 2).reshape(B_, H, S_p, Dh)
    lse_past = jnp.moveaxis(lse_past, 0, 2).reshape(B_, H, S_p)
    lse = jnp.logaddexp(lse_own, lse_past)
    o = o_own * jnp.exp(lse_own - lse)[..., None] + o_past * jnp.exp(lse_past - lse)[..., None]
    return o.transpose(0, 2, 1, 3)[:, :S_].reshape(B_, S_, H * Dh).astype(v.dtype)


def dilated_group(q, k, v, window, dilation):
    B_, S_, Hg, Dh = q.shape
    band = window // dilation
    span = dilation * band
    S_p = -(-S_ // span) * span
    L = S_p // dilation
    nb = L // band
    scale = HEAD_DIM ** -0.5

    def to_sub(t):
        t = jnp.pad(t, ((0, 0), (0, S_p - S_), (0, 0), (0, 0)))
        t = t.reshape(B_, L, dilation, Hg, Dh).transpose(0, 2, 3, 1, 4)
        return t.reshape(B_, dilation, Hg, nb, band, Dh)

    def with_prev(t):
        prev = jnp.pad(t, ((0, 0), (0, 0), (0, 0), (1, 0), (0, 0), (0, 0)))[:, :, :, :-1]
        return jnp.concatenate([prev, t], axis=4)

    qs, ks, vs = to_sub(q), to_sub(k), to_sub(v)
    kw, vw = with_prev(ks), with_prev(vs)
    s = jnp.einsum('brhnqd,brhnkd->brhnqk', qs, kw).astype(jnp.float32) * scale
    qi = jnp.arange(band)[:, None]
    kj = jnp.arange(2 * band)[None, :]
    dist = qi + band - kj
    in_band = (dist >= 0) & (dist <= band)
    no_prev = (jnp.arange(nb) == 0)[:, None, None] & (kj < band)[None]
    mask = in_band[None] & ~no_prev
    o, lse = attend(s, mask, vw, 'brhnqk,brhnkd->brhnqd')
    o = o.reshape(B_, dilation, Hg, L, Dh).transpose(0, 3, 1, 2, 4).reshape(B_, S_p, Hg, Dh)[:, :S_]
    lse = lse.reshape(B_, dilation, Hg, L).transpose(0, 3, 1, 2).reshape(B_, S_p, Hg)[:, :S_]
    return o, lse


def dilated_attention(q, k, v):
    B_, S_ = q.shape[:2]
    outs, lses = [], []
    for g, (win, dil) in enumerate(DIL_PAIRS):
        sl = slice(g * HEADS_PER_GROUP_B, (g + 1) * HEADS_PER_GROUP_B)
        o, l = dilated_group(q[:, :, sl], k[:, :, sl], v[:, :, sl], win, dil)
        outs.append(o)
        lses.append(l)
    o = jnp.stack(outs)
    wts = jax.nn.softmax(jnp.stack(lses), axis=0)
    out = jnp.sum(wts[..., None] * o, axis=0)
    return out.reshape(B_, S_, WIDTH_B_OUT).astype(v.dtype)


def moe_ffn(h, w_router, b_router, w_gate, b_gate, w_up, b_up, w_down, b_down):
    B_, S_, D = h.shape
    T = B_ * S_
    xt = h.reshape(T, D)
    logits = (xt @ w_router).astype(jnp.float32) + b_router.astype(jnp.float32)
    top_val, top_idx = lax.top_k(logits, TOP_K)
    probs = jax.nn.softmax(top_val, axis=-1)
    n_assign = T * TOP_K
    flat_e = top_idx.reshape(-1)
    flat_tok = jnp.repeat(jnp.arange(T, dtype=jnp.int32), TOP_K)
    flat_p = probs.reshape(-1)
    order = jnp.argsort(flat_e)
    e_sorted = flat_e[order]
    counts = jnp.bincount(flat_e, length=N_EXPERTS)
    padded = (counts + EXPERT_BLOCK - 1) // EXPERT_BLOCK * EXPERT_BLOCK
    start = jnp.cumsum(counts) - counts
    pend = jnp.cumsum(padded)
    pstart = pend - padded
    dest = pstart[e_sorted] + (jnp.arange(n_assign) - start[e_sorted])
    n_rows = -(-n_assign // EXPERT_BLOCK) * EXPERT_BLOCK + N_EXPERTS * EXPERT_BLOCK
    row_tok = jnp.zeros((n_rows,), jnp.int32).at[dest].set(flat_tok[order])
    row_p = jnp.zeros((n_rows,), jnp.float32).at[dest].set(flat_p[order])
    n_blk = n_rows // EXPERT_BLOCK
    blk_expert = jnp.minimum(jnp.searchsorted(pend, jnp.arange(n_blk) * EXPERT_BLOCK, side='right'),
                             N_EXPERTS - 1)
    x_rows = xt[row_tok].reshape(n_blk, EXPERT_BLOCK, D)

    def expert_block(args):
        xb, e = args
        g = xb @ w_gate[e] + b_gate[e]
        u = xb @ w_up[e] + b_up[e]
        g = jnp.minimum(g, SWIGLU_LIMIT)
        u = jnp.clip(u, -SWIGLU_LIMIT, SWIGLU_LIMIT)
        a = g * jax.nn.sigmoid(SWIGLU_ALPHA * g) * (u + 1.0)
        return a @ w_down[e] + b_down[e]

    y_rows = lax.map(expert_block, (x_rows, blk_expert)).reshape(n_rows, D)
    y = jax.ops.segment_sum(y_rows * row_p[:, None].astype(y_rows.dtype), row_tok, num_segments=T)
    return y.reshape(B_, S_, D).astype(h.dtype)


def setup_inputs(seed: int = 0) -> dict:
    key = jax.random.key(seed)
    ks = jax.random.split(key, 20)
    f32 = jnp.float32
    nrm = lambda k, shape, s: jax.random.normal(k, shape, f32) * s
    L, D, E, F = DEPTH, D_MODEL, N_EXPERTS, D_FF
    return {
        "x": nrm(ks[0], (BATCH, SEQ, D), 1.0),
        "ln_mix_g": 1.0 + nrm(ks[1], (L, D), 0.02),
        "w_in": nrm(ks[2], (L, D, IN_COLS), D ** -0.5),
        "b_gate_a": nrm(ks[3], (L, D), 0.02),
        "b_gate_b": nrm(ks[4], (L, D), 0.02),
        "w_o_a": nrm(ks[5], (L, WIDTH_A, D), WIDTH_A ** -0.5),
        "w_o_b": nrm(ks[6], (L, WIDTH_B_OUT, D), WIDTH_B_OUT ** -0.5),
        "w_out": nrm(ks[7], (L, D, D), D ** -0.5),
        "ln_ffn_g": 1.0 + nrm(ks[8], (L, D), 0.02),
        "w_router": nrm(ks[9], (L, D, E), D ** -0.5),
        "b_router": nrm(ks[10], (L, E), 0.01),
        "w_gate": nrm(ks[11], (L, E, D, F), D ** -0.5),
        "b_gate": nrm(ks[12], (L, E, F), 0.01),
        "w_up": nrm(ks[13], (L, E, D, F), D ** -0.5),
        "b_up": nrm(ks[14], (L, E, F), 0.01),
        "w_down": nrm(ks[15], (L, E, F, D), F ** -0.5),
        "b_down": nrm(ks[16], (L, E, D), 0.01),
        "ln_final_g": 1.0 + nrm(ks[17], (D,), 0.02),
    }


def reference(x, ln_mix_g, w_in, b_gate_a, b_gate_b, w_o_a, w_o_b, w_out, ln_ffn_g,
              w_router, b_router, w_gate, b_gate, w_up, b_up, w_down, b_down, ln_final_g):
    B_, S_, _ = x.shape
    pos = jnp.arange(S_)
    splits = [WIDTH_A, 2 * WIDTH_A, 3 * WIDTH_A,
              3 * WIDTH_A + WIDTH_B, 3 * WIDTH_A + 2 * WIDTH_B, 3 * WIDTH_A + 3 * WIDTH_B,
              3 * WIDTH_A + 3 * WIDTH_B + D_MODEL]
    for l in range(DEPTH):
        h = rmsnorm(x, ln_mix_g[l])
        proj = h @ w_in[l]
        qa, ka, va, qb, kb, vb, ga, gb = jnp.split(proj, splits, axis=-1)
        heads = lambda t, n: t.reshape(B_, S_, n, HEAD_DIM)
        qa = partial_rope(heads(qa, N_HEADS_A), pos)
        ka = partial_rope(heads(ka, N_HEADS_A), pos)
        qb = partial_rope(heads(qb, N_HEADS_B), pos)
        kb = partial_rope(heads(kb, N_HEADS_B), pos)
        o_a = moba_attention(qa, ka, heads(va, N_HEADS_A))
        o_b = dilated_attention(qb, kb, heads(vb, N_HEADS_B))
        mix = (jax.nn.sigmoid(ga + b_gate_a[l]) * (o_a @ w_o_a[l])
               + jax.nn.sigmoid(gb + b_gate_b[l]) * (o_b @ w_o_b[l]))
        x = x + mix @ w_out[l]
        h = rmsnorm(x, ln_ffn_g[l])
        x = x + moe_ffn(h, w_router[l], b_router[l], w_gate[l], b_gate[l], w_up[l], b_up[l],
                        w_down[l], b_down[l])
    return rmsnorm(x, ln_final_g)
```

```python
import functools

import jax
import jax.numpy as jnp
from jax import lax
from jax.experimental import pallas as pl
from jax.experimental.pallas import tpu as pltpu

D_MODEL = 1024
HEAD_DIM = 64
ROT_HALF = HEAD_DIM // 8
ROPE_THETA = 500000.0
N_HEADS_A = 8
MOBA_BLOCK = 256
MOBA_TOPK = 3
DIL_PAIRS = ((128, 1), (512, 4), (2048, 16))
DIL_BAND = 128
HEADS_PER_GROUP_B = 4
WIDTH_A = N_HEADS_A * HEAD_DIM
WIDTH_G = HEADS_PER_GROUP_B * HEAD_DIM
WIDTH_B = WIDTH_G * len(DIL_PAIRS)
N_EXPERTS = 32
TOP_K = 4
SWIGLU_LIMIT = 7.0
SWIGLU_ALPHA = 1.702
NORM_EPS = 1e-5
SCALE = HEAD_DIM ** -0.5

LANES = 128
PROJ_TM = 512
DIL_TL = 512
POST_TM = 256
EXPERT_ROWS = 256
DISPATCH_TM = 512
COMBINE_TM = 256
VMEM_LIMIT = 56 * 1024 * 1024
NEG = -1e30

BF16 = jnp.bfloat16
F32 = jnp.float32


def _dot(a, b):
    return jnp.dot(a, b, preferred_element_type=F32)


def _dot_nt(a, b):
    return lax.dot_general(a, b, (((1,), (1,)), ((), ())), preferred_element_type=F32)


def _rms(x, g):
    ms = jnp.mean(x * x, axis=-1, keepdims=True)
    return x * lax.rsqrt(ms + NORM_EPS) * g


def _cparams(sem):
    return pltpu.CompilerParams(dimension_semantics=sem, vmem_limit_bytes=VMEM_LIMIT)


def _proj_kernel(x_ref, g_ref, wq_ref, wk_ref, wv_ref, cq_ref, sq_ref, ck_ref, s1_ref, s2_ref, *outs, na):
    tm = x_ref.shape[0]
    h = _rms(x_ref[...], g_ref[...]).astype(BF16)
    nq = wq_ref.shape[0]
    heads = nq // HEAD_DIM

    qT = _dot_nt(wq_ref[...], h)
    q3 = qT.reshape(heads, HEAD_DIM, tm)
    c = cq_ref[...][None]
    s = sq_ref[...][None]
    x1 = q3[:, 0:ROT_HALF]
    x2 = q3[:, ROT_HALF:2 * ROT_HALF]
    q3 = jnp.concatenate([x1 * c - x2 * s, x2 * c + x1 * s, q3[:, 2 * ROT_HALF:]], axis=1)
    qT = (q3 * SCALE).reshape(nq, tm).astype(BF16)

    vT = _dot_nt(wv_ref[...], h).astype(BF16)

    kk = _dot(h, wk_ref[...])
    ck, s1, s2 = ck_ref[...], s1_ref[...], s2_ref[...]
    kparts = []
    for gi in range(kk.shape[1] // LANES):
        kg = kk[:, gi * LANES:(gi + 1) * LANES]
        kparts.append(kg * ck + pltpu.roll(kg, LANES - ROT_HALF, 1) * s1 + pltpu.roll(kg, ROT_HALF, 1) * s2)

    if na:
        qa_ref, va_ref, ka_ref, km_ref, qb_ref, vb_ref, kb_ref = outs
        for blk in range(tm // MOBA_BLOCK):
            sl = slice(blk * MOBA_BLOCK, (blk + 1) * MOBA_BLOCK)
            qa_ref[blk] = qT[0:na, sl]
            va_ref[blk] = vT[0:na, sl]
        for gi in range(na // LANES):
            ka_ref[:, gi * LANES:(gi + 1) * LANES] = kparts[gi].astype(BF16)
            km = kparts[gi].reshape(tm // MOBA_BLOCK, MOBA_BLOCK, LANES).sum(axis=1) * (1.0 / MOBA_BLOCK)
            km_ref[:, gi * LANES:(gi + 1) * LANES] = km
    else:
        qb_ref, vb_ref, kb_ref = outs
    for cb in range(tm // DIL_BAND):
        sl = slice(cb * DIL_BAND, (cb + 1) * DIL_BAND)
        qb_ref[cb] = qT[na:na + WIDTH_G, sl]
        vb_ref[cb] = vT[na:na + WIDTH_G, sl]
    for gi in range(WIDTH_G // LANES):
        kb_ref[:, gi * LANES:(gi + 1) * LANES] = kparts[na // LANES + gi].astype(BF16)


def _project(x, g, wqT, wk, wvT, tabs, d, na):
    B, S, D = x.shape
    L = S // d
    tm = min(PROJ_TM, L)
    cq, sq, ck, s1, s2 = tabs
    xv = x.reshape(B, L, d * D)
    grid = (B, d, L // tm)
    full = lambda a: pl.BlockSpec(a.shape, lambda b, r, i: (0,) * a.ndim)
    in_specs = [
        pl.BlockSpec((None, tm, D), lambda b, r, i: (b, i, r)),
        full(g), full(wqT), full(wk), full(wvT),
        pl.BlockSpec((None, ROT_HALF, tm), lambda b, r, i: (r, 0, i)),
        pl.BlockSpec((None, ROT_HALF, tm), lambda b, r, i: (r, 0, i)),
        pl.BlockSpec((None, tm, LANES), lambda b, r, i: (r, i, 0)),
        pl.BlockSpec((None, tm, LANES), lambda b, r, i: (r, i, 0)),
        pl.BlockSpec((None, tm, LANES), lambda b, r, i: (r, i, 0)),
    ]
    nsub = tm // DIL_BAND
    out_shape = [
        jax.ShapeDtypeStruct((B, d, L // DIL_BAND, WIDTH_G, DIL_BAND), BF16),
        jax.ShapeDtypeStruct((B, d, L // DIL_BAND, WIDTH_G, DIL_BAND), BF16),
        jax.ShapeDtypeStruct((B, d, L, WIDTH_G), BF16),
    ]
    out_specs = [
        pl.BlockSpec((None, None, nsub, WIDTH_G, DIL_BAND), lambda b, r, i: (b, r, i, 0, 0)),
        pl.BlockSpec((None, None, nsub, WIDTH_G, DIL_BAND), lambda b, r, i: (b, r, i, 0, 0)),
        pl.BlockSpec((None, None, tm, WIDTH_G), lambda b, r, i: (b, r, i, 0)),
    ]
    if na:
        nblk = tm // MOBA_BLOCK
        out_shape = [
            jax.ShapeDtypeStruct((B, S // MOBA_BLOCK, na, MOBA_BLOCK), BF16),
            jax.ShapeDtypeStruct((B, S // MOBA_BLOCK, na, MOBA_BLOCK), BF16),
            jax.ShapeDtypeStruct((B, S, na), BF16),
            jax.ShapeDtypeStruct((B, S // tm, nblk, na), F32),
        ] + out_shape
        out_specs = [
            pl.BlockSpec((None, nblk, na, MOBA_BLOCK), lambda b, r, i: (b, i, 0, 0)),
            pl.BlockSpec((None, nblk, na, MOBA_BLOCK), lambda b, r, i: (b, i, 0, 0)),
            pl.BlockSpec((None, tm, na), lambda b, r, i: (b, i, 0)),
            pl.BlockSpec((None, None, nblk, na), lambda b, r, i: (b, i, 0, 0)),
        ] + out_specs
    return pl.pallas_call(
        functools.partial(_proj_kernel, na=na),
        grid=grid, in_specs=in_specs, out_specs=out_specs, out_shape=out_shape,
        compiler_params=_cparams(("arbitrary",) * 3),
        name=f"proj_d{d}",
    )(xv, g, wqT, wk, wvT, cq, sq, ck, s1, s2)


def _moba_kernel(q_ref, k_ref, v_ref, km_ref, o_ref, qh_scr, bias_scr, m_scr, l_scr, acc_scr):
    n = pl.program_id(2)
    nb = km_ref.shape[0]
    blk = MOBA_BLOCK
    q2 = q_ref[...]
    rowid = lax.broadcasted_iota(jnp.int32, q2.shape, 0)
    bid = lax.broadcasted_iota(jnp.int32, (nb, blk), 0)
    key_i = lax.broadcasted_iota(jnp.int32, (blk, blk), 0)
    qry_i = lax.broadcasted_iota(jnp.int32, (blk, blk), 1)
    km = km_ref[...].astype(BF16)
    k_own = k_ref[pl.ds(pl.multiple_of(n * blk, blk), blk), :]

    for h in range(2):
        qh = jnp.where((rowid >= HEAD_DIM * h) & (rowid < HEAD_DIM * (h + 1)), q2, jnp.zeros_like(q2))
        qh_scr[h] = qh
        g = jnp.where(bid < n, _dot(km, qh), -jnp.inf)
        sel = jnp.zeros(g.shape, jnp.bool_)
        for _ in range(MOBA_TOPK):
            mx = jnp.max(g, axis=0, keepdims=True)
            first = jnp.min(jnp.where((g == mx) & (mx > -jnp.inf), bid, nb), axis=0, keepdims=True)
            pick = bid == first
            sel = sel | pick
            g = jnp.where(pick, -jnp.inf, g)
        bias_scr[h] = jnp.where(sel, 0.0, NEG)
        s = jnp.where(key_i <= qry_i, _dot(k_own, qh), -jnp.inf)
        m = jnp.max(s, axis=0, keepdims=True)
        p = jnp.exp(s - m)
        m_scr[h] = m
        l_scr[h] = jnp.sum(p, axis=0, keepdims=True)
        acc_scr[h] = _dot(v_ref[n, HEAD_DIM * h:HEAD_DIM * (h + 1), :], p.astype(BF16))

    def body(j, carry):
        kb = k_ref[pl.ds(pl.multiple_of(j * blk, blk), blk), :]
        for h in range(2):
            s = _dot(kb, qh_scr[h]) + bias_scr[h, pl.ds(j, 1), :]
            m_prev = m_scr[h]
            m_new = jnp.maximum(m_prev, jnp.max(s, axis=0, keepdims=True))
            alpha = jnp.exp(m_prev - m_new)
            p = jnp.exp(s - m_new)
            l_scr[h] = alpha * l_scr[h] + jnp.sum(p, axis=0, keepdims=True)
            acc_scr[h] = alpha * acc_scr[h] + _dot(v_ref[j, HEAD_DIM * h:HEAD_DIM * (h + 1), :], p.astype(BF16))
            m_scr[h] = m_new
        return carry

    lax.fori_loop(0, n, body, 0)
    oT = jnp.concatenate([acc_scr[0] / l_scr[0], acc_scr[1] / l_scr[1]], axis=0)
    o_ref[...] = oT.T.astype(o_ref.dtype)


def _moba(qT, k, vT, kmean):
    B, nb, wa, blk = qT.shape
    S = nb * blk
    hp = wa // LANES
    return pl.pallas_call(
        _moba_kernel,
        grid=(B, hp, nb),
        in_specs=[
            pl.BlockSpec((None, None, LANES, blk), lambda b, p, n: (b, n, p, 0)),
            pl.BlockSpec((None, S, LANES), lambda b, p, n: (b, 0, p)),
            pl.BlockSpec((None, nb, LANES, blk), lambda b, p, n: (b, 0, p, 0)),
            pl.BlockSpec((None, nb, LANES), lambda b, p, n: (b, 0, p)),
        ],
        out_specs=pl.BlockSpec((None, blk, LANES), lambda b, p, n: (b, n, p)),
        out_shape=jax.ShapeDtypeStruct((B, S, wa), BF16),
        scratch_shapes=[
            pltpu.VMEM((2, LANES, blk), BF16),
            pltpu.VMEM((2, nb, blk), F32),
            pltpu.VMEM((2, 1, blk), F32),
            pltpu.VMEM((2, 1, blk), F32),
            pltpu.VMEM((2, HEAD_DIM, blk), F32),
        ],
        compiler_params=_cparams(("arbitrary",) * 3),
        name="moba_attn",
    )(qT, k, vT, kmean)


def _dil_kernel(q_ref, k_ref, kp_ref, v_ref, vp_ref, o_ref, lse_ref):
    t = pl.program_id(2)
    band = DIL_BAND
    nsub = q_ref.shape[0]
    key_i = lax.broadcasted_iota(jnp.int32, (band, band), 0)
    qry_i = lax.broadcasted_iota(jnp.int32, (band, band), 1)
    rowid = lax.broadcasted_iota(jnp.int32, (LANES, band), 0)
    own_ok = key_i <= qry_i
    prev_ok = key_i >= qry_i
    for c in range(nsub):
        o_parts, l_parts = [], []
        for hp in range(WIDTH_G // LANES):
            cols = slice(hp * LANES, (hp + 1) * LANES)
            k_own = k_ref[c * band:(c + 1) * band, cols]
            k_prev = kp_ref[:, cols] if c == 0 else k_ref[(c - 1) * band:c * band, cols]
            q2 = q_ref[c, cols, :]
            for h in range(2):
                qh = jnp.where((rowid >= HEAD_DIM * h) & (rowid < HEAD_DIM * (h + 1)), q2, jnp.zeros_like(q2))
                pmask = (prev_ok & (t > 0)) if c == 0 else prev_ok
                s_own = jnp.where(own_ok, _dot(k_own, qh), -jnp.inf)
                s_prev = jnp.where(pmask, _dot(k_prev, qh), -jnp.inf)
                m = jnp.maximum(jnp.max(s_own, axis=0, keepdims=True), jnp.max(s_prev, axis=0, keepdims=True))
                p_own = jnp.exp(s_own - m)
                p_prev = jnp.exp(s_prev - m)
                l = jnp.sum(p_own, axis=0, keepdims=True) + jnp.sum(p_prev, axis=0, keepdims=True)
                rows = slice(hp * LANES + h * HEAD_DIM, hp * LANES + (h + 1) * HEAD_DIM)
                v_own = v_ref[c, rows, :]
                v_prev = vp_ref[rows, :] if c == 0 else v_ref[c - 1, rows, :]
                oT = _dot(v_own, p_own.astype(BF16)) + _dot(v_prev, p_prev.astype(BF16))
                o_parts.append(oT / l)
                l_parts.append(jnp.broadcast_to(m + jnp.log(l), (HEAD_DIM, band)))
        o_ref[c * band:(c + 1) * band, :] = jnp.concatenate(o_parts, axis=0).T
        lse_ref[c * band:(c + 1) * band, :] = jnp.concatenate(l_parts, axis=0).T


def _dilated(qT, k, vT):
    B, d, nblk, wg, band = qT.shape
    L = nblk * band
    tl = min(DIL_TL, L)
    nsub = tl // band
    prev = lambda t: jnp.maximum(t * nsub - 1, 0)
    return pl.pallas_call(
        _dil_kernel,
        grid=(B, d, L // tl),
        in_specs=[
            pl.BlockSpec((None, None, nsub, wg, band), lambda b, r, t: (b, r, t, 0, 0)),
            pl.BlockSpec((None, None, tl, wg), lambda b, r, t: (b, r, t, 0)),
            pl.BlockSpec((None, None, band, wg), lambda b, r, t: (b, r, prev(t), 0)),
            pl.BlockSpec((None, None, nsub, wg, band), lambda b, r, t: (b, r, t, 0, 0)),
            pl.BlockSpec((None, None, None, wg, band), lambda b, r, t: (b, r, prev(t), 0, 0)),
        ],
        out_specs=[
            pl.BlockSpec((None, tl, wg), lambda b, r, t: (b, t, r)),
            pl.BlockSpec((None, tl, wg), lambda b, r, t: (b, t, r)),
        ],
        out_shape=[jax.ShapeDtypeStruct((B, L, d * wg), F32)] * 2,
        compiler_params=_cparams(("arbitrary",) * 3),
        name=f"dilated_d{d}",
    )(qT, k, k, vT, vT)


def _post_kernel(x_ref, oa_ref, o1_ref, o2_ref, o3_ref, l1_ref, l2_ref, l3_ref, gmix_ref, wg_ref, bga_ref,
                 bgb_ref, woa_ref, wob_ref, wout_ref, gffn_ref, wr_ref, br_ref,
                 x1_ref, h2_ref, idx_ref, p_ref, rank_ref, pnat_ref, cnt_ref, carry_scr):
    i = pl.program_id(0)
    tm, D = x_ref.shape
    ne = wr_ref.shape[0]

    @pl.when(i == 0)
    def _():
        carry_scr[...] = jnp.zeros_like(carry_scr)

    x = x_ref[...]
    h = _rms(x, gmix_ref[...]).astype(BF16)
    gates = _dot(h, wg_ref[...])
    ga = gates[:, :D] + bga_ref[...]
    gb = gates[:, D:] + bgb_ref[...]
    l1, l2, l3 = l1_ref[...], l2_ref[...], l3_ref[...]
    mx = jnp.maximum(jnp.maximum(l1, l2), l3)
    e1, e2, e3 = jnp.exp(l1 - mx), jnp.exp(l2 - mx), jnp.exp(l3 - mx)
    ob = (e1 * o1_ref[...] + e2 * o2_ref[...] + e3 * o3_ref[...]) / (e1 + e2 + e3)
    ya = _dot(oa_ref[...], woa_ref[...])
    yb = _dot(ob.astype(BF16), wob_ref[...])
    mix = jax.nn.sigmoid(ga) * ya + jax.nn.sigmoid(gb) * yb
    x1 = x + _dot(mix.astype(BF16), wout_ref[...])
    x1_ref[...] = x1
    h2 = _rms(x1, gffn_ref[...])
    h2_ref[...] = h2

    logits = lax.dot_general(wr_ref[...], h2, (((1,), (1,)), ((), ())), preferred_element_type=F32,
                             precision=lax.Precision.HIGHEST) + br_ref[...]
    eid = lax.broadcasted_iota(jnp.int32, (ne, tm), 0)
    g = logits
    vals, idxs, picks = [], [], []
    for _ in range(TOP_K):
        m = jnp.max(g, axis=0, keepdims=True)
        first = jnp.min(jnp.where(g == m, eid, ne), axis=0, keepdims=True)
        pick = eid == first
        vals.append(m)
        idxs.append(first)
        picks.append(pick)
        g = jnp.where(pick, -jnp.inf, g)
    es = [jnp.exp(v - vals[0]) for v in vals]
    den = es[0] + es[1] + es[2] + es[3]
    probs = [e / den for e in es]

    onehot = jnp.zeros((ne, tm), F32)
    for pick in picks:
        onehot = onehot + pick.astype(F32)
    earlier = (lax.broadcasted_iota(jnp.int32, (tm, tm), 0) < lax.broadcasted_iota(jnp.int32, (tm, tm), 1))
    prefix = _dot(onehot.astype(BF16), earlier.astype(BF16)) + carry_scr[:, 0:1]
    ranks = [jnp.sum(jnp.where(pick, prefix, 0.0), axis=0, keepdims=True) for pick in picks]
    carry_scr[...] = carry_scr[...] + jnp.sum(onehot, axis=1, keepdims=True)
    cnt_ref[...] = carry_scr[...]

    zi = jnp.zeros((8 - TOP_K, tm), jnp.int32)
    idx_ref[...] = jnp.concatenate(idxs + [zi], axis=0)
    rank_ref[...] = jnp.concatenate([r.astype(jnp.int32) for r in ranks] + [zi], axis=0)
    p_ref[...] = jnp.concatenate(probs + [jnp.zeros((8 - TOP_K, tm), F32)], axis=0)
    pnat_ref[...] = jnp.concatenate(probs + [jnp.zeros((LANES - TOP_K, tm), F32)], axis=0).T


def _post(x2, oa, obs, lses, gmix, wg, bga, bgb, woa, wob, wout, gffn, wrT, br):
    T, D = x2.shape
    tm = POST_TM
    ne = wrT.shape[0]
    row = lambda w: pl.BlockSpec((tm, w), lambda i: (i, 0))
    full = lambda a: pl.BlockSpec(a.shape, lambda i: (0,) * a.ndim)
    col = pl.BlockSpec((8, tm), lambda i: (0, i))
    return pl.pallas_call(
        _post_kernel,
        grid=(T // tm,),
        in_specs=[row(D), row(WIDTH_A)] + [row(WIDTH_G)] * 6
                 + [full(a) for a in (gmix, wg, bga, bgb, woa, wob, wout, gffn, wrT, br)],
        out_specs=[row(D), row(D), col, col, col, row(LANES), pl.BlockSpec((ne, LANES), lambda i: (0, 0))],
        out_shape=[
            jax.ShapeDtypeStruct((T, D), F32), jax.ShapeDtypeStruct((T, D), F32),
            jax.ShapeDtypeStruct((8, T), jnp.int32), jax.ShapeDtypeStruct((8, T), F32),
            jax.ShapeDtypeStruct((8, T), jnp.int32), jax.ShapeDtypeStruct((T, LANES), F32),
            jax.ShapeDtypeStruct((ne, LANES), F32),
        ],
        scratch_shapes=[pltpu.VMEM((ne, LANES), F32)],
        compiler_params=_cparams(("arbitrary",)),
        name="post_mix_router",
    )(x2, oa, *obs, *lses, gmix, wg, bga, bgb, woa, wob, wout, gffn, wrT, br)


def _dispatch_kernel(pend_ref, padded_ref, dest_ref, h_ref, xr_ref, zero_scr, sem):
    i = pl.program_id(0)
    tm = h_ref.shape[0]
    br = zero_scr.shape[0]

    def zero_copy(start):
        return pltpu.make_async_copy(zero_scr, xr_ref.at[pl.ds(pl.multiple_of(start, br), br), :], sem)

    @pl.when(i == 0)
    def _():
        zero_scr[...] = jnp.zeros_like(zero_scr)
        n_used = pend_ref[N_EXPERTS - 1] // br
        n_blk = xr_ref.shape[0] // br

        def tail(fn):
            def go(e, c):
                @pl.when(padded_ref[e] > 0)
                def _():
                    fn(zero_copy(pend_ref[e] - br))
                return c
            return go

        def unused(fn):
            def go(b, c):
                fn(zero_copy(b * br))
                return c
            return go

        lax.fori_loop(0, N_EXPERTS, tail(lambda cp: cp.start()), 0)
        lax.fori_loop(n_used, n_blk, unused(lambda cp: cp.start()), 0)
        lax.fori_loop(0, N_EXPERTS, tail(lambda cp: cp.wait()), 0)
        lax.fori_loop(n_used, n_blk, unused(lambda cp: cp.wait()), 0)

    def row_copy(t, k):
        return pltpu.make_async_copy(h_ref.at[pl.ds(t, 1), :], xr_ref.at[pl.ds(dest_ref[k, t], 1), :], sem)

    def start(t, c):
        for k in range(TOP_K):
            row_copy(t, k).start()
        return c

    def wait(t, c):
        for k in range(TOP_K):
            row_copy(t, k).wait()
        return c

    lax.fori_loop(0, tm, start, 0)
    lax.fori_loop(0, tm, wait, 0)


def _dispatch(pend, padded, dest3, h2, n_rows):
    T, D = h2.shape
    tm = DISPATCH_TM
    return pl.pallas_call(
        _dispatch_kernel,
        grid_spec=pltpu.PrefetchScalarGridSpec(
            num_scalar_prefetch=2,
            grid=(T // tm,),
            in_specs=[
                pl.BlockSpec((None, TOP_K, tm), lambda i, pe, pa: (i, 0, 0), memory_space=pltpu.SMEM),
                pl.BlockSpec((tm, D), lambda i, pe, pa: (i, 0)),
            ],
            out_specs=pl.BlockSpec(memory_space=pl.ANY),
            scratch_shapes=[pltpu.VMEM((EXPERT_ROWS, D), F32), pltpu.SemaphoreType.DMA(())],
        ),
        out_shape=jax.ShapeDtypeStruct((n_rows, D), F32),
        compiler_params=_cparams(("arbitrary",)),
        name="moe_dispatch",
    )(pend, padded, dest3, h2)


def _expert_kernel(be_ref, nu_ref, x_ref, wg_ref, bg_ref, wu_ref, bu_ref, wd_ref, bd_ref, y_ref,
                   wgb_scr, wub_scr, wdb_scr):
    b = pl.program_id(0)
    used = b < nu_ref[0]
    new_expert = (b == 0) | (be_ref[b] != be_ref[jnp.maximum(b - 1, 0)])

    @pl.when(used & new_expert)
    def _():
        wgb_scr[...] = wg_ref[...].astype(BF16)
        wub_scr[...] = wu_ref[...].astype(BF16)
        wdb_scr[...] = wd_ref[...].astype(BF16)

    @pl.when(used)
    def _():
        x = x_ref[...].astype(BF16)
        g = _dot(x, wgb_scr[...]) + bg_ref[...]
        u = _dot(x, wub_scr[...]) + bu_ref[...]
        g = jnp.minimum(g, SWIGLU_LIMIT)
        u = jnp.clip(u, -SWIGLU_LIMIT, SWIGLU_LIMIT)
        a = g * jax.nn.sigmoid(SWIGLU_ALPHA * g) * (u + 1.0)
        y_ref[...] = _dot(a.astype(BF16), wdb_scr[...]) + bd_ref[...]

    @pl.when(jnp.logical_not(used))
    def _():
        y_ref[...] = jnp.zeros_like(y_ref)


def _experts(blk_expert, n_used, x_rows, w_gate, b_gate, w_up, b_up, w_down, b_down):
    n_rows, D = x_rows.shape
    E, _, F = w_gate.shape
    br = EXPERT_ROWS
    wspec = lambda shape: pl.BlockSpec((None,) + shape, lambda b, be, nu: (be[b], 0, 0))
    return pl.pallas_call(
        _expert_kernel,
        grid_spec=pltpu.PrefetchScalarGridSpec(
            num_scalar_prefetch=2,
            grid=(n_rows // br,),
            in_specs=[
                pl.BlockSpec((br, D), lambda b, be, nu: (jnp.minimum(b, nu[0] - 1), 0)),
                wspec((D, F)), wspec((1, F)), wspec((D, F)), wspec((1, F)), wspec((F, D)), wspec((1, D)),
            ],
            out_specs=pl.BlockSpec((br, D), lambda b, be, nu: (b, 0)),
            scratch_shapes=[pltpu.VMEM((D, F), BF16), pltpu.VMEM((D, F), BF16), pltpu.VMEM((F, D), BF16)],
        ),
        out_shape=jax.ShapeDtypeStruct((n_rows, D), F32),
        compiler_params=_cparams(("arbitrary",)),
        name="moe_experts",
    )(blk_expert, n_used, x_rows, w_gate, b_gate.reshape(E, 1, F), w_up, b_up.reshape(E, 1, F),
      w_down, b_down.reshape(E, 1, D))


def _combine_kernel(dest_ref, y_ref, x1_ref, p_ref, g_ref, o_ref, ybuf, sem):
    tm = x1_ref.shape[0]

    def row_copy(t, k):
        return pltpu.make_async_copy(y_ref.at[pl.ds(dest_ref[k, t], 1), :], ybuf.at[k, pl.ds(t, 1), :], sem)

    def start(t, c):
        for k in range(TOP_K):
            row_copy(t, k).start()
        return c

    def wait(t, c):
        for k in range(TOP_K):
            row_copy(t, k).wait()
        return c

    lax.fori_loop(0, tm, start, 0)
    lax.fori_loop(0, tm, wait, 0)
    p = p_ref[...]
    y = x1_ref[...]
    for k in range(TOP_K):
        y = y + p[:, k:k + 1] * ybuf[k]
    o_ref[...] = _rms(y, g_ref[...])


def _combine(dest3, y_rows, x1, pnat, g_final):
    T, D = x1.shape
    tm = COMBINE_TM
    return pl.pallas_call(
        _combine_kernel,
        grid=(T // tm,),
        in_specs=[
            pl.BlockSpec((None, TOP_K, tm), lambda i: (i, 0, 0), memory_space=pltpu.SMEM),
            pl.BlockSpec(memory_space=pl.ANY),
            pl.BlockSpec((tm, D), lambda i: (i, 0)),
            pl.BlockSpec((tm, LANES), lambda i: (i, 0)),
            pl.BlockSpec((1, D), lambda i: (0, 0)),
        ],
        out_specs=pl.BlockSpec((tm, D), lambda i: (i, 0)),
        out_shape=jax.ShapeDtypeStruct((T, D), F32),
        scratch_shapes=[pltpu.VMEM((TOP_K, tm, D), F32), pltpu.SemaphoreType.DMA(())],
        compiler_params=_cparams(("arbitrary",)),
        name="moe_combine",
    )(dest3, y_rows, x1, pnat, g_final)


def _rope_tables(S, d):
    L = S // d
    inv = jnp.float32(ROPE_THETA) ** (-jnp.arange(ROT_HALF, dtype=F32) / ROT_HALF)
    ang = jnp.arange(S).astype(F32)[:, None] * inv[None, :]
    cos = jnp.cos(ang).reshape(L, d, ROT_HALF).transpose(1, 0, 2)
    sin = jnp.sin(ang).reshape(L, d, ROT_HALF).transpose(1, 0, 2)
    cq, sq = cos.transpose(0, 2, 1), sin.transpose(0, 2, 1)
    one = jnp.ones((d, L, HEAD_DIM - 2 * ROT_HALF), F32)
    zero = jnp.zeros((d, L, HEAD_DIM - ROT_HALF), F32)
    ck = jnp.concatenate([cos, cos, one], axis=-1)
    s1 = jnp.concatenate([-sin, zero], axis=-1)
    s2 = jnp.concatenate([zero[..., :ROT_HALF], sin, zero[..., :HEAD_DIM - 2 * ROT_HALF]], axis=-1)
    rep = LANES // HEAD_DIM
    return cq, sq, jnp.tile(ck, (1, 1, rep)), jnp.tile(s1, (1, 1, rep)), jnp.tile(s2, (1, 1, rep))


def _layer(x, ln_mix_g, w_in, b_gate_a, b_gate_b, w_o_a, w_o_b, w_out, ln_ffn_g, w_router, b_router,
           w_gate, b_gate, w_up, b_up, w_down, b_down, ln_out_g):
    B, S, D = x.shape
    T = B * S
    qa0, ka0, va0 = 0, WIDTH_A, 2 * WIDTH_A
    qb0, kb0, vb0 = 3 * WIDTH_A, 3 * WIDTH_A + WIDTH_B, 3 * WIDTH_A + 2 * WIDTH_B
    g0 = 3 * WIDTH_A + 3 * WIDTH_B
    cols = lambda s, w: w_in[:, s:s + w]
    gmix = ln_mix_g.reshape(1, D)

    obs, lses = [], []
    o_a = None
    for gi, (_, d) in enumerate(DIL_PAIRS):
        off = gi * WIDTH_G
        wq, wk, wv = cols(qb0 + off, WIDTH_G), cols(kb0 + off, WIDTH_G), cols(vb0 + off, WIDTH_G)
        na = 0
        if gi == 0:
            na = WIDTH_A
            wq = jnp.concatenate([cols(qa0, WIDTH_A), wq], axis=1)
            wk = jnp.concatenate([cols(ka0, WIDTH_A), wk], axis=1)
            wv = jnp.concatenate([cols(va0, WIDTH_A), wv], axis=1)
        outs = _project(x, gmix, wq.T.astype(BF16), wk.astype(BF16), wv.T.astype(BF16),
                        _rope_tables(S, d), d, na)
        if gi == 0:
            qTa, vTa, ka, kmean = outs[:4]
            outs = outs[4:]
            o_a = _moba(qTa, ka, vTa, kmean.reshape(B, S // MOBA_BLOCK, WIDTH_A))
        qTb, vTb, kb = outs
        o_g, lse_g = _dilated(qTb, kb, vTb)
        obs.append(o_g.reshape(T, WIDTH_G))
        lses.append(lse_g.reshape(T, WIDTH_G))

    x1, h2, idxT, _, rankT, pnat, cnt = _post(
        x.reshape(T, D), o_a.reshape(T, WIDTH_A), obs, lses, gmix,
        cols(g0, 2 * D).astype(BF16), b_gate_a.reshape(1, D), b_gate_b.reshape(1, D),
        w_o_a.astype(BF16), w_o_b.astype(BF16), w_out.astype(BF16), ln_ffn_g.reshape(1, D),
        w_router.T, b_router.reshape(N_EXPERTS, 1))

    br = EXPERT_ROWS
    counts = cnt[:, 0].astype(jnp.int32)
    padded = (counts + br - 1) // br * br
    pend = jnp.cumsum(padded)
    pstart = pend - padded
    dest = jnp.take(pstart, idxT[:TOP_K]) + rankT[:TOP_K]
    n_rows = T * TOP_K + N_EXPERTS * br
    nblk = n_rows // br
    n_used = pend[-1] // br
    be = jnp.minimum(jnp.searchsorted(pend, jnp.arange(nblk, dtype=jnp.int32) * br, side='right'), N_EXPERTS - 1)
    be = jnp.where(jnp.arange(nblk) < n_used, be, be[n_used - 1]).astype(jnp.int32)

    dest_d = dest.reshape(TOP_K, T // DISPATCH_TM, DISPATCH_TM).transpose(1, 0, 2)
    x_rows = _dispatch(pend.astype(jnp.int32), padded.astype(jnp.int32), dest_d, h2, n_rows)
    y_rows = _experts(be, n_used.reshape(1).astype(jnp.int32), x_rows, w_gate, b_gate, w_up, b_up, w_down, b_down)
    dest_c = dest.reshape(TOP_K, T // COMBINE_TM, COMBINE_TM).transpose(1, 0, 2)
    out = _combine(dest_c, y_rows, x1, pnat, ln_out_g.reshape(1, D))
    return out.reshape(B, S, D)


def kernel(x, ln_mix_g, w_in, b_gate_a, b_gate_b, w_o_a, w_o_b, w_out, ln_ffn_g, w_router, b_router,
           w_gate, b_gate, w_up, b_up, w_down, b_down, ln_final_g):
    depth = ln_mix_g.shape[0]
    assert depth == 1, "the final RMSNorm is fused into the last layer's combine"
    return _layer(x, ln_mix_g[0], w_in[0], b_gate_a[0], b_gate_b[0], w_o_a[0], w_o_b[0], w_out[0],
                  ln_ffn_g[0], w_router[0], b_router[0], w_gate[0], b_gate[0], w_up[0], b_up[0],
                  w_down[0], b_down[0], ln_final_g)
```

```python
import functools

import jax
import jax.numpy as jnp
from jax import lax
from jax.experimental import pallas as pl
from jax.experimental.pallas import tpu as pltpu

D_MODEL = 1024
HEAD_DIM = 64
ROT_HALF = HEAD_DIM // 8
ROPE_THETA = 500000.0
N_HEADS_A = 8
MOBA_BLOCK = 256
MOBA_TOPK = 3
MOBA_QBLOCKS = 2
DIL_PAIRS = ((128, 1), (512, 4), (2048, 16))
DIL_BAND = 128
HEADS_PER_GROUP_B = 4
WIDTH_A = N_HEADS_A * HEAD_DIM
WIDTH_G = HEADS_PER_GROUP_B * HEAD_DIM
WIDTH_B = WIDTH_G * len(DIL_PAIRS)
N_EXPERTS = 32
TOP_K = 4
SWIGLU_LIMIT = 7.0
SWIGLU_ALPHA = 1.702
NORM_EPS = 1e-5
SCALE = HEAD_DIM ** -0.5
LOG2E = 1.4426950408889634

LANES = 128
PROJ_TM = 512
DIL_TL = 512
POST_TM = 512
EXPERT_ROWS = 256
DISPATCH_TM = 512
COMBINE_TM = 256
VMEM_LIMIT = 56 * 1024 * 1024
NEG = -1e30

BF16 = jnp.bfloat16
F32 = jnp.float32


def _dot(a, b):
    return jnp.dot(a, b, preferred_element_type=F32)


def _dot_nt(a, b):
    return lax.dot_general(a, b, (((1,), (1,)), ((), ())), preferred_element_type=F32)


def _rms(x, g):
    ms = jnp.mean(x * x, axis=-1, keepdims=True)
    return x * lax.rsqrt(ms + NORM_EPS) * g


def _cparams(sem):
    return pltpu.CompilerParams(dimension_semantics=sem, vmem_limit_bytes=VMEM_LIMIT)


def _proj_kernel(x_ref, g_ref, wq_ref, wk_ref, wv_ref, cq_ref, sq_ref, ck_ref, s1_ref, s2_ref, *outs, na):
    tm = x_ref.shape[0]
    h = _rms(x_ref[...], g_ref[...]).astype(BF16)
    nq = wq_ref.shape[0]
    heads = nq // HEAD_DIM

    qT = _dot_nt(wq_ref[...], h)
    q3 = qT.reshape(heads, HEAD_DIM, tm)
    c = cq_ref[...][None]
    s = sq_ref[...][None]
    x1 = q3[:, 0:ROT_HALF]
    x2 = q3[:, ROT_HALF:2 * ROT_HALF]
    q3 = jnp.concatenate([x1 * c - x2 * s, x2 * c + x1 * s, q3[:, 2 * ROT_HALF:]], axis=1)
    hid = lax.broadcasted_iota(jnp.int32, (heads, 1, 1), 0)
    q3 = q3 * jnp.where(hid < na // HEAD_DIM, SCALE * LOG2E, SCALE)
    qT = q3.reshape(nq, tm).astype(BF16)

    vT = _dot_nt(wv_ref[...], h).astype(BF16)

    kk = _dot(h, wk_ref[...])
    ck, s1, s2 = ck_ref[...], s1_ref[...], s2_ref[...]
    kparts = []
    for gi in range(kk.shape[1] // LANES):
        kg = kk[:, gi * LANES:(gi + 1) * LANES]
        kparts.append(kg * ck + pltpu.roll(kg, LANES - ROT_HALF, 1) * s1 + pltpu.roll(kg, ROT_HALF, 1) * s2)

    if na:
        qa_ref, va_ref, ka_ref, km_ref, qb_ref, vb_ref, kb_ref = outs
        for blk in range(tm // MOBA_BLOCK):
            sl = slice(blk * MOBA_BLOCK, (blk + 1) * MOBA_BLOCK)
            qa_ref[blk] = qT[0:na, sl]
            va_ref[blk] = vT[0:na, sl]
        lane = lax.broadcasted_iota(jnp.int32, (tm, LANES), 1)
        row = lax.broadcasted_iota(jnp.int32, (tm, LANES), 0)
        blkid = pl.program_id(2) * (tm // MOBA_BLOCK) + row // MOBA_BLOCK
        for gi in range(na // LANES):
            kg = kparts[gi]
            for e in range(2):
                in_head = (lane >= HEAD_DIM * e) & (lane < HEAD_DIM * (e + 1))
                onehot = (lane - HEAD_DIM * (1 - e)) == blkid
                col = (2 * gi + e) * LANES
                ka_ref[:, col:col + LANES] = jnp.where(in_head, kg, onehot.astype(F32)).astype(BF16)
            km = kg.reshape(tm // MOBA_BLOCK, MOBA_BLOCK, LANES).sum(axis=1) * (1.0 / MOBA_BLOCK)
            km_ref[:, gi * LANES:(gi + 1) * LANES] = km
    else:
        qb_ref, vb_ref, kb_ref = outs
    for cb in range(tm // DIL_BAND):
        sl = slice(cb * DIL_BAND, (cb + 1) * DIL_BAND)
        qb_ref[cb] = qT[na:na + WIDTH_G, sl]
        vb_ref[cb] = vT[na:na + WIDTH_G, sl]
    for gi in range(WIDTH_G // LANES):
        kb_ref[:, gi * LANES:(gi + 1) * LANES] = kparts[na // LANES + gi].astype(BF16)


def _project(x, g, wqT, wk, wvT, tabs, d, na):
    B, S, D = x.shape
    L = S // d
    tm = min(PROJ_TM, L)
    cq, sq, ck, s1, s2 = tabs
    xv = x.reshape(B, L, d * D)
    grid = (B, d, L // tm)
    full = lambda a: pl.BlockSpec(a.shape, lambda b, r, i: (0,) * a.ndim)
    in_specs = [
        pl.BlockSpec((None, tm, D), lambda b, r, i: (b, i, r)),
        full(g), full(wqT), full(wk), full(wvT),
        pl.BlockSpec((None, ROT_HALF, tm), lambda b, r, i: (r, 0, i)),
        pl.BlockSpec((None, ROT_HALF, tm), lambda b, r, i: (r, 0, i)),
        pl.BlockSpec((None, tm, LANES), lambda b, r, i: (r, i, 0)),
        pl.BlockSpec((None, tm, LANES), lambda b, r, i: (r, i, 0)),
        pl.BlockSpec((None, tm, LANES), lambda b, r, i: (r, i, 0)),
    ]
    nsub = tm // DIL_BAND
    out_shape = [
        jax.ShapeDtypeStruct((B, d, L // DIL_BAND, WIDTH_G, DIL_BAND), BF16),
        jax.ShapeDtypeStruct((B, d, L // DIL_BAND, WIDTH_G, DIL_BAND), BF16),
        jax.ShapeDtypeStruct((B, d, L, WIDTH_G), BF16),
    ]
    out_specs = [
        pl.BlockSpec((None, None, nsub, WIDTH_G, DIL_BAND), lambda b, r, i: (b, r, i, 0, 0)),
        pl.BlockSpec((None, None, nsub, WIDTH_G, DIL_BAND), lambda b, r, i: (b, r, i, 0, 0)),
        pl.BlockSpec((None, None, tm, WIDTH_G), lambda b, r, i: (b, r, i, 0)),
    ]
    if na:
        nblk = tm // MOBA_BLOCK
        out_shape = [
            jax.ShapeDtypeStruct((B, S // MOBA_BLOCK, na, MOBA_BLOCK), BF16),
            jax.ShapeDtypeStruct((B, S // MOBA_BLOCK, na, MOBA_BLOCK), BF16),
            jax.ShapeDtypeStruct((B, S, 2 * na), BF16),
            jax.ShapeDtypeStruct((B, S // tm, nblk, na), F32),
        ] + out_shape
        out_specs = [
            pl.BlockSpec((None, nblk, na, MOBA_BLOCK), lambda b, r, i: (b, i, 0, 0)),
            pl.BlockSpec((None, nblk, na, MOBA_BLOCK), lambda b, r, i: (b, i, 0, 0)),
            pl.BlockSpec((None, tm, 2 * na), lambda b, r, i: (b, i, 0)),
            pl.BlockSpec((None, None, nblk, na), lambda b, r, i: (b, i, 0, 0)),
        ] + out_specs
    return pl.pallas_call(
        functools.partial(_proj_kernel, na=na),
        grid=grid, in_specs=in_specs, out_specs=out_specs, out_shape=out_shape,
        compiler_params=_cparams(("arbitrary",) * 3),
        name=f"proj_d{d}",
    )(xv, g, wqT, wk, wvT, cq, sq, ck, s1, s2)


def _moba_kernel(q_ref, k_ref, v_ref, km_ref, o_ref, qa_scr, s_scr, p_scr):
    g = pl.program_id(2)
    nb = km_ref.shape[0]
    blk = MOBA_BLOCK
    tq = MOBA_QBLOCKS * blk
    q2 = jnp.concatenate([q_ref[i] for i in range(MOBA_QBLOCKS)], axis=1)
    bid = lax.broadcasted_iota(jnp.int32, (nb, tq), 0)
    qblk = MOBA_QBLOCKS * g + lax.broadcasted_iota(jnp.int32, (nb, tq), 1) // blk
    key_i = lax.broadcasted_iota(jnp.int32, (blk, tq), 0)
    qry_i = lax.broadcasted_iota(jnp.int32, (blk, tq), 1)
    km = km_ref[...].astype(BF16)
    zq = jnp.zeros((HEAD_DIM, tq), BF16)
    zb = jnp.zeros((HEAD_DIM - nb, tq), BF16)

    state = []
    for h in range(2):
        hl = slice(LANES * h, LANES * (h + 1))
        vl = slice(HEAD_DIM * h, HEAD_DIM * (h + 1))
        qh = q2[HEAD_DIM * h:HEAD_DIM * (h + 1)]
        q_plain = jnp.concatenate([qh, zq] if h == 0 else [zq, qh], axis=0)
        gt = jnp.where(bid < qblk, _dot(km, q_plain), -jnp.inf)
        sel = jnp.zeros(gt.shape, jnp.bool_)
        for _ in range(MOBA_TOPK):
            mx = jnp.max(gt, axis=0, keepdims=True)
            first = jnp.min(jnp.where((gt == mx) & (mx > -jnp.inf), bid, nb), axis=0, keepdims=True)
            pick = bid == first
            sel = sel | pick
            gt = jnp.where(pick, -jnp.inf, gt)
        bias = jnp.where(sel, 0.0, NEG).astype(BF16)
        qa_scr[h] = jnp.concatenate([qh, bias, zb] if h == 0 else [bias, zb, qh], axis=0)
        bias_d = jnp.where(sel | (bid == qblk), 0.0, NEG).astype(BF16)
        q_diag = jnp.concatenate([qh, bias_d, zb] if h == 0 else [bias_d, zb, qh], axis=0)
        ss = []
        for i in range(MOBA_QBLOCKS):
            j = MOBA_QBLOCKS * g + i
            sd = _dot(k_ref[pl.ds(pl.multiple_of(j * blk, blk), blk), hl], q_diag)
            own = (qry_i >= i * blk) & (qry_i < (i + 1) * blk)
            ss.append(jnp.where(own & (key_i > qry_i - i * blk), NEG, sd))
        m = ss[0].max(axis=0, keepdims=True)
        for sd in ss[1:]:
            m = jnp.maximum(m, sd.max(axis=0, keepdims=True))
        l = jnp.zeros((1, tq), F32)
        acc = jnp.zeros((HEAD_DIM, tq), F32)
        for i, sd in enumerate(ss):
            p = jnp.exp2(sd - m)
            l = l + jnp.sum(p, axis=0, keepdims=True)
            acc = acc + _dot(v_ref[MOBA_QBLOCKS * g + i, vl, :], p.astype(BF16))
        state += [m, l, acc]

    def blocks(c):
        return [jnp.where((c >= 0) & (c < g), MOBA_QBLOCKS * c + i, nb - 1) for i in range(MOBA_QBLOCKS)]

    def body(it, carry):
        alphas, ms, ls, accs = carry
        jn = blocks(it)
        jv = blocks(it - 2)
        a_new, m_new, l_new, acc_new = [], [], [], []
        for h in range(2):
            hl = slice(LANES * h, LANES * (h + 1))
            vl = slice(HEAD_DIM * h, HEAD_DIM * (h + 1))
            qa = qa_scr[h]
            s_next = [_dot(k_ref[pl.ds(pl.multiple_of(j * blk, blk), blk), hl], qa) for j in jn]
            s_cur = [s_scr[h, i] for i in range(MOBA_QBLOCKS)]
            p_old = [p_scr[h, i] for i in range(MOBA_QBLOCKS)]
            mn = ms[h]
            for sc in s_cur:
                mn = jnp.maximum(mn, jnp.max(sc, axis=0, keepdims=True))
            alpha = jnp.exp2(ms[h] - mn)
            l = alpha * ls[h]
            acc = alphas[h] * accs[h]
            for i in range(MOBA_QBLOCKS):
                p = jnp.exp2(s_cur[i] - mn)
                l = l + jnp.sum(p, axis=0, keepdims=True)
                acc = acc + _dot(v_ref[jv[i], vl, :], p_old[i])
                s_scr[h, i] = s_next[i]
                p_scr[h, i] = p.astype(BF16)
            m_new.append(mn)
            l_new.append(l)
            a_new.append(alpha)
            acc_new.append(acc)
        return (tuple(a_new), tuple(m_new), tuple(l_new), tuple(acc_new))

    s_scr[...] = jnp.full(s_scr.shape, NEG, F32)
    p_scr[...] = jnp.zeros(p_scr.shape, BF16)
    one = jnp.ones((1, tq), F32)
    init = ((one, one), (state[0], state[3]), (state[1], state[4]), (state[2], state[5]))
    _, _, ls, accs = lax.fori_loop(0, g + 2, body, init)
    oT = jnp.concatenate([accs[0] / ls[0], accs[1] / ls[1]], axis=0)
    o_ref[...] = oT.T.astype(o_ref.dtype)


def _moba(qT, k, vT, kmean):
    B, nb, wa, blk = qT.shape
    S = nb * blk
    hp = wa // LANES
    nq = MOBA_QBLOCKS
    assert nb % 16 == 0 and nb <= HEAD_DIM, "block-mask rows must fit the spare half of a head pair"
    return pl.pallas_call(
        _moba_kernel,
        grid=(B, hp, nb // nq),
        in_specs=[
            pl.BlockSpec((None, nq, LANES, blk), lambda b, p, n: (b, n, p, 0)),
            pl.BlockSpec((None, S, 2 * LANES), lambda b, p, n: (b, 0, p)),
            pl.BlockSpec((None, nb, LANES, blk), lambda b, p, n: (b, 0, p, 0)),
            pl.BlockSpec((None, nb, LANES), lambda b, p, n: (b, 0, p)),
        ],
        out_specs=pl.BlockSpec((None, nq * blk, LANES), lambda b, p, n: (b, n, p)),
        out_shape=jax.ShapeDtypeStruct((B, S, wa), BF16),
        scratch_shapes=[pltpu.VMEM((2, LANES, nq * blk), BF16), pltpu.VMEM((2, nq, blk, nq * blk), F32),
                        pltpu.VMEM((2, nq, blk, nq * blk), BF16)],
        compiler_params=_cparams(("arbitrary",) * 3),
        name="moba_attn",
    )(qT, k, vT, kmean)


def _dil_kernel(q_ref, k_ref, kp_ref, v_ref, vp_ref, o_ref, lse_ref):
    t = pl.program_id(2)
    band = DIL_BAND
    nsub = q_ref.shape[0]
    key_i = lax.broadcasted_iota(jnp.int32, (band, band), 0)
    qry_i = lax.broadcasted_iota(jnp.int32, (band, band), 1)
    rowid = lax.broadcasted_iota(jnp.int32, (LANES, band), 0)
    own_ok = key_i <= qry_i
    prev_ok = key_i >= qry_i
    for c in range(nsub):
        o_parts, l_parts = [], []
        for hp in range(WIDTH_G // LANES):
            cols = slice(hp * LANES, (hp + 1) * LANES)
            k_own = k_ref[c * band:(c + 1) * band, cols]
            k_prev = kp_ref[:, cols] if c == 0 else k_ref[(c - 1) * band:c * band, cols]
            q2 = q_ref[c, cols, :]
            for h in range(2):
                qh = jnp.where((rowid >= HEAD_DIM * h) & (rowid < HEAD_DIM * (h + 1)), q2, jnp.zeros_like(q2))
                pmask = (prev_ok & (t > 0)) if c == 0 else prev_ok
                s_own = jnp.where(own_ok, _dot(k_own, qh), -jnp.inf)
                s_prev = jnp.where(pmask, _dot(k_prev, qh), -jnp.inf)
                m = jnp.maximum(jnp.max(s_own, axis=0, keepdims=True), jnp.max(s_prev, axis=0, keepdims=True))
                p_own = jnp.exp(s_own - m)
                p_prev = jnp.exp(s_prev - m)
                l = jnp.sum(p_own, axis=0, keepdims=True) + jnp.sum(p_prev, axis=0, keepdims=True)
                rows = slice(hp * LANES + h * HEAD_DIM, hp * LANES + (h + 1) * HEAD_DIM)
                v_own = v_ref[c, rows, :]
                v_prev = vp_ref[rows, :] if c == 0 else v_ref[c - 1, rows, :]
                oT = _dot(v_own, p_own.astype(BF16)) + _dot(v_prev, p_prev.astype(BF16))
                o_parts.append(oT / l)
                l_parts.append(jnp.broadcast_to(m + jnp.log(l), (HEAD_DIM, band)))
        o_ref[c * band:(c + 1) * band, :] = jnp.concatenate(o_parts, axis=0).T
        lse_ref[c * band:(c + 1) * band, :] = jnp.concatenate(l_parts, axis=0).T


def _dilated(qT, k, vT):
    B, d, nblk, wg, band = qT.shape
    L = nblk * band
    tl = min(DIL_TL, L)
    nsub = tl // band
    prev = lambda t: jnp.maximum(t * nsub - 1, 0)
    return pl.pallas_call(
        _dil_kernel,
        grid=(B, d, L // tl),
        in_specs=[
            pl.BlockSpec((None, None, nsub, wg, band), lambda b, r, t: (b, r, t, 0, 0)),
            pl.BlockSpec((None, None, tl, wg), lambda b, r, t: (b, r, t, 0)),
            pl.BlockSpec((None, None, band, wg), lambda b, r, t: (b, r, prev(t), 0)),
            pl.BlockSpec((None, None, nsub, wg, band), lambda b, r, t: (b, r, t, 0, 0)),
            pl.BlockSpec((None, None, None, wg, band), lambda b, r, t: (b, r, prev(t), 0, 0)),
        ],
        out_specs=[
            pl.BlockSpec((None, tl, wg), lambda b, r, t: (b, t, r)),
            pl.BlockSpec((None, tl, wg), lambda b, r, t: (b, t, r)),
        ],
        out_shape=[jax.ShapeDtypeStruct((B, L, d * wg), F32)] * 2,
        compiler_params=_cparams(("arbitrary",) * 3),
        name=f"dilated_d{d}",
    )(qT, k, k, vT, vT)


def _post_kernel(x_ref, oa_ref, o1_ref, o2_ref, o3_ref, l1_ref, l2_ref, l3_ref, gmix_ref, wg_ref, bga_ref,
                 bgb_ref, woa_ref, wob_ref, wout_ref, gffn_ref, wr_ref, br_ref,
                 x1_ref, h2_ref, idx_ref, p_ref, rank_ref, pnat_ref, cnt_ref, carry_scr):
    i = pl.program_id(0)
    tm, D = x_ref.shape
    ne = wr_ref.shape[0]

    @pl.when(i == 0)
    def _():
        carry_scr[...] = jnp.zeros_like(carry_scr)

    x = x_ref[...]
    h = _rms(x, gmix_ref[...]).astype(BF16)
    gates = _dot(h, wg_ref[...])
    ga = gates[:, :D] + bga_ref[...]
    gb = gates[:, D:] + bgb_ref[...]
    l1, l2, l3 = l1_ref[...], l2_ref[...], l3_ref[...]
    mx = jnp.maximum(jnp.maximum(l1, l2), l3)
    e1, e2, e3 = jnp.exp(l1 - mx), jnp.exp(l2 - mx), jnp.exp(l3 - mx)
    ob = (e1 * o1_ref[...] + e2 * o2_ref[...] + e3 * o3_ref[...]) / (e1 + e2 + e3)
    ya = _dot(oa_ref[...], woa_ref[...])
    yb = _dot(ob.astype(BF16), wob_ref[...])
    mix = jax.nn.sigmoid(ga) * ya + jax.nn.sigmoid(gb) * yb
    x1 = x + _dot(mix.astype(BF16), wout_ref[...])
    x1_ref[...] = x1
    h2 = _rms(x1, gffn_ref[...])
    h2_ref[...] = h2

    logits = lax.dot_general(wr_ref[...], h2, (((1,), (1,)), ((), ())), preferred_element_type=F32,
                             precision=lax.Precision.HIGHEST) + br_ref[...]
    eid = lax.broadcasted_iota(jnp.int32, (ne, tm), 0)
    g = logits
    vals, idxs, picks = [], [], []
    for _ in range(TOP_K):
        m = jnp.max(g, axis=0, keepdims=True)
        first = jnp.min(jnp.where(g == m, eid, ne), axis=0, keepdims=True)
        pick = eid == first
        vals.append(m)
        idxs.append(first)
        picks.append(pick)
        g = jnp.where(pick, -jnp.inf, g)
    es = [jnp.exp(v - vals[0]) for v in vals]
    den = es[0] + es[1] + es[2] + es[3]
    probs = [e / den for e in es]

    onehot = jnp.zeros((ne, tm), F32)
    for pick in picks:
        onehot = onehot + pick.astype(F32)
    earlier = (lax.broadcasted_iota(jnp.int32, (tm, tm), 0) < lax.broadcasted_iota(jnp.int32, (tm, tm), 1))
    prefix = _dot(onehot.astype(BF16), earlier.astype(BF16)) + carry_scr[:, 0:1]
    ranks = [jnp.sum(jnp.where(pick, prefix, 0.0), axis=0, keepdims=True) for pick in picks]
    carry_scr[...] = carry_scr[...] + jnp.sum(onehot, axis=1, keepdims=True)
    cnt_ref[...] = carry_scr[...]

    zi = jnp.zeros((8 - TOP_K, tm), jnp.int32)
    idx_ref[...] = jnp.concatenate(idxs + [zi], axis=0)
    rank_ref[...] = jnp.concatenate([r.astype(jnp.int32) for r in ranks] + [zi], axis=0)
    p_ref[...] = jnp.concatenate(probs + [jnp.zeros((8 - TOP_K, tm), F32)], axis=0)
    pnat_ref[...] = jnp.concatenate(probs + [jnp.zeros((LANES - TOP_K, tm), F32)], axis=0).T


def _post(x2, oa, obs, lses, gmix, wg, bga, bgb, woa, wob, wout, gffn, wrT, br):
    T, D = x2.shape
    tm = POST_TM
    ne = wrT.shape[0]
    row = lambda w: pl.BlockSpec((tm, w), lambda i: (i, 0))
    full = lambda a: pl.BlockSpec(a.shape, lambda i: (0,) * a.ndim)
    col = pl.BlockSpec((8, tm), lambda i: (0, i))
    return pl.pallas_call(
        _post_kernel,
        grid=(T // tm,),
        in_specs=[row(D), row(WIDTH_A)] + [row(WIDTH_G)] * 6
                 + [full(a) for a in (gmix, wg, bga, bgb, woa, wob, wout, gffn, wrT, br)],
        out_specs=[row(D), row(D), col, col, col, row(LANES), pl.BlockSpec((ne, LANES), lambda i: (0, 0))],
        out_shape=[
            jax.ShapeDtypeStruct((T, D), F32), jax.ShapeDtypeStruct((T, D), F32),
            jax.ShapeDtypeStruct((8, T), jnp.int32), jax.ShapeDtypeStruct((8, T), F32),
            jax.ShapeDtypeStruct((8, T), jnp.int32), jax.ShapeDtypeStruct((T, LANES), F32),
            jax.ShapeDtypeStruct((ne, LANES), F32),
        ],
        scratch_shapes=[pltpu.VMEM((ne, LANES), F32)],
        compiler_params=_cparams(("arbitrary",)),
        name="post_mix_router",
    )(x2, oa, *obs, *lses, gmix, wg, bga, bgb, woa, wob, wout, gffn, wrT, br)


def _dispatch_kernel(pend_ref, padded_ref, dest_ref, h_ref, xr_ref, zero_scr, sem):
    i = pl.program_id(0)
    tm = h_ref.shape[0]
    br = zero_scr.shape[0]

    def zero_copy(start):
        return pltpu.make_async_copy(zero_scr, xr_ref.at[pl.ds(pl.multiple_of(start, br), br), :], sem)

    @pl.when(i == 0)
    def _():
        zero_scr[...] = jnp.zeros_like(zero_scr)
        n_used = pend_ref[N_EXPERTS - 1] // br
        n_blk = xr_ref.shape[0] // br

        def tail(fn):
            def go(e, c):
                @pl.when(padded_ref[e] > 0)
                def _():
                    fn(zero_copy(pend_ref[e] - br))
                return c
            return go

        def unused(fn):
            def go(b, c):
                fn(zero_copy(b * br))
                return c
            return go

        lax.fori_loop(0, N_EXPERTS, tail(lambda cp: cp.start()), 0)
        lax.fori_loop(n_used, n_blk, unused(lambda cp: cp.start()), 0)
        lax.fori_loop(0, N_EXPERTS, tail(lambda cp: cp.wait()), 0)
        lax.fori_loop(n_used, n_blk, unused(lambda cp: cp.wait()), 0)

    def row_copy(t, k):
        return pltpu.make_async_copy(h_ref.at[pl.ds(t, 1), :], xr_ref.at[pl.ds(dest_ref[k, t], 1), :], sem)

    def start(t, c):
        for k in range(TOP_K):
            row_copy(t, k).start()
        return c

    def wait(t, c):
        for k in range(TOP_K):
            row_copy(t, k).wait()
        return c

    lax.fori_loop(0, tm, start, 0)
    lax.fori_loop(0, tm, wait, 0)


def _dispatch(pend, padded, dest3, h2, n_rows):
    T, D = h2.shape
    tm = DISPATCH_TM
    return pl.pallas_call(
        _dispatch_kernel,
        grid_spec=pltpu.PrefetchScalarGridSpec(
            num_scalar_prefetch=2,
            grid=(T // tm,),
            in_specs=[
                pl.BlockSpec((None, TOP_K, tm), lambda i, pe, pa: (i, 0, 0), memory_space=pltpu.SMEM),
                pl.BlockSpec((tm, D), lambda i, pe, pa: (i, 0)),
            ],
            out_specs=pl.BlockSpec(memory_space=pl.ANY),
            scratch_shapes=[pltpu.VMEM((EXPERT_ROWS, D), F32), pltpu.SemaphoreType.DMA(())],
        ),
        out_shape=jax.ShapeDtypeStruct((n_rows, D), F32),
        compiler_params=_cparams(("arbitrary",)),
        name="moe_dispatch",
    )(pend, padded, dest3, h2)


def _expert_kernel(be_ref, nu_ref, x_ref, wg_ref, bg_ref, wu_ref, bu_ref, wd_ref, bd_ref, y_ref,
                   wgb_scr, wub_scr, wdb_scr):
    b = pl.program_id(0)
    used = b < nu_ref[0]
    new_expert = (b == 0) | (be_ref[b] != be_ref[jnp.maximum(b - 1, 0)])

    @pl.when(used & new_expert)
    def _():
        wgb_scr[...] = wg_ref[...].astype(BF16)
        wub_scr[...] = wu_ref[...].astype(BF16)
        wdb_scr[...] = wd_ref[...].astype(BF16)

    @pl.when(used)
    def _():
        x = x_ref[...].astype(BF16)
        g = _dot(x, wgb_scr[...]) + bg_ref[...]
        u = _dot(x, wub_scr[...]) + bu_ref[...]
        g = jnp.minimum(g, SWIGLU_LIMIT)
        u = jnp.clip(u, -SWIGLU_LIMIT, SWIGLU_LIMIT)
        a = g * jax.nn.sigmoid(SWIGLU_ALPHA * g) * (u + 1.0)
        y_ref[...] = _dot(a.astype(BF16), wdb_scr[...]) + bd_ref[...]

    @pl.when(jnp.logical_not(used))
    def _():
        y_ref[...] = jnp.zeros_like(y_ref)


def _experts(blk_expert, n_used, x_rows, w_gate, b_gate, w_up, b_up, w_down, b_down):
    n_rows, D = x_rows.shape
    E, _, F = w_gate.shape
    br = EXPERT_ROWS
    wspec = lambda shape: pl.BlockSpec((None,) + shape, lambda b, be, nu: (be[b], 0, 0))
    return pl.pallas_call(
        _expert_kernel,
        grid_spec=pltpu.PrefetchScalarGridSpec(
            num_scalar_prefetch=2,
            grid=(n_rows // br,),
            in_specs=[
                pl.BlockSpec((br, D), lambda b, be, nu: (jnp.minimum(b, nu[0] - 1), 0)),
                wspec((D, F)), wspec((1, F)), wspec((D, F)), wspec((1, F)), wspec((F, D)), wspec((1, D)),
            ],
            out_specs=pl.BlockSpec((br, D), lambda b, be, nu: (b, 0)),
            scratch_shapes=[pltpu.VMEM((D, F), BF16), pltpu.VMEM((D, F), BF16), pltpu.VMEM((F, D), BF16)],
        ),
        out_shape=jax.ShapeDtypeStruct((n_rows, D), F32),
        compiler_params=_cparams(("arbitrary",)),
        name="moe_experts",
    )(blk_expert, n_used, x_rows, w_gate, b_gate.reshape(E, 1, F), w_up, b_up.reshape(E, 1, F),
      w_down, b_down.reshape(E, 1, D))


def _combine_kernel(dest_ref, y_ref, x1_ref, p_ref, g_ref, o_ref, ybuf, sem):
    tm = x1_ref.shape[0]

    def row_copy(t, k):
        return pltpu.make_async_copy(y_ref.at[pl.ds(dest_ref[k, t], 1), :], ybuf.at[k, pl.ds(t, 1), :], sem)

    def start(t, c):
        for k in range(TOP_K):
            row_copy(t, k).start()
        return c

    def wait(t, c):
        for k in range(TOP_K):
            row_copy(t, k).wait()
        return c

    lax.fori_loop(0, tm, start, 0)
    lax.fori_loop(0, tm, wait, 0)
    p = p_ref[...]
    y = x1_ref[...]
    for k in range(TOP_K):
        y = y + p[:, k:k + 1] * ybuf[k]
    o_ref[...] = _rms(y, g_ref[...])


def _combine(dest3, y_rows, x1, pnat, g_final):
    T, D = x1.shape
    tm = COMBINE_TM
    return pl.pallas_call(
        _combine_kernel,
        grid=(T // tm,),
        in_specs=[
            pl.BlockSpec((None, TOP_K, tm), lambda i: (i, 0, 0), memory_space=pltpu.SMEM),
            pl.BlockSpec(memory_space=pl.ANY),
            pl.BlockSpec((tm, D), lambda i: (i, 0)),
            pl.BlockSpec((tm, LANES), lambda i: (i, 0)),
            pl.BlockSpec((1, D), lambda i: (0, 0)),
        ],
        out_specs=pl.BlockSpec((tm, D), lambda i: (i, 0)),
        out_shape=jax.ShapeDtypeStruct((T, D), F32),
        scratch_shapes=[pltpu.VMEM((TOP_K, tm, D), F32), pltpu.SemaphoreType.DMA(())],
        compiler_params=_cparams(("arbitrary",)),
        name="moe_combine",
    )(dest3, y_rows, x1, pnat, g_final)


def _rope_tables(S, d):
    L = S // d

    def angle(shape, r_ax, i_ax, f):
        pos = lax.broadcasted_iota(jnp.int32, shape, i_ax) * d + lax.broadcasted_iota(jnp.int32, shape, r_ax)
        inv = jnp.float32(ROPE_THETA) ** (-f.astype(F32) / ROT_HALF)
        return pos.astype(F32) * inv

    sh = (d, ROT_HALF, L)
    ang = angle(sh, 0, 2, lax.broadcasted_iota(jnp.int32, sh, 1))
    cq, sq = jnp.cos(ang), jnp.sin(ang)
    sh = (d, L, LANES)
    lane = lax.broadcasted_iota(jnp.int32, sh, 2) % HEAD_DIM
    ang = angle(sh, 0, 1, lane % ROT_HALF)
    cos, sin = jnp.cos(ang), jnp.sin(ang)
    ck = jnp.where(lane < 2 * ROT_HALF, cos, 1.0)
    s1 = jnp.where(lane < ROT_HALF, -sin, 0.0)
    s2 = jnp.where((lane >= ROT_HALF) & (lane < 2 * ROT_HALF), sin, 0.0)
    return cq, sq, ck, s1, s2


def _layer(x, ln_mix_g, w_in, b_gate_a, b_gate_b, w_o_a, w_o_b, w_out, ln_ffn_g, w_router, b_router,
           w_gate, b_gate, w_up, b_up, w_down, b_down, ln_out_g):
    B, S, D = x.shape
    T = B * S
    qa0, ka0, va0 = 0, WIDTH_A, 2 * WIDTH_A
    qb0, kb0, vb0 = 3 * WIDTH_A, 3 * WIDTH_A + WIDTH_B, 3 * WIDTH_A + 2 * WIDTH_B
    g0 = 3 * WIDTH_A + 3 * WIDTH_B
    cols = lambda s, w: w_in[:, s:s + w]
    gmix = ln_mix_g.reshape(1, D)

    obs, lses = [], []
    o_a = None
    for gi, (_, d) in enumerate(DIL_PAIRS):
        off = gi * WIDTH_G
        wq, wk, wv = cols(qb0 + off, WIDTH_G), cols(kb0 + off, WIDTH_G), cols(vb0 + off, WIDTH_G)
        na = 0
        if gi == 0:
            na = WIDTH_A
            wq = jnp.concatenate([cols(qa0, WIDTH_A), wq], axis=1)
            wk = jnp.concatenate([cols(ka0, WIDTH_A), wk], axis=1)
            wv = jnp.concatenate([cols(va0, WIDTH_A), wv], axis=1)
        outs = _project(x, gmix, wq.T.astype(BF16), wk.astype(BF16), wv.T.astype(BF16),
                        _rope_tables(S, d), d, na)
        if gi == 0:
            qTa, vTa, ka, kmean = outs[:4]
            outs = outs[4:]
            o_a = _moba(qTa, ka, vTa, kmean.reshape(B, S // MOBA_BLOCK, WIDTH_A))
        qTb, vTb, kb = outs
        o_g, lse_g = _dilated(qTb, kb, vTb)
        obs.append(o_g.reshape(T, WIDTH_G))
        lses.append(lse_g.reshape(T, WIDTH_G))

    x1, h2, idxT, _, rankT, pnat, cnt = _post(
        x.reshape(T, D), o_a.reshape(T, WIDTH_A), obs, lses, gmix,
        cols(g0, 2 * D).astype(BF16), b_gate_a.reshape(1, D), b_gate_b.reshape(1, D),
        w_o_a.astype(BF16), w_o_b.astype(BF16), w_out.astype(BF16), ln_ffn_g.reshape(1, D),
        w_router.T, b_router.reshape(N_EXPERTS, 1))

    br = EXPERT_ROWS
    counts = cnt[:, 0].astype(jnp.int32)
    padded = (counts + br - 1) // br * br
    pend = jnp.cumsum(padded)
    pstart = pend - padded
    eids = jnp.arange(N_EXPERTS, dtype=jnp.int32)[:, None, None]
    dest = jnp.sum(jnp.where(idxT[None, :TOP_K] == eids, pstart[:, None, None], 0), axis=0) + rankT[:TOP_K]
    n_rows = T * TOP_K + N_EXPERTS * br
    nblk = n_rows // br
    n_used = pend[-1] // br
    first_row = jnp.arange(nblk, dtype=jnp.int32) * br
    be = jnp.minimum(jnp.sum((pend[None, :] <= first_row[:, None]).astype(jnp.int32), axis=1), N_EXPERTS - 1)
    be = jnp.where(jnp.arange(nblk) < n_used, be, be[n_used - 1]).astype(jnp.int32)

    dest_d = dest.reshape(TOP_K, T // DISPATCH_TM, DISPATCH_TM).transpose(1, 0, 2)
    x_rows = _dispatch(pend.astype(jnp.int32), padded.astype(jnp.int32), dest_d, h2, n_rows)
    y_rows = _experts(be, n_used.reshape(1).astype(jnp.int32), x_rows, w_gate, b_gate, w_up, b_up, w_down, b_down)
    dest_c = dest.reshape(TOP_K, T // COMBINE_TM, COMBINE_TM).transpose(1, 0, 2)
    out = _combine(dest_c, y_rows, x1, pnat, ln_out_g.reshape(1, D))
    return out.reshape(B, S, D)


def kernel(x, ln_mix_g, w_in, b_gate_a, b_gate_b, w_o_a, w_o_b, w_out, ln_ffn_g, w_router, b_router,
           w_gate, b_gate, w_up, b_up, w_down, b_down, ln_final_g):
    depth = ln_mix_g.shape[0]
    assert depth == 1, "the final RMSNorm is fused into the last layer's combine"
    return _layer(x, ln_mix_g[0], w_in[0], b_gate_a[0], b_gate_b[0], w_o_a[0], w_o_b[0], w_out[0],
                  ln_ffn_g[0], w_router[0], b_router[0], w_gate[0], b_gate[0], w_up[0], b_up[0],
                  w_down[0], b_down[0], ln_final_g)
```

```python
import functools

import jax
import jax.numpy as jnp
from jax import lax
from jax.experimental import pallas as pl
from jax.experimental.pallas import tpu as pltpu

D_MODEL = 1024
HEAD_DIM = 64
ROT_HALF = HEAD_DIM // 8
ROPE_THETA = 500000.0
N_HEADS_A = 8
MOBA_BLOCK = 256
MOBA_TOPK = 3
MOBA_QBLOCKS = 2
DIL_PAIRS = ((128, 1), (512, 4), (2048, 16))
DIL_BAND = 128
HEADS_PER_GROUP_B = 4
WIDTH_A = N_HEADS_A * HEAD_DIM
WIDTH_G = HEADS_PER_GROUP_B * HEAD_DIM
WIDTH_B = WIDTH_G * len(DIL_PAIRS)
N_EXPERTS = 32
TOP_K = 4
SWIGLU_LIMIT = 7.0
SWIGLU_ALPHA = 1.702
NORM_EPS = 1e-5
SCALE = HEAD_DIM ** -0.5
LOG2E = 1.4426950408889634

LANES = 128
PROJ_TM = 512
DIL_TL = 512
POST_TM = 512
EXPERT_ROWS = 256
DISPATCH_TM = 512
COMBINE_TM = 256
SUBLANES = 8
ROW_UNROLL = 8
VMEM_LIMIT = 56 * 1024 * 1024
NEG = -1e30

BF16 = jnp.bfloat16
F32 = jnp.float32


def _dot(a, b):
    return jnp.dot(a, b, preferred_element_type=F32)


def _dot_nt(a, b):
    return lax.dot_general(a, b, (((1,), (1,)), ((), ())), preferred_element_type=F32)


def _rms(x, g):
    ms = jnp.mean(x * x, axis=-1, keepdims=True)
    return x * lax.rsqrt(ms + NORM_EPS) * g


def _rows_to_tiles(ref, val):
    n = val.shape[0]
    for c in range(val.shape[1] // LANES):
        ref[pl.ds(c, n, stride=SUBLANES), :] = val[:, c * LANES:(c + 1) * LANES]


def _tiles_to_rows(ref, n):
    return jnp.concatenate([ref[pl.ds(c, n, stride=SUBLANES), :] for c in range(SUBLANES)], axis=1)


def _cparams(sem):
    return pltpu.CompilerParams(dimension_semantics=sem, vmem_limit_bytes=VMEM_LIMIT)


def _proj_kernel(x_ref, g_ref, wq_ref, wk_ref, wv_ref, cq_ref, sq_ref, ck_ref, s1_ref, s2_ref, *outs, na):
    tm = x_ref.shape[0]
    h = _rms(x_ref[...], g_ref[...]).astype(BF16)
    nq = wq_ref.shape[0]
    heads = nq // HEAD_DIM

    qT = _dot_nt(wq_ref[...], h)
    q3 = qT.reshape(heads, HEAD_DIM, tm)
    c = cq_ref[...][None]
    s = sq_ref[...][None]
    x1 = q3[:, 0:ROT_HALF]
    x2 = q3[:, ROT_HALF:2 * ROT_HALF]
    q3 = jnp.concatenate([x1 * c - x2 * s, x2 * c + x1 * s, q3[:, 2 * ROT_HALF:]], axis=1)
    hid = lax.broadcasted_iota(jnp.int32, (heads, 1, 1), 0)
    q3 = q3 * jnp.where(hid < na // HEAD_DIM, SCALE * LOG2E, SCALE)
    qT = q3.reshape(nq, tm).astype(BF16)

    vT = _dot_nt(wv_ref[...], h).astype(BF16)

    kk = _dot(h, wk_ref[...])
    ck, s1, s2 = ck_ref[...], s1_ref[...], s2_ref[...]
    kparts = []
    for gi in range(kk.shape[1] // LANES):
        kg = kk[:, gi * LANES:(gi + 1) * LANES]
        kparts.append(kg * ck + pltpu.roll(kg, LANES - ROT_HALF, 1) * s1 + pltpu.roll(kg, ROT_HALF, 1) * s2)

    if na:
        qa_ref, va_ref, ka_ref, km_ref, qb_ref, vb_ref, kb_ref = outs
        for blk in range(tm // MOBA_BLOCK):
            sl = slice(blk * MOBA_BLOCK, (blk + 1) * MOBA_BLOCK)
            qa_ref[blk] = qT[0:na, sl]
            va_ref[blk] = vT[0:na, sl]
        lane = lax.broadcasted_iota(jnp.int32, (tm, LANES), 1)
        row = lax.broadcasted_iota(jnp.int32, (tm, LANES), 0)
        blkid = pl.program_id(2) * (tm // MOBA_BLOCK) + row // MOBA_BLOCK
        for gi in range(na // LANES):
            kg = kparts[gi]
            for e in range(2):
                in_head = (lane >= HEAD_DIM * e) & (lane < HEAD_DIM * (e + 1))
                onehot = (lane - HEAD_DIM * (1 - e)) == blkid
                col = (2 * gi + e) * LANES
                ka_ref[:, col:col + LANES] = jnp.where(in_head, kg, onehot.astype(F32)).astype(BF16)
            km = kg.reshape(tm // MOBA_BLOCK, MOBA_BLOCK, LANES).sum(axis=1) * (1.0 / MOBA_BLOCK)
            km_ref[:, gi * LANES:(gi + 1) * LANES] = km
    else:
        qb_ref, vb_ref, kb_ref = outs
    for cb in range(tm // DIL_BAND):
        sl = slice(cb * DIL_BAND, (cb + 1) * DIL_BAND)
        qb_ref[cb] = qT[na:na + WIDTH_G, sl]
        vb_ref[cb] = vT[na:na + WIDTH_G, sl]
    for gi in range(WIDTH_G // LANES):
        kb_ref[:, gi * LANES:(gi + 1) * LANES] = kparts[na // LANES + gi].astype(BF16)


def _project(x, g, wqT, wk, wvT, tabs, d, na):
    B, S, D = x.shape
    L = S // d
    tm = min(PROJ_TM, L)
    cq, sq, ck, s1, s2 = tabs
    xv = x.reshape(B, L, d * D)
    grid = (B, d, L // tm)
    full = lambda a: pl.BlockSpec(a.shape, lambda b, r, i: (0,) * a.ndim)
    in_specs = [
        pl.BlockSpec((None, tm, D), lambda b, r, i: (b, i, r)),
        full(g), full(wqT), full(wk), full(wvT),
        pl.BlockSpec((None, ROT_HALF, tm), lambda b, r, i: (r, 0, i)),
        pl.BlockSpec((None, ROT_HALF, tm), lambda b, r, i: (r, 0, i)),
        pl.BlockSpec((None, tm, LANES), lambda b, r, i: (r, i, 0)),
        pl.BlockSpec((None, tm, LANES), lambda b, r, i: (r, i, 0)),
        pl.BlockSpec((None, tm, LANES), lambda b, r, i: (r, i, 0)),
    ]
    nsub = tm // DIL_BAND
    out_shape = [
        jax.ShapeDtypeStruct((B, d, L // DIL_BAND, WIDTH_G, DIL_BAND), BF16),
        jax.ShapeDtypeStruct((B, d, L // DIL_BAND, WIDTH_G, DIL_BAND), BF16),
        jax.ShapeDtypeStruct((B, d, L, WIDTH_G), BF16),
    ]
    out_specs = [
        pl.BlockSpec((None, None, nsub, WIDTH_G, DIL_BAND), lambda b, r, i: (b, r, i, 0, 0)),
        pl.BlockSpec((None, None, nsub, WIDTH_G, DIL_BAND), lambda b, r, i: (b, r, i, 0, 0)),
        pl.BlockSpec((None, None, tm, WIDTH_G), lambda b, r, i: (b, r, i, 0)),
    ]
    if na:
        nblk = tm // MOBA_BLOCK
        out_shape = [
            jax.ShapeDtypeStruct((B, S // MOBA_BLOCK, na, MOBA_BLOCK), BF16),
            jax.ShapeDtypeStruct((B, S // MOBA_BLOCK, na, MOBA_BLOCK), BF16),
            jax.ShapeDtypeStruct((B, S, 2 * na), BF16),
            jax.ShapeDtypeStruct((B, S // tm, nblk, na), F32),
        ] + out_shape
        out_specs = [
            pl.BlockSpec((None, nblk, na, MOBA_BLOCK), lambda b, r, i: (b, i, 0, 0)),
            pl.BlockSpec((None, nblk, na, MOBA_BLOCK), lambda b, r, i: (b, i, 0, 0)),
            pl.BlockSpec((None, tm, 2 * na), lambda b, r, i: (b, i, 0)),
            pl.BlockSpec((None, None, nblk, na), lambda b, r, i: (b, i, 0, 0)),
        ] + out_specs
    return pl.pallas_call(
        functools.partial(_proj_kernel, na=na),
        grid=grid, in_specs=in_specs, out_specs=out_specs, out_shape=out_shape,
        compiler_params=_cparams(("arbitrary",) * 3),
        name=f"proj_d{d}",
    )(xv, g, wqT, wk, wvT, cq, sq, ck, s1, s2)


def _moba_kernel(q_ref, k_ref, v_ref, km_ref, o_ref, qa_scr, s_scr, p_scr):
    g = pl.program_id(2)
    nb = km_ref.shape[0]
    blk = MOBA_BLOCK
    tq = MOBA_QBLOCKS * blk
    q2 = jnp.concatenate([q_ref[i] for i in range(MOBA_QBLOCKS)], axis=1)
    bid = lax.broadcasted_iota(jnp.int32, (nb, tq), 0)
    qblk = MOBA_QBLOCKS * g + lax.broadcasted_iota(jnp.int32, (nb, tq), 1) // blk
    key_i = lax.broadcasted_iota(jnp.int32, (blk, tq), 0)
    qry_i = lax.broadcasted_iota(jnp.int32, (blk, tq), 1)
    km = km_ref[...].astype(BF16)
    zq = jnp.zeros((HEAD_DIM, tq), BF16)
    zb = jnp.zeros((HEAD_DIM - nb, tq), BF16)

    state = []
    for h in range(2):
        hl = slice(LANES * h, LANES * (h + 1))
        vl = slice(HEAD_DIM * h, HEAD_DIM * (h + 1))
        qh = q2[HEAD_DIM * h:HEAD_DIM * (h + 1)]
        q_plain = jnp.concatenate([qh, zq] if h == 0 else [zq, qh], axis=0)
        gt = jnp.where(bid < qblk, _dot(km, q_plain), -jnp.inf)
        sel = jnp.zeros(gt.shape, jnp.bool_)
        for _ in range(MOBA_TOPK):
            mx = jnp.max(gt, axis=0, keepdims=True)
            first = jnp.min(jnp.where((gt == mx) & (mx > -jnp.inf), bid, nb), axis=0, keepdims=True)
            pick = bid == first
            sel = sel | pick
            gt = jnp.where(pick, -jnp.inf, gt)
        bias = jnp.where(sel, 0.0, NEG).astype(BF16)
        qa_scr[h] = jnp.concatenate([qh, bias, zb] if h == 0 else [bias, zb, qh], axis=0)
        bias_d = jnp.where(sel | (bid == qblk), 0.0, NEG).astype(BF16)
        q_diag = jnp.concatenate([qh, bias_d, zb] if h == 0 else [bias_d, zb, qh], axis=0)
        ss = []
        for i in range(MOBA_QBLOCKS):
            j = MOBA_QBLOCKS * g + i
            sd = _dot(k_ref[pl.ds(pl.multiple_of(j * blk, blk), blk), hl], q_diag)
            own = (qry_i >= i * blk) & (qry_i < (i + 1) * blk)
            ss.append(jnp.where(own & (key_i > qry_i - i * blk), NEG, sd))
        m = ss[0].max(axis=0, keepdims=True)
        for sd in ss[1:]:
            m = jnp.maximum(m, sd.max(axis=0, keepdims=True))
        l = jnp.zeros((1, tq), F32)
        acc = jnp.zeros((HEAD_DIM, tq), F32)
        for i, sd in enumerate(ss):
            p = jnp.exp2(sd - m)
            l = l + jnp.sum(p, axis=0, keepdims=True)
            acc = acc + _dot(v_ref[MOBA_QBLOCKS * g + i, vl, :], p.astype(BF16))
        state += [m, l, acc]

    def blocks(c):
        return [jnp.where((c >= 0) & (c < g), MOBA_QBLOCKS * c + i, nb - 1) for i in range(MOBA_QBLOCKS)]

    def body(it, carry):
        alphas, ms, ls, accs = carry
        jn = blocks(it)
        jv = blocks(it - 2)
        a_new, m_new, l_new, acc_new = [], [], [], []
        for h in range(2):
            hl = slice(LANES * h, LANES * (h + 1))
            vl = slice(HEAD_DIM * h, HEAD_DIM * (h + 1))
            qa = qa_scr[h]
            s_next = [_dot(k_ref[pl.ds(pl.multiple_of(j * blk, blk), blk), hl], qa) for j in jn]
            s_cur = [s_scr[h, i] for i in range(MOBA_QBLOCKS)]
            p_old = [p_scr[h, i] for i in range(MOBA_QBLOCKS)]
            mn = ms[h]
            for sc in s_cur:
                mn = jnp.maximum(mn, jnp.max(sc, axis=0, keepdims=True))
            alpha = jnp.exp2(ms[h] - mn)
            l = alpha * ls[h]
            acc = alphas[h] * accs[h]
            for i in range(MOBA_QBLOCKS):
                p = jnp.exp2(s_cur[i] - mn)
                l = l + jnp.sum(p, axis=0, keepdims=True)
                acc = acc + _dot(v_ref[jv[i], vl, :], p_old[i])
                s_scr[h, i] = s_next[i]
                p_scr[h, i] = p.astype(BF16)
            m_new.append(mn)
            l_new.append(l)
            a_new.append(alpha)
            acc_new.append(acc)
        return (tuple(a_new), tuple(m_new), tuple(l_new), tuple(acc_new))

    s_scr[...] = jnp.full(s_scr.shape, NEG, F32)
    p_scr[...] = jnp.zeros(p_scr.shape, BF16)
    one = jnp.ones((1, tq), F32)
    init = ((one, one), (state[0], state[3]), (state[1], state[4]), (state[2], state[5]))
    _, _, ls, accs = lax.fori_loop(0, g + 2, body, init)
    oT = jnp.concatenate([accs[0] / ls[0], accs[1] / ls[1]], axis=0)
    o_ref[...] = oT.T.astype(o_ref.dtype)


def _moba(qT, k, vT, kmean):
    B, nb, wa, blk = qT.shape
    S = nb * blk
    hp = wa // LANES
    nq = MOBA_QBLOCKS
    assert nb % 16 == 0 and nb <= HEAD_DIM, "block-mask rows must fit the spare half of a head pair"
    return pl.pallas_call(
        _moba_kernel,
        grid=(B, hp, nb // nq),
        in_specs=[
            pl.BlockSpec((None, nq, LANES, blk), lambda b, p, n: (b, n, p, 0)),
            pl.BlockSpec((None, S, 2 * LANES), lambda b, p, n: (b, 0, p)),
            pl.BlockSpec((None, nb, LANES, blk), lambda b, p, n: (b, 0, p, 0)),
            pl.BlockSpec((None, nb, LANES), lambda b, p, n: (b, 0, p)),
        ],
        out_specs=pl.BlockSpec((None, nq * blk, LANES), lambda b, p, n: (b, n, p)),
        out_shape=jax.ShapeDtypeStruct((B, S, wa), BF16),
        scratch_shapes=[pltpu.VMEM((2, LANES, nq * blk), BF16), pltpu.VMEM((2, nq, blk, nq * blk), F32),
                        pltpu.VMEM((2, nq, blk, nq * blk), BF16)],
        compiler_params=_cparams(("arbitrary",) * 3),
        name="moba_attn",
    )(qT, k, vT, kmean)


def _dil_kernel(q_ref, k_ref, kp_ref, v_ref, vp_ref, o_ref, lse_ref):
    t = pl.program_id(2)
    band = DIL_BAND
    nsub = q_ref.shape[0]
    key_i = lax.broadcasted_iota(jnp.int32, (band, band), 0)
    qry_i = lax.broadcasted_iota(jnp.int32, (band, band), 1)
    rowid = lax.broadcasted_iota(jnp.int32, (LANES, band), 0)
    own_ok = key_i <= qry_i
    prev_ok = key_i >= qry_i
    for c in range(nsub):
        o_parts, l_parts = [], []
        for hp in range(WIDTH_G // LANES):
            cols = slice(hp * LANES, (hp + 1) * LANES)
            k_own = k_ref[c * band:(c + 1) * band, cols]
            k_prev = kp_ref[:, cols] if c == 0 else k_ref[(c - 1) * band:c * band, cols]
            q2 = q_ref[c, cols, :]
            for h in range(2):
                qh = jnp.where((rowid >= HEAD_DIM * h) & (rowid < HEAD_DIM * (h + 1)), q2, jnp.zeros_like(q2))
                pmask = (prev_ok & (t > 0)) if c == 0 else prev_ok
                s_own = jnp.where(own_ok, _dot(k_own, qh), -jnp.inf)
                s_prev = jnp.where(pmask, _dot(k_prev, qh), -jnp.inf)
                m = jnp.maximum(jnp.max(s_own, axis=0, keepdims=True), jnp.max(s_prev, axis=0, keepdims=True))
                p_own = jnp.exp(s_own - m)
                p_prev = jnp.exp(s_prev - m)
                l = jnp.sum(p_own, axis=0, keepdims=True) + jnp.sum(p_prev, axis=0, keepdims=True)
                rows = slice(hp * LANES + h * HEAD_DIM, hp * LANES + (h + 1) * HEAD_DIM)
                v_own = v_ref[c, rows, :]
                v_prev = vp_ref[rows, :] if c == 0 else v_ref[c - 1, rows, :]
                oT = _dot(v_own, p_own.astype(BF16)) + _dot(v_prev, p_prev.astype(BF16))
                o_parts.append(oT / l)
                l_parts.append(jnp.broadcast_to(m + jnp.log(l), (HEAD_DIM, band)))
        o_ref[c * band:(c + 1) * band, :] = jnp.concatenate(o_parts, axis=0).T
        lse_ref[c * band:(c + 1) * band, :] = jnp.concatenate(l_parts, axis=0).T


def _dilated(qT, k, vT):
    B, d, nblk, wg, band = qT.shape
    L = nblk * band
    tl = min(DIL_TL, L)
    nsub = tl // band
    prev = lambda t: jnp.maximum(t * nsub - 1, 0)
    return pl.pallas_call(
        _dil_kernel,
        grid=(B, d, L // tl),
        in_specs=[
            pl.BlockSpec((None, None, nsub, wg, band), lambda b, r, t: (b, r, t, 0, 0)),
            pl.BlockSpec((None, None, tl, wg), lambda b, r, t: (b, r, t, 0)),
            pl.BlockSpec((None, None, band, wg), lambda b, r, t: (b, r, prev(t), 0)),
            pl.BlockSpec((None, None, nsub, wg, band), lambda b, r, t: (b, r, t, 0, 0)),
            pl.BlockSpec((None, None, None, wg, band), lambda b, r, t: (b, r, prev(t), 0, 0)),
        ],
        out_specs=[
            pl.BlockSpec((None, tl, wg), lambda b, r, t: (b, t, r)),
            pl.BlockSpec((None, tl, wg), lambda b, r, t: (b, t, r)),
        ],
        out_shape=[jax.ShapeDtypeStruct((B, L, d * wg), F32)] * 2,
        compiler_params=_cparams(("arbitrary",) * 3),
        name=f"dilated_d{d}",
    )(qT, k, k, vT, vT)


def _post_kernel(x_ref, oa_ref, o1_ref, o2_ref, o3_ref, l1_ref, l2_ref, l3_ref, gmix_ref, wg_ref, bga_ref,
                 bgb_ref, woa_ref, wob_ref, wout_ref, gffn_ref, wr_ref, br_ref,
                 x1_ref, h2_ref, idx_ref, p_ref, rank_ref, pnat_ref, cnt_ref, carry_scr):
    i = pl.program_id(0)
    tm, D = x_ref.shape
    ne = wr_ref.shape[0]

    @pl.when(i == 0)
    def _():
        carry_scr[...] = jnp.zeros_like(carry_scr)

    x = x_ref[...]
    h = _rms(x, gmix_ref[...]).astype(BF16)
    gates = _dot(h, wg_ref[...])
    ga = gates[:, :D] + bga_ref[...]
    gb = gates[:, D:] + bgb_ref[...]
    l1, l2, l3 = l1_ref[...], l2_ref[...], l3_ref[...]
    mx = jnp.maximum(jnp.maximum(l1, l2), l3)
    e1, e2, e3 = jnp.exp(l1 - mx), jnp.exp(l2 - mx), jnp.exp(l3 - mx)
    ob = (e1 * o1_ref[...] + e2 * o2_ref[...] + e3 * o3_ref[...]) / (e1 + e2 + e3)
    ya = _dot(oa_ref[...], woa_ref[...])
    yb = _dot(ob.astype(BF16), wob_ref[...])
    mix = jax.nn.sigmoid(ga) * ya + jax.nn.sigmoid(gb) * yb
    x1 = x + _dot(mix.astype(BF16), wout_ref[...])
    x1_ref[...] = x1
    h2 = _rms(x1, gffn_ref[...])
    _rows_to_tiles(h2_ref, h2)

    logits = lax.dot_general(wr_ref[...], h2, (((1,), (1,)), ((), ())), preferred_element_type=F32,
                             precision=lax.Precision.HIGHEST) + br_ref[...]
    eid = lax.broadcasted_iota(jnp.int32, (ne, tm), 0)
    g = logits
    vals, idxs, picks = [], [], []
    for _ in range(TOP_K):
        m = jnp.max(g, axis=0, keepdims=True)
        first = jnp.min(jnp.where(g == m, eid, ne), axis=0, keepdims=True)
        pick = eid == first
        vals.append(m)
        idxs.append(first)
        picks.append(pick)
        g = jnp.where(pick, -jnp.inf, g)
    es = [jnp.exp(v - vals[0]) for v in vals]
    den = es[0] + es[1] + es[2] + es[3]
    probs = [e / den for e in es]

    onehot = jnp.zeros((ne, tm), F32)
    for pick in picks:
        onehot = onehot + pick.astype(F32)
    earlier = (lax.broadcasted_iota(jnp.int32, (tm, tm), 0) < lax.broadcasted_iota(jnp.int32, (tm, tm), 1))
    prefix = _dot(onehot.astype(BF16), earlier.astype(BF16)) + carry_scr[:, 0:1]
    ranks = [jnp.sum(jnp.where(pick, prefix, 0.0), axis=0, keepdims=True) for pick in picks]
    carry_scr[...] = carry_scr[...] + jnp.sum(onehot, axis=1, keepdims=True)
    cnt_ref[...] = carry_scr[...]

    zi = jnp.zeros((8 - TOP_K, tm), jnp.int32)
    idx_ref[...] = jnp.concatenate(idxs + [zi], axis=0)
    rank_ref[...] = jnp.concatenate([r.astype(jnp.int32) for r in ranks] + [zi], axis=0)
    p_ref[...] = jnp.concatenate(probs + [jnp.zeros((8 - TOP_K, tm), F32)], axis=0)
    pnat_ref[...] = jnp.concatenate(probs + [jnp.zeros((LANES - TOP_K, tm), F32)], axis=0).T


def _post(x2, oa, obs, lses, gmix, wg, bga, bgb, woa, wob, wout, gffn, wrT, br):
    T, D = x2.shape
    tm = POST_TM
    ne = wrT.shape[0]
    row = lambda w: pl.BlockSpec((tm, w), lambda i: (i, 0))
    full = lambda a: pl.BlockSpec(a.shape, lambda i: (0,) * a.ndim)
    col = pl.BlockSpec((8, tm), lambda i: (0, i))
    return pl.pallas_call(
        _post_kernel,
        grid=(T // tm,),
        in_specs=[row(D), row(WIDTH_A)] + [row(WIDTH_G)] * 6
                 + [full(a) for a in (gmix, wg, bga, bgb, woa, wob, wout, gffn, wrT, br)],
        out_specs=[row(D), pl.BlockSpec((tm * SUBLANES, LANES), lambda i: (i, 0)), col, col, col, row(LANES),
                   pl.BlockSpec((ne, LANES), lambda i: (0, 0))],
        out_shape=[
            jax.ShapeDtypeStruct((T, D), F32), jax.ShapeDtypeStruct((T * SUBLANES, LANES), F32),
            jax.ShapeDtypeStruct((8, T), jnp.int32), jax.ShapeDtypeStruct((8, T), F32),
            jax.ShapeDtypeStruct((8, T), jnp.int32), jax.ShapeDtypeStruct((T, LANES), F32),
            jax.ShapeDtypeStruct((ne, LANES), F32),
        ],
        scratch_shapes=[pltpu.VMEM((ne, LANES), F32)],
        compiler_params=_cparams(("arbitrary",)),
        name="post_mix_router",
    )(x2, oa, *obs, *lses, gmix, wg, bga, bgb, woa, wob, wout, gffn, wrT, br)


def _dispatch_kernel(pend_ref, padded_ref, dest_ref, h_ref, xr_ref, zero_scr, sem):
    i = pl.program_id(0)
    tm = h_ref.shape[0] // SUBLANES
    br = zero_scr.shape[0]

    def zero_copy(blk_start):
        return pltpu.make_async_copy(zero_scr, xr_ref.at[pl.ds(pl.multiple_of(blk_start * SUBLANES, br), br), :], sem)

    @pl.when(i == 0)
    def _():
        zero_scr[...] = jnp.zeros_like(zero_scr)
        n_used = pend_ref[N_EXPERTS - 1] // EXPERT_ROWS
        n_blk = xr_ref.shape[0] // br

        def tail(fn):
            def go(e, c):
                @pl.when(padded_ref[e] > 0)
                def _():
                    fn(zero_copy(pend_ref[e] - EXPERT_ROWS))
                return c
            return go

        def unused(fn):
            def go(b, c):
                fn(zero_copy(b * EXPERT_ROWS))
                return c
            return go

        lax.fori_loop(0, N_EXPERTS, tail(lambda cp: cp.start()), 0)
        lax.fori_loop(n_used, n_blk, unused(lambda cp: cp.start()), 0)
        lax.fori_loop(0, N_EXPERTS, tail(lambda cp: cp.wait()), 0)
        lax.fori_loop(n_used, n_blk, unused(lambda cp: cp.wait()), 0)

    def row_copy(t, k):
        src = h_ref.at[pl.ds(pl.multiple_of(t * SUBLANES, SUBLANES), SUBLANES), :]
        dst = xr_ref.at[pl.ds(pl.multiple_of(dest_ref[k, t] * SUBLANES, SUBLANES), SUBLANES), :]
        return pltpu.make_async_copy(src, dst, sem)

    def start(tb, c):
        for u in range(ROW_UNROLL):
            for k in range(TOP_K):
                row_copy(tb * ROW_UNROLL + u, k).start(priority=(u * TOP_K + k) % 2)
        return c

    def wait(tb, c):
        for u in range(ROW_UNROLL):
            for k in range(TOP_K):
                row_copy(tb * ROW_UNROLL + u, k).wait()
        return c

    lax.fori_loop(0, tm // ROW_UNROLL, start, 0)
    lax.fori_loop(0, tm // ROW_UNROLL, wait, 0)


def _dispatch(pend, padded, dest3, h2t, n_rows):
    tm = DISPATCH_TM
    return pl.pallas_call(
        _dispatch_kernel,
        grid_spec=pltpu.PrefetchScalarGridSpec(
            num_scalar_prefetch=2,
            grid=(h2t.shape[0] // (tm * SUBLANES),),
            in_specs=[
                pl.BlockSpec((None, TOP_K, tm), lambda i, pe, pa: (i, 0, 0), memory_space=pltpu.SMEM),
                pl.BlockSpec((tm * SUBLANES, LANES), lambda i, pe, pa: (i, 0)),
            ],
            out_specs=pl.BlockSpec(memory_space=pl.ANY),
            scratch_shapes=[pltpu.VMEM((EXPERT_ROWS * SUBLANES, LANES), F32), pltpu.SemaphoreType.DMA(())],
        ),
        out_shape=jax.ShapeDtypeStruct((n_rows * SUBLANES, LANES), F32),
        compiler_params=_cparams(("arbitrary",)),
        name="moe_dispatch",
    )(pend, padded, dest3, h2t)


def _expert_kernel(be_ref, nu_ref, x_ref, wg_ref, bg_ref, wu_ref, bu_ref, wd_ref, bd_ref, y_ref,
                   wgb_scr, wub_scr, wdb_scr):
    b = pl.program_id(0)
    used = b < nu_ref[0]
    new_expert = (b == 0) | (be_ref[b] != be_ref[jnp.maximum(b - 1, 0)])

    @pl.when(used & new_expert)
    def _():
        wgb_scr[...] = wg_ref[...].astype(BF16)
        wub_scr[...] = wu_ref[...].astype(BF16)
        wdb_scr[...] = wd_ref[...].astype(BF16)

    @pl.when(used)
    def _():
        x = _tiles_to_rows(x_ref, EXPERT_ROWS).astype(BF16)
        g = _dot(x, wgb_scr[...]) + bg_ref[...]
        u = _dot(x, wub_scr[...]) + bu_ref[...]
        g = jnp.minimum(g, SWIGLU_LIMIT)
        u = jnp.clip(u, -SWIGLU_LIMIT, SWIGLU_LIMIT)
        a = g * jax.nn.sigmoid(SWIGLU_ALPHA * g) * (u + 1.0)
        _rows_to_tiles(y_ref, _dot(a.astype(BF16), wdb_scr[...]) + bd_ref[...])

    @pl.when(jnp.logical_not(used))
    def _():
        y_ref[...] = jnp.zeros_like(y_ref)


def _experts(blk_expert, n_used, x_rows, w_gate, b_gate, w_up, b_up, w_down, b_down):
    E, D, F = w_gate.shape
    n_rows = x_rows.shape[0] // SUBLANES
    br = EXPERT_ROWS
    tb = br * SUBLANES
    wspec = lambda shape: pl.BlockSpec((None,) + shape, lambda b, be, nu: (be[b], 0, 0))
    return pl.pallas_call(
        _expert_kernel,
        grid_spec=pltpu.PrefetchScalarGridSpec(
            num_scalar_prefetch=2,
            grid=(n_rows // br,),
            in_specs=[
                pl.BlockSpec((tb, LANES), lambda b, be, nu: (jnp.minimum(b, nu[0] - 1), 0)),
                wspec((D, F)), wspec((1, F)), wspec((D, F)), wspec((1, F)), wspec((F, D)), wspec((1, D)),
            ],
            out_specs=pl.BlockSpec((tb, LANES), lambda b, be, nu: (b, 0)),
            scratch_shapes=[pltpu.VMEM((D, F), BF16), pltpu.VMEM((D, F), BF16), pltpu.VMEM((F, D), BF16)],
        ),
        out_shape=jax.ShapeDtypeStruct((n_rows * SUBLANES, LANES), F32),
        compiler_params=_cparams(("arbitrary",)),
        name="moe_experts",
    )(blk_expert, n_used, x_rows, w_gate, b_gate.reshape(E, 1, F), w_up, b_up.reshape(E, 1, F),
      w_down, b_down.reshape(E, 1, D))


def _combine_kernel(dest_ref, y_ref, x1_ref, p_ref, g_ref, o_ref, ybuf, sem):
    tm = x1_ref.shape[0]

    def row_copy(t, k):
        src = y_ref.at[pl.ds(pl.multiple_of(dest_ref[k, t] * SUBLANES, SUBLANES), SUBLANES), :]
        dst = ybuf.at[k, pl.ds(pl.multiple_of(t * SUBLANES, SUBLANES), SUBLANES), :]
        return pltpu.make_async_copy(src, dst, sem)

    def start(tb, c):
        for u in range(ROW_UNROLL):
            for k in range(TOP_K):
                row_copy(tb * ROW_UNROLL + u, k).start(priority=(u * TOP_K + k) % 2)
        return c

    def wait(tb, c):
        for u in range(ROW_UNROLL):
            for k in range(TOP_K):
                row_copy(tb * ROW_UNROLL + u, k).wait()
        return c

    lax.fori_loop(0, tm // ROW_UNROLL, start, 0)
    lax.fori_loop(0, tm // ROW_UNROLL, wait, 0)
    p = p_ref[...]
    y = x1_ref[...]
    for k in range(TOP_K):
        y = y + p[:, k:k + 1] * _tiles_to_rows(ybuf.at[k], tm)
    o_ref[...] = _rms(y, g_ref[...])


def _combine(dest3, y_rows, x1, pnat, g_final):
    T, D = x1.shape
    tm = COMBINE_TM
    return pl.pallas_call(
        _combine_kernel,
        grid=(T // tm,),
        in_specs=[
            pl.BlockSpec((None, TOP_K, tm), lambda i: (i, 0, 0), memory_space=pltpu.SMEM),
            pl.BlockSpec(memory_space=pl.ANY),
            pl.BlockSpec((tm, D), lambda i: (i, 0)),
            pl.BlockSpec((tm, LANES), lambda i: (i, 0)),
            pl.BlockSpec((1, D), lambda i: (0, 0)),
        ],
        out_specs=pl.BlockSpec((tm, D), lambda i: (i, 0)),
        out_shape=jax.ShapeDtypeStruct((T, D), F32),
        scratch_shapes=[pltpu.VMEM((TOP_K, tm * SUBLANES, LANES), F32), pltpu.SemaphoreType.DMA(())],
        compiler_params=_cparams(("arbitrary",)),
        name="moe_combine",
    )(dest3, y_rows, x1, pnat, g_final)


def _rope_tables(S, d):
    L = S // d

    def angle(shape, r_ax, i_ax, f):
        pos = lax.broadcasted_iota(jnp.int32, shape, i_ax) * d + lax.broadcasted_iota(jnp.int32, shape, r_ax)
        inv = jnp.float32(ROPE_THETA) ** (-f.astype(F32) / ROT_HALF)
        return pos.astype(F32) * inv

    sh = (d, ROT_HALF, L)
    ang = angle(sh, 0, 2, lax.broadcasted_iota(jnp.int32, sh, 1))
    cq, sq = jnp.cos(ang), jnp.sin(ang)
    sh = (d, L, LANES)
    lane = lax.broadcasted_iota(jnp.int32, sh, 2) % HEAD_DIM
    ang = angle(sh, 0, 1, lane % ROT_HALF)
    cos, sin = jnp.cos(ang), jnp.sin(ang)
    ck = jnp.where(lane < 2 * ROT_HALF, cos, 1.0)
    s1 = jnp.where(lane < ROT_HALF, -sin, 0.0)
    s2 = jnp.where((lane >= ROT_HALF) & (lane < 2 * ROT_HALF), sin, 0.0)
    return cq, sq, ck, s1, s2


def _layer(x, ln_mix_g, w_in, b_gate_a, b_gate_b, w_o_a, w_o_b, w_out, ln_ffn_g, w_router, b_router,
           w_gate, b_gate, w_up, b_up, w_down, b_down, ln_out_g):
    B, S, D = x.shape
    T = B * S
    qa0, ka0, va0 = 0, WIDTH_A, 2 * WIDTH_A
    qb0, kb0, vb0 = 3 * WIDTH_A, 3 * WIDTH_A + WIDTH_B, 3 * WIDTH_A + 2 * WIDTH_B
    g0 = 3 * WIDTH_A + 3 * WIDTH_B
    cols = lambda s, w: w_in[:, s:s + w]
    gmix = ln_mix_g.reshape(1, D)

    obs, lses = [], []
    o_a = None
    for gi, (_, d) in enumerate(DIL_PAIRS):
        off = gi * WIDTH_G
        wq, wk, wv = cols(qb0 + off, WIDTH_G), cols(kb0 + off, WIDTH_G), cols(vb0 + off, WIDTH_G)
        na = 0
        if gi == 0:
            na = WIDTH_A
            wq = jnp.concatenate([cols(qa0, WIDTH_A), wq], axis=1)
            wk = jnp.concatenate([cols(ka0, WIDTH_A), wk], axis=1)
            wv = jnp.concatenate([cols(va0, WIDTH_A), wv], axis=1)
        outs = _project(x, gmix, wq.T.astype(BF16), wk.astype(BF16), wv.T.astype(BF16),
                        _rope_tables(S, d), d, na)
        if gi == 0:
            qTa, vTa, ka, kmean = outs[:4]
            outs = outs[4:]
            o_a = _moba(qTa, ka, vTa, kmean.reshape(B, S // MOBA_BLOCK, WIDTH_A))
        qTb, vTb, kb = outs
        o_g, lse_g = _dilated(qTb, kb, vTb)
        obs.append(o_g.reshape(T, WIDTH_G))
        lses.append(lse_g.reshape(T, WIDTH_G))

    x1, h2, idxT, _, rankT, pnat, cnt = _post(
        x.reshape(T, D), o_a.reshape(T, WIDTH_A), obs, lses, gmix,
        cols(g0, 2 * D).astype(BF16), b_gate_a.reshape(1, D), b_gate_b.reshape(1, D),
        w_o_a.astype(BF16), w_o_b.astype(BF16), w_out.astype(BF16), ln_ffn_g.reshape(1, D),
        w_router.T, b_router.reshape(N_EXPERTS, 1))

    br = EXPERT_ROWS
    counts = cnt[:, 0].astype(jnp.int32)
    padded = (counts + br - 1) // br * br
    pend = jnp.cumsum(padded)
    pstart = pend - padded
    eids = jnp.arange(N_EXPERTS, dtype=jnp.int32)[:, None, None]
    dest = jnp.sum(jnp.where(idxT[None, :TOP_K] == eids, pstart[:, None, None], 0), axis=0) + rankT[:TOP_K]
    n_rows = T * TOP_K + N_EXPERTS * br
    nblk = n_rows // br
    n_used = pend[-1] // br
    first_row = jnp.arange(nblk, dtype=jnp.int32) * br
    be = jnp.minimum(jnp.sum((pend[None, :] <= first_row[:, None]).astype(jnp.int32), axis=1), N_EXPERTS - 1)
    be = jnp.where(jnp.arange(nblk) < n_used, be, be[n_used - 1]).astype(jnp.int32)

    dest_d = dest.reshape(TOP_K, T // DISPATCH_TM, DISPATCH_TM).transpose(1, 0, 2)
    x_rows = _dispatch(pend.astype(jnp.int32), padded.astype(jnp.int32), dest_d, h2, n_rows)
    y_rows = _experts(be, n_used.reshape(1).astype(jnp.int32), x_rows, w_gate, b_gate, w_up, b_up, w_down, b_down)
    dest_c = dest.reshape(TOP_K, T // COMBINE_TM, COMBINE_TM).transpose(1, 0, 2)
    out = _combine(dest_c, y_rows, x1, pnat, ln_out_g.reshape(1, D))
    return out.reshape(B, S, D)


def kernel(x, ln_mix_g, w_in, b_gate_a, b_gate_b, w_o_a, w_o_b, w_out, ln_ffn_g, w_router, b_router,
           w_gate, b_gate, w_up, b_up, w_down, b_down, ln_final_g):
    depth = ln_mix_g.shape[0]
    assert depth == 1, "the final RMSNorm is fused into the last layer's combine"
    return _layer(x, ln_mix_g[0], w_in[0], b_gate_a[0], b_gate_b[0], w_o_a[0], w_o_b[0], w_out[0],
                  ln_ffn_g[0], w_router[0], b_router[0], w_gate[0], b_gate[0], w_up[0], b_up[0],
                  w_down[0], b_down[0], ln_final_g)
```

```python
import functools

import jax
import jax.numpy as jnp
from jax import lax
from jax.experimental import pallas as pl
from jax.experimental.pallas import tpu as pltpu

D_MODEL = 1024
HEAD_DIM = 64
ROT_HALF = HEAD_DIM // 8
ROPE_THETA = 500000.0
N_HEADS_A = 8
MOBA_BLOCK = 256
MOBA_TOPK = 3
MOBA_QBLOCKS = 2
DIL_PAIRS = ((128, 1), (512, 4), (2048, 16))
DIL_BAND = 128
HEADS_PER_GROUP_B = 4
WIDTH_A = N_HEADS_A * HEAD_DIM
WIDTH_G = HEADS_PER_GROUP_B * HEAD_DIM
WIDTH_B = WIDTH_G * len(DIL_PAIRS)
N_EXPERTS = 32
TOP_K = 4
SWIGLU_LIMIT = 7.0
SWIGLU_ALPHA = 1.702
NORM_EPS = 1e-5
SCALE = HEAD_DIM ** -0.5
LOG2E = 1.4426950408889634

LANES = 128
PROJ_TM = 512
PERM_TILE = 512
DIL_TL = 512
POST_TM = 512
EXPERT_ROWS = 256
DISPATCH_TM = 512
COMBINE_TM = 256
SUBLANES = 8
ROW_UNROLL = 8
VMEM_LIMIT = 56 * 1024 * 1024
NEG = -1e30

BF16 = jnp.bfloat16
F32 = jnp.float32


def _dot(a, b):
    return jnp.dot(a, b, preferred_element_type=F32)


def _dot_nt(a, b):
    return lax.dot_general(a, b, (((1,), (1,)), ((), ())), preferred_element_type=F32)


def _rms(x, g):
    ms = jnp.mean(x * x, axis=-1, keepdims=True)
    return x * lax.rsqrt(ms + NORM_EPS) * g


def _rows_to_tiles(ref, val):
    n = val.shape[0]
    for c in range(val.shape[1] // LANES):
        ref[pl.ds(c, n, stride=SUBLANES), :] = val[:, c * LANES:(c + 1) * LANES]


def _tiles_to_rows(ref, n):
    return jnp.concatenate([ref[pl.ds(c, n, stride=SUBLANES), :] for c in range(SUBLANES)], axis=1)


def _cparams(sem):
    return pltpu.CompilerParams(dimension_semantics=sem, vmem_limit_bytes=VMEM_LIMIT)


def _deinterleave(h, d):
    tm = h.shape[0]
    sub = min(tm, PERM_TILE)
    cs = sub // d
    ri = lax.broadcasted_iota(jnp.int32, (sub, sub), 0)
    ui = lax.broadcasted_iota(jnp.int32, (sub, sub), 1)
    perm = (ui == (ri % cs) * d + ri // cs).astype(BF16)
    slabs = [_dot(perm, h[s0:s0 + sub]).astype(BF16) for s0 in range(0, tm, sub)]
    return jnp.concatenate([sl[r * cs:(r + 1) * cs] for r in range(d) for sl in slabs], axis=0)


def _proj_kernel(x_ref, g_ref, wq_ref, wk_ref, wv_ref, cq_ref, sq_ref, ck_ref, s1_ref, s2_ref, *outs, na, d):
    tm = x_ref.shape[0]
    h = _rms(x_ref[...], g_ref[...]).astype(BF16)
    if d > 1:
        h = _deinterleave(h, d)
    nq = wq_ref.shape[0]
    heads = nq // HEAD_DIM

    qT = _dot_nt(wq_ref[...], h)
    q3 = qT.reshape(heads, HEAD_DIM, tm)
    c = cq_ref[...][None]
    s = sq_ref[...][None]
    x1 = q3[:, 0:ROT_HALF]
    x2 = q3[:, ROT_HALF:2 * ROT_HALF]
    q3 = jnp.concatenate([x1 * c - x2 * s, x2 * c + x1 * s, q3[:, 2 * ROT_HALF:]], axis=1)
    hid = lax.broadcasted_iota(jnp.int32, (heads, 1, 1), 0)
    q3 = q3 * jnp.where(hid < na // HEAD_DIM, SCALE * LOG2E, SCALE)
    qT = q3.reshape(nq, tm).astype(BF16)

    vT = _dot_nt(wv_ref[...], h).astype(BF16)

    kk = _dot(h, wk_ref[...])
    ck, s1, s2 = ck_ref[...], s1_ref[...], s2_ref[...]
    kparts = []
    for gi in range(kk.shape[1] // LANES):
        kg = kk[:, gi * LANES:(gi + 1) * LANES]
        kparts.append(kg * ck + pltpu.roll(kg, LANES - ROT_HALF, 1) * s1 + pltpu.roll(kg, ROT_HALF, 1) * s2)

    if na:
        qa_ref, va_ref, ka_ref, km_ref, qb_ref, vb_ref, kb_ref = outs
        for blk in range(tm // MOBA_BLOCK):
            sl = slice(blk * MOBA_BLOCK, (blk + 1) * MOBA_BLOCK)
            qa_ref[blk] = qT[0:na, sl]
            va_ref[blk] = vT[0:na, sl]
        lane = lax.broadcasted_iota(jnp.int32, (tm, LANES), 1)
        row = lax.broadcasted_iota(jnp.int32, (tm, LANES), 0)
        blkid = pl.program_id(1) * (tm // MOBA_BLOCK) + row // MOBA_BLOCK
        for gi in range(na // LANES):
            kg = kparts[gi]
            for e in range(2):
                in_head = (lane >= HEAD_DIM * e) & (lane < HEAD_DIM * (e + 1))
                onehot = (lane - HEAD_DIM * (1 - e)) == blkid
                col = (2 * gi + e) * LANES
                ka_ref[:, col:col + LANES] = jnp.where(in_head, kg, onehot.astype(F32)).astype(BF16)
            km = kg.reshape(tm // MOBA_BLOCK, MOBA_BLOCK, LANES).sum(axis=1) * (1.0 / MOBA_BLOCK)
            km_ref[:, gi * LANES:(gi + 1) * LANES] = km
    else:
        qb_ref, vb_ref, kb_ref = outs
    per = tm // d
    for r in range(d):
        for cb in range(per // DIL_BAND):
            sl = slice(r * per + cb * DIL_BAND, r * per + (cb + 1) * DIL_BAND)
            qb_ref[r, cb] = qT[na:na + WIDTH_G, sl]
            vb_ref[r, cb] = vT[na:na + WIDTH_G, sl]
        for gi in range(WIDTH_G // LANES):
            kb_ref[r, :, gi * LANES:(gi + 1) * LANES] = kparts[na // LANES + gi][r * per:(r + 1) * per].astype(BF16)


def _project(x, g, wqT, wk, wvT, d, na):
    B, S, D = x.shape
    L = S // d
    tm = PROJ_TM * max(1, d * DIL_BAND // PROJ_TM)
    per = tm // d
    cq, sq, ck, s1, s2 = _rope_tables(S, d, tm)
    grid = (B, S // tm)
    full = lambda a: pl.BlockSpec(a.shape, lambda b, i: (0,) * a.ndim)
    in_specs = [
        pl.BlockSpec((None, tm, D), lambda b, i: (b, i, 0)),
        full(g), full(wqT), full(wk), full(wvT),
        pl.BlockSpec((ROT_HALF, tm), lambda b, i: (0, i)),
        pl.BlockSpec((ROT_HALF, tm), lambda b, i: (0, i)),
        pl.BlockSpec((tm, LANES), lambda b, i: (i, 0)),
        pl.BlockSpec((tm, LANES), lambda b, i: (i, 0)),
        pl.BlockSpec((tm, LANES), lambda b, i: (i, 0)),
    ]
    nsub = per // DIL_BAND
    out_shape = [
        jax.ShapeDtypeStruct((B, d, L // DIL_BAND, WIDTH_G, DIL_BAND), BF16),
        jax.ShapeDtypeStruct((B, d, L // DIL_BAND, WIDTH_G, DIL_BAND), BF16),
        jax.ShapeDtypeStruct((B, d, L, WIDTH_G), BF16),
    ]
    out_specs = [
        pl.BlockSpec((None, d, nsub, WIDTH_G, DIL_BAND), lambda b, i: (b, 0, i, 0, 0)),
        pl.BlockSpec((None, d, nsub, WIDTH_G, DIL_BAND), lambda b, i: (b, 0, i, 0, 0)),
        pl.BlockSpec((None, d, per, WIDTH_G), lambda b, i: (b, 0, i, 0)),
    ]
    if na:
        nblk = tm // MOBA_BLOCK
        out_shape = [
            jax.ShapeDtypeStruct((B, S // MOBA_BLOCK, na, MOBA_BLOCK), BF16),
            jax.ShapeDtypeStruct((B, S // MOBA_BLOCK, na, MOBA_BLOCK), BF16),
            jax.ShapeDtypeStruct((B, S, 2 * na), BF16),
            jax.ShapeDtypeStruct((B, S // tm, nblk, na), F32),
        ] + out_shape
        out_specs = [
            pl.BlockSpec((None, nblk, na, MOBA_BLOCK), lambda b, i: (b, i, 0, 0)),
            pl.BlockSpec((None, nblk, na, MOBA_BLOCK), lambda b, i: (b, i, 0, 0)),
            pl.BlockSpec((None, tm, 2 * na), lambda b, i: (b, i, 0)),
            pl.BlockSpec((None, None, nblk, na), lambda b, i: (b, i, 0, 0)),
        ] + out_specs
    return pl.pallas_call(
        functools.partial(_proj_kernel, na=na, d=d),
        grid=grid, in_specs=in_specs, out_specs=out_specs, out_shape=out_shape,
        compiler_params=_cparams(("arbitrary",) * 2),
        name=f"proj_d{d}",
    )(x, g, wqT, wk, wvT, cq, sq, ck, s1, s2)


def _moba_kernel(q_ref, k_ref, v_ref, km_ref, o_ref, qa_scr, s_scr, p_scr):
    g = pl.program_id(2)
    nb = km_ref.shape[0]
    blk = MOBA_BLOCK
    tq = MOBA_QBLOCKS * blk
    q2 = jnp.concatenate([q_ref[i] for i in range(MOBA_QBLOCKS)], axis=1)
    bid = lax.broadcasted_iota(jnp.int32, (nb, tq), 0)
    qblk = MOBA_QBLOCKS * g + lax.broadcasted_iota(jnp.int32, (nb, tq), 1) // blk
    key_i = lax.broadcasted_iota(jnp.int32, (blk, tq), 0)
    qry_i = lax.broadcasted_iota(jnp.int32, (blk, tq), 1)
    km = km_ref[...].astype(BF16)
    zq = jnp.zeros((HEAD_DIM, tq), BF16)
    zb = jnp.zeros((HEAD_DIM - nb, tq), BF16)

    state = []
    for h in range(2):
        hl = slice(LANES * h, LANES * (h + 1))
        vl = slice(HEAD_DIM * h, HEAD_DIM * (h + 1))
        qh = q2[HEAD_DIM * h:HEAD_DIM * (h + 1)]
        q_plain = jnp.concatenate([qh, zq] if h == 0 else [zq, qh], axis=0)
        gt = jnp.where(bid < qblk, _dot(km, q_plain), -jnp.inf)
        sel = jnp.zeros(gt.shape, jnp.bool_)
        for _ in range(MOBA_TOPK):
            mx = jnp.max(gt, axis=0, keepdims=True)
            first = jnp.min(jnp.where((gt == mx) & (mx > -jnp.inf), bid, nb), axis=0, keepdims=True)
            pick = bid == first
            sel = sel | pick
            gt = jnp.where(pick, -jnp.inf, gt)
        bias = jnp.where(sel, 0.0, NEG).astype(BF16)
        qa_scr[h] = jnp.concatenate([qh, bias, zb] if h == 0 else [bias, zb, qh], axis=0)
        bias_d = jnp.where(sel | (bid == qblk), 0.0, NEG).astype(BF16)
        q_diag = jnp.concatenate([qh, bias_d, zb] if h == 0 else [bias_d, zb, qh], axis=0)
        ss = []
        for i in range(MOBA_QBLOCKS):
            j = MOBA_QBLOCKS * g + i
            sd = _dot(k_ref[pl.ds(pl.multiple_of(j * blk, blk), blk), hl], q_diag)
            own = (qry_i >= i * blk) & (qry_i < (i + 1) * blk)
            ss.append(jnp.where(own & (key_i > qry_i - i * blk), NEG, sd))
        m = ss[0].max(axis=0, keepdims=True)
        for sd in ss[1:]:
            m = jnp.maximum(m, sd.max(axis=0, keepdims=True))
        l = jnp.zeros((1, tq), F32)
        acc = jnp.zeros((HEAD_DIM, tq), F32)
        for i, sd in enumerate(ss):
            p = jnp.exp2(sd - m)
            l = l + jnp.sum(p, axis=0, keepdims=True)
            acc = acc + _dot(v_ref[MOBA_QBLOCKS * g + i, vl, :], p.astype(BF16))
        state += [m, l, acc]

    def blocks(c):
        return [jnp.where((c >= 0) & (c < g), MOBA_QBLOCKS * c + i, nb - 1) for i in range(MOBA_QBLOCKS)]

    def scores(c, h):
        hl = slice(LANES * h, LANES * (h + 1))
        qa = qa_scr[h]
        return [_dot(k_ref[pl.ds(pl.multiple_of(j * blk, blk), blk), hl], qa) for j in blocks(c)]

    def values(c, h, alpha, acc):
        vl = slice(HEAD_DIM * h, HEAD_DIM * (h + 1))
        acc = alpha * acc
        for i, j in enumerate(blocks(c)):
            acc = acc + _dot(v_ref[j, vl, :], p_scr[h, i])
        return acc

    for h in range(2):
        for i, sc in enumerate(scores(0, h)):
            s_scr[h, i] = sc
    p_scr[...] = jnp.zeros(p_scr.shape, BF16)

    def body(it, carry):
        alphas, ms, ls, accs = carry
        a_new, m_new, l_new, acc_new = [], [], [], []
        for h in range(2):
            s_next = scores(it + 1, h)
            s_cur = [s_scr[h, i] for i in range(MOBA_QBLOCKS)]
            acc_new.append(values(it - 1, h, alphas[h], accs[h]))
            mn = ms[h]
            for sc in s_cur:
                mn = jnp.maximum(mn, jnp.max(sc, axis=0, keepdims=True))
            alpha = jnp.exp2(ms[h] - mn)
            l = alpha * ls[h]
            for i in range(MOBA_QBLOCKS):
                p = jnp.exp2(s_cur[i] - mn)
                l = l + jnp.sum(p, axis=0, keepdims=True)
                s_scr[h, i] = s_next[i]
                p_scr[h, i] = p.astype(BF16)
            m_new.append(mn)
            l_new.append(l)
            a_new.append(alpha)
        return (tuple(a_new), tuple(m_new), tuple(l_new), tuple(acc_new))

    one = jnp.ones((1, tq), F32)
    init = ((one, one), (state[0], state[3]), (state[1], state[4]), (state[2], state[5]))
    alphas, _, ls, accs = lax.fori_loop(0, g, body, init)
    accs = [values(g - 1, h, alphas[h], accs[h]) for h in range(2)]
    oT = jnp.concatenate([accs[0] / ls[0], accs[1] / ls[1]], axis=0)
    o_ref[...] = oT.T.astype(o_ref.dtype)


def _moba(qT, k, vT, kmean):
    B, nb, wa, blk = qT.shape
    S = nb * blk
    hp = wa // LANES
    nq = MOBA_QBLOCKS
    assert nb % 16 == 0 and nb <= HEAD_DIM, "block-mask rows must fit the spare half of a head pair"
    return pl.pallas_call(
        _moba_kernel,
        grid=(B, hp, nb // nq),
        in_specs=[
            pl.BlockSpec((None, nq, LANES, blk), lambda b, p, n: (b, n, p, 0)),
            pl.BlockSpec((None, S, 2 * LANES), lambda b, p, n: (b, 0, p)),
            pl.BlockSpec((None, nb, LANES, blk), lambda b, p, n: (b, 0, p, 0)),
            pl.BlockSpec((None, nb, LANES), lambda b, p, n: (b, 0, p)),
        ],
        out_specs=pl.BlockSpec((None, nq * blk, LANES), lambda b, p, n: (b, n, p)),
        out_shape=jax.ShapeDtypeStruct((B, S, wa), BF16),
        scratch_shapes=[pltpu.VMEM((2, LANES, nq * blk), BF16), pltpu.VMEM((2, nq, blk, nq * blk), F32),
                        pltpu.VMEM((2, nq, blk, nq * blk), BF16)],
        compiler_params=_cparams(("arbitrary",) * 3),
        name="moba_attn",
    )(qT, k, vT, kmean)


def _dil_kernel(q_ref, k_ref, kp_ref, v_ref, vp_ref, o_ref, lse_ref):
    t = pl.program_id(2)
    band = DIL_BAND
    nsub = q_ref.shape[0]
    key_i = lax.broadcasted_iota(jnp.int32, (band, band), 0)
    qry_i = lax.broadcasted_iota(jnp.int32, (band, band), 1)
    rowid = lax.broadcasted_iota(jnp.int32, (LANES, band), 0)
    own_ok = key_i <= qry_i
    prev_ok = key_i >= qry_i
    for c in range(nsub):
        o_parts, l_parts = [], []
        for hp in range(WIDTH_G // LANES):
            cols = slice(hp * LANES, (hp + 1) * LANES)
            k_own = k_ref[c * band:(c + 1) * band, cols]
            k_prev = kp_ref[:, cols] if c == 0 else k_ref[(c - 1) * band:c * band, cols]
            q2 = q_ref[c, cols, :]
            for h in range(2):
                qh = jnp.where((rowid >= HEAD_DIM * h) & (rowid < HEAD_DIM * (h + 1)), q2, jnp.zeros_like(q2))
                pmask = (prev_ok & (t > 0)) if c == 0 else prev_ok
                s_own = jnp.where(own_ok, _dot(k_own, qh), -jnp.inf)
                s_prev = jnp.where(pmask, _dot(k_prev, qh), -jnp.inf)
                m = jnp.maximum(jnp.max(s_own, axis=0, keepdims=True), jnp.max(s_prev, axis=0, keepdims=True))
                p_own = jnp.exp(s_own - m)
                p_prev = jnp.exp(s_prev - m)
                l = jnp.sum(p_own, axis=0, keepdims=True) + jnp.sum(p_prev, axis=0, keepdims=True)
                rows = slice(hp * LANES + h * HEAD_DIM, hp * LANES + (h + 1) * HEAD_DIM)
                v_own = v_ref[c, rows, :]
                v_prev = vp_ref[rows, :] if c == 0 else v_ref[c - 1, rows, :]
                oT = _dot(v_own, p_own.astype(BF16)) + _dot(v_prev, p_prev.astype(BF16))
                o_parts.append(oT / l)
                l_parts.append(jnp.broadcast_to(m + jnp.log(l), (HEAD_DIM, band)))
        o_ref[c * band:(c + 1) * band, :] = jnp.concatenate(o_parts, axis=0).T
        lse_ref[c * band:(c + 1) * band, :] = jnp.concatenate(l_parts, axis=0).T


def _dilated(qT, k, vT):
    B, d, nblk, wg, band = qT.shape
    L = nblk * band
    tl = min(DIL_TL, L)
    nsub = tl // band
    prev = lambda t: jnp.maximum(t * nsub - 1, 0)
    return pl.pallas_call(
        _dil_kernel,
        grid=(B, d, L // tl),
        in_specs=[
            pl.BlockSpec((None, None, nsub, wg, band), lambda b, r, t: (b, r, t, 0, 0)),
            pl.BlockSpec((None, None, tl, wg), lambda b, r, t: (b, r, t, 0)),
            pl.BlockSpec((None, None, band, wg), lambda b, r, t: (b, r, prev(t), 0)),
            pl.BlockSpec((None, None, nsub, wg, band), lambda b, r, t: (b, r, t, 0, 0)),
            pl.BlockSpec((None, None, None, wg, band), lambda b, r, t: (b, r, prev(t), 0, 0)),
        ],
        out_specs=[
            pl.BlockSpec((None, None, tl, wg), lambda b, r, t: (b, r, t, 0)),
            pl.BlockSpec((None, None, tl, wg), lambda b, r, t: (b, r, t, 0)),
        ],
        out_shape=[jax.ShapeDtypeStruct((B, d, L, wg), F32)] * 2,
        compiler_params=_cparams(("arbitrary",) * 3),
        name=f"dilated_d{d}",
    )(qT, k, k, vT, vT)


def _interleave(ref, scr):
    d, per, w = ref.shape
    if d == 1:
        return ref[0]
    for r in range(d):
        for sl in range(w // LANES):
            scr[sl, pl.ds(r, per, stride=d), :] = ref[r, :, sl * LANES:(sl + 1) * LANES]
    return jnp.concatenate([scr[sl] for sl in range(w // LANES)], axis=1)


def _post_kernel(x_ref, oa_ref, o1_ref, o2_ref, o3_ref, l1_ref, l2_ref, l3_ref, gmix_ref, wg_ref, bga_ref,
                 bgb_ref, woa_ref, wob_ref, wout_ref, gffn_ref, wr_ref, br_ref,
                 x1_ref, h2_ref, idx_ref, rank_ref, pnat_ref, cnt_ref, carry_scr, il_scr):
    i = pl.program_id(0)
    tm, D = x_ref.shape
    ne = wr_ref.shape[0]

    @pl.when(i == 0)
    def _():
        carry_scr[...] = jnp.zeros_like(carry_scr)

    x = x_ref[...]
    h = _rms(x, gmix_ref[...]).astype(BF16)
    gates = _dot(h, wg_ref[...])
    ga = gates[:, :D] + bga_ref[...]
    gb = gates[:, D:] + bgb_ref[...]
    l1, l2, l3 = [_interleave(r, il_scr.at[n]) for n, r in enumerate((l1_ref, l2_ref, l3_ref))]
    o1, o2, o3 = [_interleave(r, il_scr.at[3 + n]) for n, r in enumerate((o1_ref, o2_ref, o3_ref))]
    mx = jnp.maximum(jnp.maximum(l1, l2), l3)
    e1, e2, e3 = jnp.exp(l1 - mx), jnp.exp(l2 - mx), jnp.exp(l3 - mx)
    ob = (e1 * o1 + e2 * o2 + e3 * o3) / (e1 + e2 + e3)
    ya = _dot(oa_ref[...], woa_ref[...])
    yb = _dot(ob.astype(BF16), wob_ref[...])
    mix = jax.nn.sigmoid(ga) * ya + jax.nn.sigmoid(gb) * yb
    x1 = x + _dot(mix.astype(BF16), wout_ref[...])
    x1_ref[...] = x1
    h2 = _rms(x1, gffn_ref[...])
    _rows_to_tiles(h2_ref, h2)

    wr = wr_ref[...]
    wr_hi = wr.astype(BF16)
    wr_lo = (wr - wr_hi.astype(F32)).astype(BF16)
    h2_hi = h2.astype(BF16)
    h2_lo = (h2 - h2_hi.astype(F32)).astype(BF16)
    logits = _dot_nt(wr_hi, h2_hi) + (_dot_nt(wr_hi, h2_lo) + _dot_nt(wr_lo, h2_hi)) + br_ref[...]
    eid = lax.broadcasted_iota(jnp.int32, (ne, tm), 0)
    g = logits
    vals, idxs, picks = [], [], []
    for _ in range(TOP_K):
        m = jnp.max(g, axis=0, keepdims=True)
        first = jnp.min(jnp.where(g == m, eid, ne), axis=0, keepdims=True)
        pick = eid == first
        vals.append(m)
        idxs.append(first)
        picks.append(pick)
        g = jnp.where(pick, -jnp.inf, g)
    es = [jnp.exp(v - vals[0]) for v in vals]
    den = es[0] + es[1] + es[2] + es[3]
    probs = [e / den for e in es]

    onehot = jnp.zeros((ne, tm), F32)
    for pick in picks:
        onehot = onehot + pick.astype(F32)
    earlier = (lax.broadcasted_iota(jnp.int32, (tm, tm), 0) < lax.broadcasted_iota(jnp.int32, (tm, tm), 1))
    prefix = _dot(onehot.astype(BF16), earlier.astype(BF16)) + carry_scr[:, 0:1]
    ranks = [jnp.sum(jnp.where(pick, prefix, 0.0), axis=0, keepdims=True) for pick in picks]
    carry_scr[...] = carry_scr[...] + jnp.sum(onehot, axis=1, keepdims=True)
    cnt_ref[...] = carry_scr[...]

    zi = jnp.zeros((8 - TOP_K, tm), jnp.int32)
    idx_ref[...] = jnp.concatenate(idxs + [zi], axis=0)
    rank_ref[...] = jnp.concatenate([r.astype(jnp.int32) for r in ranks] + [zi], axis=0)
    pnat_ref[...] = jnp.concatenate(probs + [jnp.zeros((LANES - TOP_K, tm), F32)], axis=0).T


def _post(x2, oa, obs, lses, gmix, wg, bga, bgb, woa, wob, wout, gffn, wrT, br):
    T, D = x2.shape
    tm = POST_TM
    ne = wrT.shape[0]
    row = lambda w: pl.BlockSpec((tm, w), lambda i: (i, 0))

    def grouped(a):
        _, d, L, w = a.shape
        nt = L * d // tm
        return pl.BlockSpec((None, d, tm // d, w), lambda i: (i // nt, 0, i % nt, 0))

    full = lambda a: pl.BlockSpec(a.shape, lambda i: (0,) * a.ndim)
    col = pl.BlockSpec((8, tm), lambda i: (0, i))
    return pl.pallas_call(
        _post_kernel,
        grid=(T // tm,),
        in_specs=[row(D), row(WIDTH_A)] + [grouped(a) for a in (*obs, *lses)]
                 + [full(a) for a in (gmix, wg, bga, bgb, woa, wob, wout, gffn, wrT, br)],
        out_specs=[row(D), pl.BlockSpec((tm * SUBLANES, LANES), lambda i: (i, 0)), col, col, row(LANES),
                   pl.BlockSpec((ne, LANES), lambda i: (0, 0))],
        out_shape=[
            jax.ShapeDtypeStruct((T, D), F32), jax.ShapeDtypeStruct((T * SUBLANES, LANES), F32),
            jax.ShapeDtypeStruct((8, T), jnp.int32), jax.ShapeDtypeStruct((8, T), jnp.int32),
            jax.ShapeDtypeStruct((T, LANES), F32),
            jax.ShapeDtypeStruct((ne, LANES), F32),
        ],
        scratch_shapes=[pltpu.VMEM((ne, LANES), F32), pltpu.VMEM((6, WIDTH_G // LANES, tm, LANES), F32)],
        compiler_params=_cparams(("arbitrary",)),
        name="post_mix_router",
    )(x2, oa, *obs, *lses, gmix, wg, bga, bgb, woa, wob, wout, gffn, wrT, br)


def _dispatch_kernel(pend_ref, padded_ref, dest_ref, h_ref, xr_ref, zero_scr, sem):
    i = pl.program_id(0)
    tm = h_ref.shape[0] // SUBLANES
    br = zero_scr.shape[0]

    def zero_copy(blk_start):
        return pltpu.make_async_copy(zero_scr, xr_ref.at[pl.ds(pl.multiple_of(blk_start * SUBLANES, br), br), :], sem)

    @pl.when(i == 0)
    def _():
        zero_scr[...] = jnp.zeros_like(zero_scr)
        n_used = pend_ref[N_EXPERTS - 1] // EXPERT_ROWS
        n_blk = xr_ref.shape[0] // br

        def tail(fn):
            def go(e, c):
                @pl.when(padded_ref[e] > 0)
                def _():
                    fn(zero_copy(pend_ref[e] - EXPERT_ROWS))
                return c
            return go

        def unused(fn):
            def go(b, c):
                fn(zero_copy(b * EXPERT_ROWS))
                return c
            return go

        lax.fori_loop(0, N_EXPERTS, tail(lambda cp: cp.start()), 0)
        lax.fori_loop(n_used, n_blk, unused(lambda cp: cp.start()), 0)
        lax.fori_loop(0, N_EXPERTS, tail(lambda cp: cp.wait()), 0)
        lax.fori_loop(n_used, n_blk, unused(lambda cp: cp.wait()), 0)

    def row_copy(t, k):
        src = h_ref.at[pl.ds(pl.multiple_of(t * SUBLANES, SUBLANES), SUBLANES), :]
        dst = xr_ref.at[pl.ds(pl.multiple_of(dest_ref[k, t] * SUBLANES, SUBLANES), SUBLANES), :]
        return pltpu.make_async_copy(src, dst, sem)

    def start(tb, c):
        for u in range(ROW_UNROLL):
            for k in range(TOP_K):
                row_copy(tb * ROW_UNROLL + u, k).start(priority=(u * TOP_K + k) % 2)
        return c

    def wait(tb, c):
        for u in range(ROW_UNROLL):
            for k in range(TOP_K):
                row_copy(tb * ROW_UNROLL + u, k).wait()
        return c

    lax.fori_loop(0, tm // ROW_UNROLL, start, 0)
    lax.fori_loop(0, tm // ROW_UNROLL, wait, 0)


def _dispatch(pend, padded, dest3, h2t, n_rows):
    tm = DISPATCH_TM
    return pl.pallas_call(
        _dispatch_kernel,
        grid_spec=pltpu.PrefetchScalarGridSpec(
            num_scalar_prefetch=2,
            grid=(h2t.shape[0] // (tm * SUBLANES),),
            in_specs=[
                pl.BlockSpec((None, TOP_K, tm), lambda i, pe, pa: (i, 0, 0), memory_space=pltpu.SMEM),
                pl.BlockSpec((tm * SUBLANES, LANES), lambda i, pe, pa: (i, 0)),
            ],
            out_specs=pl.BlockSpec(memory_space=pl.ANY),
            scratch_shapes=[pltpu.VMEM((EXPERT_ROWS * SUBLANES, LANES), F32), pltpu.SemaphoreType.DMA(())],
        ),
        out_shape=jax.ShapeDtypeStruct((n_rows * SUBLANES, LANES), F32),
        compiler_params=_cparams(("arbitrary",)),
        name="moe_dispatch",
    )(pend, padded, dest3, h2t)


def _expert_kernel(be_ref, nu_ref, x_ref, wg_ref, bg_ref, wu_ref, bu_ref, wd_ref, bd_ref, y_ref,
                   wgb_scr, wub_scr, wdb_scr):
    b = pl.program_id(0)
    used = b < nu_ref[0]
    new_expert = (b == 0) | (be_ref[b] != be_ref[jnp.maximum(b - 1, 0)])

    @pl.when(used & new_expert)
    def _():
        wgb_scr[...] = wg_ref[...].astype(BF16)
        wub_scr[...] = wu_ref[...].astype(BF16)
        wdb_scr[...] = wd_ref[...].astype(BF16)

    @pl.when(used)
    def _():
        x = _tiles_to_rows(x_ref, EXPERT_ROWS).astype(BF16)
        g = _dot(x, wgb_scr[...]) + bg_ref[...]
        u = _dot(x, wub_scr[...]) + bu_ref[...]
        g = jnp.minimum(g, SWIGLU_LIMIT)
        u = jnp.clip(u, -SWIGLU_LIMIT, SWIGLU_LIMIT)
        a = g * jax.nn.sigmoid(SWIGLU_ALPHA * g) * (u + 1.0)
        _rows_to_tiles(y_ref, _dot(a.astype(BF16), wdb_scr[...]) + bd_ref[...])

    @pl.when(jnp.logical_not(used))
    def _():
        y_ref[...] = jnp.zeros_like(y_ref)


def _experts(blk_expert, n_used, x_rows, w_gate, b_gate, w_up, b_up, w_down, b_down):
    E, D, F = w_gate.shape
    n_rows = x_rows.shape[0] // SUBLANES
    br = EXPERT_ROWS
    tb = br * SUBLANES
    wspec = lambda shape: pl.BlockSpec((None,) + shape, lambda b, be, nu: (be[b], 0, 0))
    return pl.pallas_call(
        _expert_kernel,
        grid_spec=pltpu.PrefetchScalarGridSpec(
            num_scalar_prefetch=2,
            grid=(n_rows // br,),
            in_specs=[
                pl.BlockSpec((tb, LANES), lambda b, be, nu: (jnp.minimum(b, nu[0] - 1), 0)),
                wspec((D, F)), wspec((1, F)), wspec((D, F)), wspec((1, F)), wspec((F, D)), wspec((1, D)),
            ],
            out_specs=pl.BlockSpec((tb, LANES), lambda b, be, nu: (b, 0)),
            scratch_shapes=[pltpu.VMEM((D, F), BF16), pltpu.VMEM((D, F), BF16), pltpu.VMEM((F, D), BF16)],
        ),
        out_shape=jax.ShapeDtypeStruct((n_rows * SUBLANES, LANES), F32),
        compiler_params=_cparams(("arbitrary",)),
        name="moe_experts",
    )(blk_expert, n_used, x_rows, w_gate, b_gate.reshape(E, 1, F), w_up, b_up.reshape(E, 1, F),
      w_down, b_down.reshape(E, 1, D))


def _combine_kernel(dest_ref, y_ref, x1_ref, p_ref, g_ref, o_ref, ybuf, sem):
    tm = x1_ref.shape[0]

    def row_copy(t, k):
        src = y_ref.at[pl.ds(pl.multiple_of(dest_ref[k, t] * SUBLANES, SUBLANES), SUBLANES), :]
        dst = ybuf.at[k, pl.ds(pl.multiple_of(t * SUBLANES, SUBLANES), SUBLANES), :]
        return pltpu.make_async_copy(src, dst, sem)

    def start(tb, c):
        for u in range(ROW_UNROLL):
            for k in range(TOP_K):
                row_copy(tb * ROW_UNROLL + u, k).start(priority=(u * TOP_K + k) % 2)
        return c

    def wait(tb, c):
        for u in range(ROW_UNROLL):
            for k in range(TOP_K):
                row_copy(tb * ROW_UNROLL + u, k).wait()
        return c

    lax.fori_loop(0, tm // ROW_UNROLL, start, 0)
    lax.fori_loop(0, tm // ROW_UNROLL, wait, 0)
    p = p_ref[...]
    y = x1_ref[...]
    for k in range(TOP_K):
        y = y + p[:, k:k + 1] * _tiles_to_rows(ybuf.at[k], tm)
    o_ref[...] = _rms(y, g_ref[...])


def _combine(dest3, y_rows, x1, pnat, g_final):
    T, D = x1.shape
    tm = COMBINE_TM
    return pl.pallas_call(
        _combine_kernel,
        grid=(T // tm,),
        in_specs=[
            pl.BlockSpec((None, TOP_K, tm), lambda i: (i, 0, 0), memory_space=pltpu.SMEM),
            pl.BlockSpec(memory_space=pl.ANY),
            pl.BlockSpec((tm, D), lambda i: (i, 0)),
            pl.BlockSpec((tm, LANES), lambda i: (i, 0)),
            pl.BlockSpec((1, D), lambda i: (0, 0)),
        ],
        out_specs=pl.BlockSpec((tm, D), lambda i: (i, 0)),
        out_shape=jax.ShapeDtypeStruct((T, D), F32),
        scratch_shapes=[pltpu.VMEM((TOP_K, tm * SUBLANES, LANES), F32), pltpu.SemaphoreType.DMA(())],
        compiler_params=_cparams(("arbitrary",)),
        name="moe_combine",
    )(dest3, y_rows, x1, pnat, g_final)


def _rope_tables(S, d, tm):
    per = tm // d

    def angle(shape, ax, f):
        rho = lax.broadcasted_iota(jnp.int32, shape, ax)
        w = rho % tm
        pos = ((rho // tm) * per + w % per) * d + w // per
        inv = jnp.float32(ROPE_THETA) ** (-f.astype(F32) / ROT_HALF)
        return pos.astype(F32) * inv

    sh = (ROT_HALF, S)
    ang = angle(sh, 1, lax.broadcasted_iota(jnp.int32, sh, 0))
    cq, sq = jnp.cos(ang), jnp.sin(ang)
    sh = (S, LANES)
    lane = lax.broadcasted_iota(jnp.int32, sh, 1) % HEAD_DIM
    ang = angle(sh, 0, lane % ROT_HALF)
    cos, sin = jnp.cos(ang), jnp.sin(ang)
    ck = jnp.where(lane < 2 * ROT_HALF, cos, 1.0)
    s1 = jnp.where(lane < ROT_HALF, -sin, 0.0)
    s2 = jnp.where((lane >= ROT_HALF) & (lane < 2 * ROT_HALF), sin, 0.0)
    return cq, sq, ck, s1, s2


def _layer(x, ln_mix_g, w_in, b_gate_a, b_gate_b, w_o_a, w_o_b, w_out, ln_ffn_g, w_router, b_router,
           w_gate, b_gate, w_up, b_up, w_down, b_down, ln_out_g):
    B, S, D = x.shape
    T = B * S
    qa0, ka0, va0 = 0, WIDTH_A, 2 * WIDTH_A
    qb0, kb0, vb0 = 3 * WIDTH_A, 3 * WIDTH_A + WIDTH_B, 3 * WIDTH_A + 2 * WIDTH_B
    g0 = 3 * WIDTH_A + 3 * WIDTH_B
    cols = lambda s, w: w_in[:, s:s + w]
    gmix = ln_mix_g.reshape(1, D)

    obs, lses = [], []
    o_a = None
    for gi, (_, d) in enumerate(DIL_PAIRS):
        off = gi * WIDTH_G
        wq, wk, wv = cols(qb0 + off, WIDTH_G), cols(kb0 + off, WIDTH_G), cols(vb0 + off, WIDTH_G)
        na = 0
        if gi == 0:
            na = WIDTH_A
            wq = jnp.concatenate([cols(qa0, WIDTH_A), wq], axis=1)
            wk = jnp.concatenate([cols(ka0, WIDTH_A), wk], axis=1)
            wv = jnp.concatenate([cols(va0, WIDTH_A), wv], axis=1)
        outs = _project(x, gmix, wq.T.astype(BF16), wk.astype(BF16), wv.T.astype(BF16), d, na)
        if gi == 0:
            qTa, vTa, ka, kmean = outs[:4]
            outs = outs[4:]
            o_a = _moba(qTa, ka, vTa, kmean.reshape(B, S // MOBA_BLOCK, WIDTH_A))
        qTb, vTb, kb = outs
        o_g, lse_g = _dilated(qTb, kb, vTb)
        obs.append(o_g)
        lses.append(lse_g)

    x1, h2, idxT, rankT, pnat, cnt = _post(
        x.reshape(T, D), o_a.reshape(T, WIDTH_A), obs, lses, gmix,
        cols(g0, 2 * D).astype(BF16), b_gate_a.reshape(1, D), b_gate_b.reshape(1, D),
        w_o_a.astype(BF16), w_o_b.astype(BF16), w_out.astype(BF16), ln_ffn_g.reshape(1, D),
        w_router.T, b_router.reshape(N_EXPERTS, 1))

    br = EXPERT_ROWS
    counts = cnt[:, 0].astype(jnp.int32)
    padded = (counts + br - 1) // br * br
    pend = jnp.cumsum(padded)
    pstart = pend - padded
    eids = jnp.arange(N_EXPERTS, dtype=jnp.int32)[:, None, None]
    dest = jnp.sum(jnp.where(idxT[None, :TOP_K] == eids, pstart[:, None, None], 0), axis=0) + rankT[:TOP_K]
    n_rows = T * TOP_K + N_EXPERTS * br
    nblk = n_rows // br
    n_used = pend[-1] // br
    first_row = jnp.arange(nblk, dtype=jnp.int32) * br
    be = jnp.minimum(jnp.sum((pend[None, :] <= first_row[:, None]).astype(jnp.int32), axis=1), N_EXPERTS - 1)
    be = jnp.where(jnp.arange(nblk) < n_used, be, be[n_used - 1]).astype(jnp.int32)

    dest_d = dest.reshape(TOP_K, T // DISPATCH_TM, DISPATCH_TM).transpose(1, 0, 2)
    x_rows = _dispatch(pend.astype(jnp.int32), padded.astype(jnp.int32), dest_d, h2, n_rows)
    y_rows = _experts(be, n_used.reshape(1).astype(jnp.int32), x_rows, w_gate, b_gate, w_up, b_up, w_down, b_down)
    dest_c = dest.reshape(TOP_K, T // COMBINE_TM, COMBINE_TM).transpose(1, 0, 2)
    out = _combine(dest_c, y_rows, x1, pnat, ln_out_g.reshape(1, D))
    return out.reshape(B, S, D)


def kernel(x, ln_mix_g, w_in, b_gate_a, b_gate_b, w_o_a, w_o_b, w_out, ln_ffn_g, w_router, b_router,
           w_gate, b_gate, w_up, b_up, w_down, b_down, ln_final_g):
    depth = ln_mix_g.shape[0]
    assert depth == 1, "the final RMSNorm is fused into the last layer's combine"
    return _layer(x, ln_mix_g[0], w_in[0], b_gate_a[0], b_gate_b[0], w_o_a[0], w_o_b[0], w_out[0],
                  ln_ffn_g[0], w_router[0], b_router[0], w_gate[0], b_gate[0], w_up[0], b_up[0],
                  w_down[0], b_down[0], ln_final_g)
```

```python
import functools

import jax
import jax.numpy as jnp
from jax import lax
from jax.experimental import pallas as pl
from jax.experimental.pallas import tpu as pltpu

D_MODEL = 1024
HEAD_DIM = 64
ROT_HALF = HEAD_DIM // 8
ROPE_THETA = 500000.0
N_HEADS_A = 8
MOBA_BLOCK = 256
MOBA_TOPK = 3
MOBA_QBLOCKS = 2
DIL_PAIRS = ((128, 1), (512, 4), (2048, 16))
DIL_BAND = 128
HEADS_PER_GROUP_B = 4
WIDTH_A = N_HEADS_A * HEAD_DIM
WIDTH_G = HEADS_PER_GROUP_B * HEAD_DIM
WIDTH_B = WIDTH_G * len(DIL_PAIRS)
N_EXPERTS = 32
TOP_K = 4
SWIGLU_LIMIT = 7.0
SWIGLU_ALPHA = 1.702
NORM_EPS = 1e-5
SCALE = HEAD_DIM ** -0.5
LOG2E = 1.4426950408889634

LANES = 128
PROJ_TM = 512
PERM_TILE = 512
DIL_TL = 512
POST_TM = 512
EXPERT_ROWS = 256
DISPATCH_TM = 512
COMBINE_TM = 256
SUBLANES = 8
ROW_UNROLL = 8
VMEM_LIMIT = 56 * 1024 * 1024
NEG = -1e30

BF16 = jnp.bfloat16
F32 = jnp.float32


def _dot(a, b):
    return jnp.dot(a, b, preferred_element_type=F32)


def _dot_nt(a, b):
    return lax.dot_general(a, b, (((1,), (1,)), ((), ())), preferred_element_type=F32)


def _rms(x, g):
    ms = jnp.mean(x * x, axis=-1, keepdims=True)
    return x * lax.rsqrt(ms + NORM_EPS) * g


def _rows_to_tiles(ref, val):
    n = val.shape[0]
    for c in range(val.shape[1] // LANES):
        ref[pl.ds(c, n, stride=SUBLANES), :] = val[:, c * LANES:(c + 1) * LANES]


def _tiles_to_rows(ref, n):
    return jnp.concatenate([ref[pl.ds(c, n, stride=SUBLANES), :] for c in range(SUBLANES)], axis=1)


def _cparams(sem):
    return pltpu.CompilerParams(dimension_semantics=sem, vmem_limit_bytes=VMEM_LIMIT)


def _deinterleave(h, d):
    tm = h.shape[0]
    sub = min(tm, PERM_TILE)
    cs = sub // d
    ri = lax.broadcasted_iota(jnp.int32, (sub, sub), 0)
    ui = lax.broadcasted_iota(jnp.int32, (sub, sub), 1)
    perm = (ui == (ri % cs) * d + ri // cs).astype(BF16)
    slabs = [_dot(perm, h[s0:s0 + sub]).astype(BF16) for s0 in range(0, tm, sub)]
    return jnp.concatenate([sl[r * cs:(r + 1) * cs] for r in range(d) for sl in slabs], axis=0)


def _proj_kernel(x_ref, g_ref, wq_ref, wk_ref, wv_ref, cq_ref, sq_ref, ck_ref, s1_ref, s2_ref, *outs, na, d):
    tm = x_ref.shape[0]
    h = _rms(x_ref[...], g_ref[...]).astype(BF16)
    if d > 1:
        h = _deinterleave(h, d)
    nq = wq_ref.shape[0]
    heads = nq // HEAD_DIM

    qT = _dot_nt(wq_ref[...], h)
    q3 = qT.reshape(heads, HEAD_DIM, tm)
    c = cq_ref[...][None]
    s = sq_ref[...][None]
    x1 = q3[:, 0:ROT_HALF]
    x2 = q3[:, ROT_HALF:2 * ROT_HALF]
    q3 = jnp.concatenate([x1 * c - x2 * s, x2 * c + x1 * s, q3[:, 2 * ROT_HALF:]], axis=1)
    hid = lax.broadcasted_iota(jnp.int32, (heads, 1, 1), 0)
    q3 = q3 * jnp.where(hid < na // HEAD_DIM, SCALE * LOG2E, SCALE)
    qT = q3.reshape(nq, tm).astype(BF16)

    vT = _dot_nt(wv_ref[...], h).astype(BF16)

    kk = _dot(h, wk_ref[...])
    ck, s1, s2 = ck_ref[...], s1_ref[...], s2_ref[...]
    kparts = []
    for gi in range(kk.shape[1] // LANES):
        kg = kk[:, gi * LANES:(gi + 1) * LANES]
        kparts.append(kg * ck + pltpu.roll(kg, LANES - ROT_HALF, 1) * s1 + pltpu.roll(kg, ROT_HALF, 1) * s2)

    if na:
        qa_ref, va_ref, ka_ref, km_ref, qb_ref, vb_ref, kb_ref = outs
        for blk in range(tm // MOBA_BLOCK):
            sl = slice(blk * MOBA_BLOCK, (blk + 1) * MOBA_BLOCK)
            qa_ref[blk] = qT[0:na, sl]
            va_ref[blk] = vT[0:na, sl]
        lane = lax.broadcasted_iota(jnp.int32, (tm, LANES), 1)
        row = lax.broadcasted_iota(jnp.int32, (tm, LANES), 0)
        blkid = pl.program_id(1) * (tm // MOBA_BLOCK) + row // MOBA_BLOCK
        for gi in range(na // LANES):
            kg = kparts[gi]
            for e in range(2):
                in_head = (lane >= HEAD_DIM * e) & (lane < HEAD_DIM * (e + 1))
                onehot = (lane - HEAD_DIM * (1 - e)) == blkid
                col = (2 * gi + e) * LANES
                ka_ref[:, col:col + LANES] = jnp.where(in_head, kg, onehot.astype(F32)).astype(BF16)
            km = kg.reshape(tm // MOBA_BLOCK, MOBA_BLOCK, LANES).sum(axis=1) * (1.0 / MOBA_BLOCK)
            km_ref[:, gi * LANES:(gi + 1) * LANES] = km
    else:
        qb_ref, vb_ref, kb_ref = outs
    per = tm // d
    for r in range(d):
        for cb in range(per // DIL_BAND):
            sl = slice(r * per + cb * DIL_BAND, r * per + (cb + 1) * DIL_BAND)
            qb_ref[r, cb] = qT[na:na + WIDTH_G, sl]
            vb_ref[r, cb] = vT[na:na + WIDTH_G, sl]
        for gi in range(WIDTH_G // LANES):
            kb_ref[r, :, gi * LANES:(gi + 1) * LANES] = kparts[na // LANES + gi][r * per:(r + 1) * per].astype(BF16)


def _project(x, g, wqT, wk, wvT, d, na):
    B, S, D = x.shape
    L = S // d
    tm = PROJ_TM * max(1, d * DIL_BAND // PROJ_TM)
    per = tm // d
    cq, sq, ck, s1, s2 = _rope_tables(S, d, tm)
    grid = (B, S // tm)
    full = lambda a: pl.BlockSpec(a.shape, lambda b, i: (0,) * a.ndim)
    in_specs = [
        pl.BlockSpec((None, tm, D), lambda b, i: (b, i, 0)),
        full(g), full(wqT), full(wk), full(wvT),
        pl.BlockSpec((ROT_HALF, tm), lambda b, i: (0, i)),
        pl.BlockSpec((ROT_HALF, tm), lambda b, i: (0, i)),
        pl.BlockSpec((tm, LANES), lambda b, i: (i, 0)),
        pl.BlockSpec((tm, LANES), lambda b, i: (i, 0)),
        pl.BlockSpec((tm, LANES), lambda b, i: (i, 0)),
    ]
    nsub = per // DIL_BAND
    out_shape = [
        jax.ShapeDtypeStruct((B, d, L // DIL_BAND, WIDTH_G, DIL_BAND), BF16),
        jax.ShapeDtypeStruct((B, d, L // DIL_BAND, WIDTH_G, DIL_BAND), BF16),
        jax.ShapeDtypeStruct((B, d, L, WIDTH_G), BF16),
    ]
    out_specs = [
        pl.BlockSpec((None, d, nsub, WIDTH_G, DIL_BAND), lambda b, i: (b, 0, i, 0, 0)),
        pl.BlockSpec((None, d, nsub, WIDTH_G, DIL_BAND), lambda b, i: (b, 0, i, 0, 0)),
        pl.BlockSpec((None, d, per, WIDTH_G), lambda b, i: (b, 0, i, 0)),
    ]
    if na:
        nblk = tm // MOBA_BLOCK
        out_shape = [
            jax.ShapeDtypeStruct((B, S // MOBA_BLOCK, na, MOBA_BLOCK), BF16),
            jax.ShapeDtypeStruct((B, S // MOBA_BLOCK, na, MOBA_BLOCK), BF16),
            jax.ShapeDtypeStruct((B, S, 2 * na), BF16),
            jax.ShapeDtypeStruct((B, S // tm, nblk, na), F32),
        ] + out_shape
        out_specs = [
            pl.BlockSpec((None, nblk, na, MOBA_BLOCK), lambda b, i: (b, i, 0, 0)),
            pl.BlockSpec((None, nblk, na, MOBA_BLOCK), lambda b, i: (b, i, 0, 0)),
            pl.BlockSpec((None, tm, 2 * na), lambda b, i: (b, i, 0)),
            pl.BlockSpec((None, None, nblk, na), lambda b, i: (b, i, 0, 0)),
        ] + out_specs
    return pl.pallas_call(
        functools.partial(_proj_kernel, na=na, d=d),
        grid=grid, in_specs=in_specs, out_specs=out_specs, out_shape=out_shape,
        compiler_params=_cparams(("arbitrary",) * 2),
        name=f"proj_d{d}",
    )(x, g, wqT, wk, wvT, cq, sq, ck, s1, s2)


def _moba_kernel(q_ref, k_ref, v_ref, km_ref, o_ref, qa_scr, s_scr, p_scr):
    g = pl.program_id(2)
    nb = km_ref.shape[0]
    blk = MOBA_BLOCK
    tq = MOBA_QBLOCKS * blk
    q2 = jnp.concatenate([q_ref[i] for i in range(MOBA_QBLOCKS)], axis=1)
    bid = lax.broadcasted_iota(jnp.int32, (nb, tq), 0)
    qblk = MOBA_QBLOCKS * g + lax.broadcasted_iota(jnp.int32, (nb, tq), 1) // blk
    key_i = lax.broadcasted_iota(jnp.int32, (blk, tq), 0)
    qry_i = lax.broadcasted_iota(jnp.int32, (blk, tq), 1)
    km = km_ref[...].astype(BF16)
    zq = jnp.zeros((HEAD_DIM, tq), BF16)
    zb = jnp.zeros((HEAD_DIM - nb, tq), BF16)

    state = []
    for h in range(2):
        hl = slice(LANES * h, LANES * (h + 1))
        vl = slice(HEAD_DIM * h, HEAD_DIM * (h + 1))
        qh = q2[HEAD_DIM * h:HEAD_DIM * (h + 1)]
        q_plain = jnp.concatenate([qh, zq] if h == 0 else [zq, qh], axis=0)
        gt = jnp.where(bid < qblk, _dot(km, q_plain), -jnp.inf)
        sel = jnp.zeros(gt.shape, jnp.bool_)
        for _ in range(MOBA_TOPK):
            mx = jnp.max(gt, axis=0, keepdims=True)
            first = jnp.min(jnp.where((gt == mx) & (mx > -jnp.inf), bid, nb), axis=0, keepdims=True)
            pick = bid == first
            sel = sel | pick
            gt = jnp.where(pick, -jnp.inf, gt)
        bias = jnp.where(sel, 0.0, NEG).astype(BF16)
        qa_scr[h] = jnp.concatenate([qh, bias, zb] if h == 0 else [bias, zb, qh], axis=0)
        bias_d = jnp.where(sel | (bid == qblk), 0.0, NEG).astype(BF16)
        q_diag = jnp.concatenate([qh, bias_d, zb] if h == 0 else [bias_d, zb, qh], axis=0)
        ss = []
        for i in range(MOBA_QBLOCKS):
            j = MOBA_QBLOCKS * g + i
            sd = _dot(k_ref[pl.ds(pl.multiple_of(j * blk, blk), blk), hl], q_diag)
            own = (qry_i >= i * blk) & (qry_i < (i + 1) * blk)
            ss.append(jnp.where(own & (key_i > qry_i - i * blk), NEG, sd))
        m = ss[0].max(axis=0, keepdims=True)
        for sd in ss[1:]:
            m = jnp.maximum(m, sd.max(axis=0, keepdims=True))
        l = jnp.zeros((1, tq), F32)
        acc = jnp.zeros((HEAD_DIM, tq), F32)
        for i, sd in enumerate(ss):
            p = jnp.exp2(sd - m)
            l = l + jnp.sum(p, axis=0, keepdims=True)
            acc = acc + _dot(v_ref[MOBA_QBLOCKS * g + i, vl, :], p.astype(BF16))
        state += [m, l, acc]

    def blocks(c):
        return [jnp.where((c >= 0) & (c < g), MOBA_QBLOCKS * c + i, nb - 1) for i in range(MOBA_QBLOCKS)]

    def scores(c, h):
        hl = slice(LANES * h, LANES * (h + 1))
        qa = qa_scr[h]
        return [_dot(k_ref[pl.ds(pl.multiple_of(j * blk, blk), blk), hl], qa) for j in blocks(c)]

    def values(c, h, alpha, acc):
        vl = slice(HEAD_DIM * h, HEAD_DIM * (h + 1))
        acc = alpha * acc
        for i, j in enumerate(blocks(c)):
            acc = acc + _dot(v_ref[j, vl, :], p_scr[h, i])
        return acc

    for h in range(2):
        for i, sc in enumerate(scores(0, h)):
            s_scr[h, i] = sc
    p_scr[...] = jnp.zeros(p_scr.shape, BF16)

    def body(it, carry):
        alphas, ms, ls, accs = carry
        a_new, m_new, l_new, acc_new = [], [], [], []
        for h in range(2):
            s_next = scores(it + 1, h)
            s_cur = [s_scr[h, i] for i in range(MOBA_QBLOCKS)]
            acc_new.append(values(it - 1, h, alphas[h], accs[h]))
            mn = ms[h]
            for sc in s_cur:
                mn = jnp.maximum(mn, jnp.max(sc, axis=0, keepdims=True))
            alpha = jnp.exp2(ms[h] - mn)
            l = alpha * ls[h]
            for i in range(MOBA_QBLOCKS):
                p = jnp.exp2(s_cur[i] - mn)
                l = l + jnp.sum(p, axis=0, keepdims=True)
                s_scr[h, i] = s_next[i]
                p_scr[h, i] = p.astype(BF16)
            m_new.append(mn)
            l_new.append(l)
            a_new.append(alpha)
        return (tuple(a_new), tuple(m_new), tuple(l_new), tuple(acc_new))

    one = jnp.ones((1, tq), F32)
    init = ((one, one), (state[0], state[3]), (state[1], state[4]), (state[2], state[5]))
    alphas, _, ls, accs = lax.fori_loop(0, g, body, init)
    accs = [values(g - 1, h, alphas[h], accs[h]) for h in range(2)]
    oT = jnp.concatenate([accs[0] / ls[0], accs[1] / ls[1]], axis=0)
    o_ref[...] = oT.T.astype(o_ref.dtype)


def _moba(qT, k, vT, kmean):
    B, nb, wa, blk = qT.shape
    S = nb * blk
    hp = wa // LANES
    nq = MOBA_QBLOCKS
    assert nb % 16 == 0 and nb <= HEAD_DIM, "block-mask rows must fit the spare half of a head pair"
    return pl.pallas_call(
        _moba_kernel,
        grid=(B, hp, nb // nq),
        in_specs=[
            pl.BlockSpec((None, nq, LANES, blk), lambda b, p, n: (b, n, p, 0)),
            pl.BlockSpec((None, S, 2 * LANES), lambda b, p, n: (b, 0, p)),
            pl.BlockSpec((None, nb, LANES, blk), lambda b, p, n: (b, 0, p, 0)),
            pl.BlockSpec((None, nb, LANES), lambda b, p, n: (b, 0, p)),
        ],
        out_specs=pl.BlockSpec((None, nq * blk, LANES), lambda b, p, n: (b, n, p)),
        out_shape=jax.ShapeDtypeStruct((B, S, wa), BF16),
        scratch_shapes=[pltpu.VMEM((2, LANES, nq * blk), BF16), pltpu.VMEM((2, nq, blk, nq * blk), F32),
                        pltpu.VMEM((2, nq, blk, nq * blk), BF16)],
        compiler_params=_cparams(("arbitrary",) * 3),
        name="moba_attn",
    )(qT, k, vT, kmean)


def _dil_kernel(q_ref, k_ref, kp_ref, v_ref, vp_ref, o_ref, lse_ref):
    t = pl.program_id(2)
    band = DIL_BAND
    nsub = q_ref.shape[0]
    key_i = lax.broadcasted_iota(jnp.int32, (band, band), 0)
    qry_i = lax.broadcasted_iota(jnp.int32, (band, band), 1)
    rowid = lax.broadcasted_iota(jnp.int32, (LANES, band), 0)
    own_ok = key_i <= qry_i
    prev_ok = key_i >= qry_i
    for c in range(nsub):
        o_parts, l_parts = [], []
        for hp in range(WIDTH_G // LANES):
            cols = slice(hp * LANES, (hp + 1) * LANES)
            k_own = k_ref[c * band:(c + 1) * band, cols]
            k_prev = kp_ref[:, cols] if c == 0 else k_ref[(c - 1) * band:c * band, cols]
            q2 = q_ref[c, cols, :]
            for h in range(2):
                qh = jnp.where((rowid >= HEAD_DIM * h) & (rowid < HEAD_DIM * (h + 1)), q2, jnp.zeros_like(q2))
                pmask = (prev_ok & (t > 0)) if c == 0 else prev_ok
                s_own = jnp.where(own_ok, _dot(k_own, qh), -jnp.inf)
                s_prev = jnp.where(pmask, _dot(k_prev, qh), -jnp.inf)
                m = jnp.maximum(jnp.max(s_own, axis=0, keepdims=True), jnp.max(s_prev, axis=0, keepdims=True))
                p_own = jnp.exp(s_own - m)
                p_prev = jnp.exp(s_prev - m)
                l = jnp.sum(p_own, axis=0, keepdims=True) + jnp.sum(p_prev, axis=0, keepdims=True)
                rows = slice(hp * LANES + h * HEAD_DIM, hp * LANES + (h + 1) * HEAD_DIM)
                v_own = v_ref[c, rows, :]
                v_prev = vp_ref[rows, :] if c == 0 else v_ref[c - 1, rows, :]
                oT = _dot(v_own, p_own.astype(BF16)) + _dot(v_prev, p_prev.astype(BF16))
                o_parts.append(oT / l)
                l_parts.append(jnp.broadcast_to(m + jnp.log(l), (HEAD_DIM, band)))
        o_ref[c * band:(c + 1) * band, :] = jnp.concatenate(o_parts, axis=0).T
        lse_ref[c * band:(c + 1) * band, :] = jnp.concatenate(l_parts, axis=0).T


def _dilated(qT, k, vT):
    B, d, nblk, wg, band = qT.shape
    L = nblk * band
    tl = min(DIL_TL, L)
    nsub = tl // band
    prev = lambda t: jnp.maximum(t * nsub - 1, 0)
    return pl.pallas_call(
        _dil_kernel,
        grid=(B, d, L // tl),
        in_specs=[
            pl.BlockSpec((None, None, nsub, wg, band), lambda b, r, t: (b, r, t, 0, 0)),
            pl.BlockSpec((None, None, tl, wg), lambda b, r, t: (b, r, t, 0)),
            pl.BlockSpec((None, None, band, wg), lambda b, r, t: (b, r, prev(t), 0)),
            pl.BlockSpec((None, None, nsub, wg, band), lambda b, r, t: (b, r, t, 0, 0)),
            pl.BlockSpec((None, None, None, wg, band), lambda b, r, t: (b, r, prev(t), 0, 0)),
        ],
        out_specs=[
            pl.BlockSpec((None, None, tl, wg), lambda b, r, t: (b, r, t, 0)),
            pl.BlockSpec((None, None, tl, wg), lambda b, r, t: (b, r, t, 0)),
        ],
        out_shape=[jax.ShapeDtypeStruct((B, d, L, wg), F32)] * 2,
        compiler_params=_cparams(("arbitrary",) * 3),
        name=f"dilated_d{d}",
    )(qT, k, k, vT, vT)


def _interleave(ref, scr):
    d, per, w = ref.shape
    if d == 1:
        return ref[0]
    for r in range(d):
        for sl in range(w // LANES):
            scr[sl, pl.ds(r, per, stride=d), :] = ref[r, :, sl * LANES:(sl + 1) * LANES]
    return jnp.concatenate([scr[sl] for sl in range(w // LANES)], axis=1)


def _post_kernel(x_ref, oa_ref, o1_ref, o2_ref, o3_ref, l1_ref, l2_ref, l3_ref, gmix_ref, wg_ref, bga_ref,
                 bgb_ref, woa_ref, wob_ref, wout_ref, gffn_ref, wr_ref, br_ref,
                 x1_ref, h2_ref, idx_ref, rank_ref, pnat_ref, cnt_ref, carry_scr, il_scr):
    i = pl.program_id(0)
    tm, D = x_ref.shape
    ne = wr_ref.shape[0]

    @pl.when(i == 0)
    def _():
        carry_scr[...] = jnp.zeros_like(carry_scr)

    x = x_ref[...]
    h = _rms(x, gmix_ref[...]).astype(BF16)
    gates = _dot(h, wg_ref[...])
    ga = gates[:, :D] + bga_ref[...]
    gb = gates[:, D:] + bgb_ref[...]
    l1, l2, l3 = [_interleave(r, il_scr.at[n]) for n, r in enumerate((l1_ref, l2_ref, l3_ref))]
    o1, o2, o3 = [_interleave(r, il_scr.at[3 + n]) for n, r in enumerate((o1_ref, o2_ref, o3_ref))]
    mx = jnp.maximum(jnp.maximum(l1, l2), l3)
    e1, e2, e3 = jnp.exp(l1 - mx), jnp.exp(l2 - mx), jnp.exp(l3 - mx)
    ob = (e1 * o1 + e2 * o2 + e3 * o3) / (e1 + e2 + e3)
    ya = _dot(oa_ref[...], woa_ref[...])
    yb = _dot(ob.astype(BF16), wob_ref[...])
    mix = jax.nn.sigmoid(ga) * ya + jax.nn.sigmoid(gb) * yb
    x1 = x + _dot(mix.astype(BF16), wout_ref[...])
    x1_ref[...] = x1
    h2 = _rms(x1, gffn_ref[...])
    _rows_to_tiles(h2_ref, h2)

    wr = wr_ref[...]
    wr_hi = wr.astype(BF16)
    wr_lo = (wr - wr_hi.astype(F32)).astype(BF16)
    h2_hi = h2.astype(BF16)
    h2_lo = (h2 - h2_hi.astype(F32)).astype(BF16)
    logits = _dot_nt(wr_hi, h2_hi) + (_dot_nt(wr_hi, h2_lo) + _dot_nt(wr_lo, h2_hi)) + br_ref[...]
    eid = lax.broadcasted_iota(jnp.int32, (ne, tm), 0)
    g = logits
    vals, idxs, picks = [], [], []
    for _ in range(TOP_K):
        m = jnp.max(g, axis=0, keepdims=True)
        first = jnp.min(jnp.where(g == m, eid, ne), axis=0, keepdims=True)
        pick = eid == first
        vals.append(m)
        idxs.append(first)
        picks.append(pick)
        g = jnp.where(pick, -jnp.inf, g)
    es = [jnp.exp(v - vals[0]) for v in vals]
    den = es[0] + es[1] + es[2] + es[3]
    probs = [e / den for e in es]

    onehot = jnp.zeros((ne, tm), F32)
    for pick in picks:
        onehot = onehot + pick.astype(F32)
    earlier = (lax.broadcasted_iota(jnp.int32, (tm, tm), 0) < lax.broadcasted_iota(jnp.int32, (tm, tm), 1))
    prefix = _dot(onehot.astype(BF16), earlier.astype(BF16)) + carry_scr[:, 0:1]
    ranks = [jnp.sum(jnp.where(pick, prefix, 0.0), axis=0, keepdims=True) for pick in picks]
    carry_scr[...] = carry_scr[...] + jnp.sum(onehot, axis=1, keepdims=True)
    cnt_ref[...] = carry_scr[...]

    zi = jnp.zeros((8 - TOP_K, tm), jnp.int32)
    idx_ref[...] = jnp.concatenate(idxs + [zi], axis=0)
    rank_ref[...] = jnp.concatenate([r.astype(jnp.int32) for r in ranks] + [zi], axis=0)
    pnat_ref[...] = jnp.concatenate(probs + [jnp.zeros((LANES - TOP_K, tm), F32)], axis=0).T


def _post(x2, oa, obs, lses, gmix, wg, bga, bgb, woa, wob, wout, gffn, wrT, br):
    T, D = x2.shape
    tm = POST_TM
    ne = wrT.shape[0]
    row = lambda w: pl.BlockSpec((tm, w), lambda i: (i, 0))

    def grouped(a):
        _, d, L, w = a.shape
        nt = L * d // tm
        return pl.BlockSpec((None, d, tm // d, w), lambda i: (i // nt, 0, i % nt, 0))

    full = lambda a: pl.BlockSpec(a.shape, lambda i: (0,) * a.ndim)
    col = pl.BlockSpec((8, tm), lambda i: (0, i))
    return pl.pallas_call(
        _post_kernel,
        grid=(T // tm,),
        in_specs=[row(D), row(WIDTH_A)] + [grouped(a) for a in (*obs, *lses)]
                 + [full(a) for a in (gmix, wg, bga, bgb, woa, wob, wout, gffn, wrT, br)],
        out_specs=[row(D), pl.BlockSpec((tm * SUBLANES, LANES), lambda i: (i, 0)), col, col, row(LANES),
                   pl.BlockSpec((ne, LANES), lambda i: (0, 0))],
        out_shape=[
            jax.ShapeDtypeStruct((T, D), F32), jax.ShapeDtypeStruct((T * SUBLANES, LANES), F32),
            jax.ShapeDtypeStruct((8, T), jnp.int32), jax.ShapeDtypeStruct((8, T), jnp.int32),
            jax.ShapeDtypeStruct((T, LANES), F32),
            jax.ShapeDtypeStruct((ne, LANES), F32),
        ],
        scratch_shapes=[pltpu.VMEM((ne, LANES), F32), pltpu.VMEM((6, WIDTH_G // LANES, tm, LANES), F32)],
        compiler_params=_cparams(("arbitrary",)),
        name="post_mix_router",
    )(x2, oa, *obs, *lses, gmix, wg, bga, bgb, woa, wob, wout, gffn, wrT, br)


def _dispatch_kernel(pend_ref, padded_ref, dest_ref, h_ref, xr_ref, zero_scr, sem):
    i = pl.program_id(0)
    tm = h_ref.shape[0] // SUBLANES
    br = zero_scr.shape[0]

    def zero_copy(blk_start):
        return pltpu.make_async_copy(zero_scr, xr_ref.at[pl.ds(pl.multiple_of(blk_start * SUBLANES, br), br), :], sem)

    @pl.when(i == 0)
    def _():
        zero_scr[...] = jnp.zeros_like(zero_scr)
        n_used = pend_ref[N_EXPERTS - 1] // EXPERT_ROWS
        n_blk = xr_ref.shape[0] // br

        def tail(fn):
            def go(e, c):
                @pl.when(padded_ref[e] > 0)
                def _():
                    fn(zero_copy(pend_ref[e] - EXPERT_ROWS))
                return c
            return go

        def unused(fn):
            def go(b, c):
                fn(zero_copy(b * EXPERT_ROWS))
                return c
            return go

        lax.fori_loop(0, N_EXPERTS, tail(lambda cp: cp.start()), 0)
        lax.fori_loop(n_used, n_blk, unused(lambda cp: cp.start()), 0)
        lax.fori_loop(0, N_EXPERTS, tail(lambda cp: cp.wait()), 0)
        lax.fori_loop(n_used, n_blk, unused(lambda cp: cp.wait()), 0)

    def row_copy(t, k):
        src = h_ref.at[pl.ds(pl.multiple_of(t * SUBLANES, SUBLANES), SUBLANES), :]
        dst = xr_ref.at[pl.ds(pl.multiple_of(dest_ref[k, t] * SUBLANES, SUBLANES), SUBLANES), :]
        return pltpu.make_async_copy(src, dst, sem)

    def start(tb, c):
        for u in range(ROW_UNROLL):
            for k in range(TOP_K):
                row_copy(tb * ROW_UNROLL + u, k).start(priority=(u * TOP_K + k) % 2)
        return c

    def wait(tb, c):
        for u in range(ROW_UNROLL):
            for k in range(TOP_K):
                row_copy(tb * ROW_UNROLL + u, k).wait()
        return c

    lax.fori_loop(0, tm // ROW_UNROLL, start, 0)
    lax.fori_loop(0, tm // ROW_UNROLL, wait, 0)


def _dispatch(pend, padded, dest3, h2t, n_rows):
    tm = DISPATCH_TM
    return pl.pallas_call(
        _dispatch_kernel,
        grid_spec=pltpu.PrefetchScalarGridSpec(
            num_scalar_prefetch=2,
            grid=(h2t.shape[0] // (tm * SUBLANES),),
            in_specs=[
                pl.BlockSpec((None, TOP_K, tm), lambda i, pe, pa: (i, 0, 0), memory_space=pltpu.SMEM),
                pl.BlockSpec((tm * SUBLANES, LANES), lambda i, pe, pa: (i, 0)),
            ],
            out_specs=pl.BlockSpec(memory_space=pl.ANY),
            scratch_shapes=[pltpu.VMEM((EXPERT_ROWS * SUBLANES, LANES), F32), pltpu.SemaphoreType.DMA(())],
        ),
        out_shape=jax.ShapeDtypeStruct((n_rows * SUBLANES, LANES), F32),
        compiler_params=_cparams(("arbitrary",)),
        name="moe_dispatch",
    )(pend, padded, dest3, h2t)


def _expert_kernel(pstart_ref, padded_ref, x_hbm, wg_ref, bg_ref, wu_ref, bu_ref, wd_ref, bd_ref, y_hbm,
                   w_scr, xbuf, ybuf, xsem, ysem):
    e = pl.program_id(0)
    tb = xbuf.shape[1]
    nblk = padded_ref[e] // EXPERT_ROWS
    row0 = pstart_ref[e]

    def rows(j):
        return pl.ds(pl.multiple_of((row0 + j * EXPERT_ROWS) * SUBLANES, tb), tb)

    def x_copy(j, slot):
        return pltpu.make_async_copy(x_hbm.at[rows(j), :], xbuf.at[slot], xsem.at[slot])

    def y_copy(j, slot):
        return pltpu.make_async_copy(ybuf.at[slot], y_hbm.at[rows(j), :], ysem.at[slot])

    @pl.when(nblk > 0)
    def _():
        x_copy(0, 0).start()
        w_scr[0] = wg_ref[...].astype(BF16)
        w_scr[1] = wu_ref[...].astype(BF16)
        w_scr[2] = wd_ref[...].astype(BF16)

        def body(j, c):
            slot = j % 2

            @pl.when(j + 1 < nblk)
            def _():
                x_copy(j + 1, 1 - slot).start()

            x_copy(j, slot).wait()

            @pl.when(j >= 2)
            def _():
                y_copy(j - 2, slot).wait()

            x = _tiles_to_rows(xbuf.at[slot], EXPERT_ROWS).astype(BF16)
            g = _dot(x, w_scr[0]) + bg_ref[...]
            u = _dot(x, w_scr[1]) + bu_ref[...]
            g = jnp.minimum(g, SWIGLU_LIMIT)
            u = jnp.clip(u, -SWIGLU_LIMIT, SWIGLU_LIMIT)
            a = g * jax.nn.sigmoid(SWIGLU_ALPHA * g) * (u + 1.0)
            _rows_to_tiles(ybuf.at[slot], _dot(a.astype(BF16), w_scr[2]) + bd_ref[...])
            y_copy(j, slot).start()
            return c

        lax.fori_loop(0, nblk, body, 0)

        @pl.when(nblk >= 2)
        def _():
            y_copy(nblk - 2, nblk % 2).wait()

        y_copy(nblk - 1, (nblk - 1) % 2).wait()

    @pl.when(e == pl.num_programs(0) - 1)
    def _():
        n_used = (pstart_ref[e] + padded_ref[e]) // EXPERT_ROWS
        n_all = y_hbm.shape[0] // tb
        ybuf[0] = jnp.zeros(ybuf.shape[1:], F32)

        def zero_copy(b):
            return pltpu.make_async_copy(ybuf.at[0], y_hbm.at[pl.ds(pl.multiple_of(b * tb, tb), tb), :], ysem.at[0])

        lax.fori_loop(n_used, n_all, lambda b, c: (zero_copy(b).start(), c)[1], 0)
        lax.fori_loop(n_used, n_all, lambda b, c: (zero_copy(b).wait(), c)[1], 0)


def _experts(pstart, padded, x_rows, w_gate, b_gate, w_up, b_up, w_down, b_down):
    E, D, F = w_gate.shape
    tb = EXPERT_ROWS * SUBLANES
    wspec = lambda shape: pl.BlockSpec((None,) + shape, lambda e, ps, pa: (e, 0, 0))
    return pl.pallas_call(
        _expert_kernel,
        grid_spec=pltpu.PrefetchScalarGridSpec(
            num_scalar_prefetch=2,
            grid=(E,),
            in_specs=[
                pl.BlockSpec(memory_space=pl.ANY),
                wspec((D, F)), wspec((1, F)), wspec((D, F)), wspec((1, F)), wspec((F, D)), wspec((1, D)),
            ],
            out_specs=pl.BlockSpec(memory_space=pl.ANY),
            scratch_shapes=[pltpu.VMEM((3, D, F), BF16), pltpu.VMEM((2, tb, LANES), F32),
                            pltpu.VMEM((2, tb, LANES), F32), pltpu.SemaphoreType.DMA((2,)),
                            pltpu.SemaphoreType.DMA((2,))],
        ),
        out_shape=jax.ShapeDtypeStruct(x_rows.shape, F32),
        compiler_params=_cparams(("arbitrary",)),
        name="moe_experts",
    )(pstart, padded, x_rows, w_gate, b_gate.reshape(E, 1, F), w_up, b_up.reshape(E, 1, F),
      w_down, b_down.reshape(E, 1, D))


def _combine_kernel(dest_ref, dnext_ref, y_ref, x1_ref, p_ref, g_ref, o_ref, ybuf, sem):
    i = pl.program_id(0)
    tm = x1_ref.shape[0]
    slot = i % 2

    def row_copy(d_ref, sl, t, k):
        src = y_ref.at[pl.ds(pl.multiple_of(d_ref[k, t] * SUBLANES, SUBLANES), SUBLANES), :]
        dst = ybuf.at[sl, k, pl.ds(pl.multiple_of(t * SUBLANES, SUBLANES), SUBLANES), :]
        return pltpu.make_async_copy(src, dst, sem.at[sl])

    def request(d_ref, sl):
        def go(tb, c):
            for u in range(ROW_UNROLL):
                for k in range(TOP_K):
                    row_copy(d_ref, sl, tb * ROW_UNROLL + u, k).start(priority=(u * TOP_K + k) % 2)
            return c
        lax.fori_loop(0, tm // ROW_UNROLL, go, 0)

    @pl.when(i == 0)
    def _():
        request(dest_ref, 0)

    @pl.when(i + 1 < pl.num_programs(0))
    def _():
        request(dnext_ref, 1 - slot)

    def wait(tb, c):
        for u in range(ROW_UNROLL):
            for k in range(TOP_K):
                row_copy(dest_ref, slot, tb * ROW_UNROLL + u, k).wait()
        return c

    lax.fori_loop(0, tm // ROW_UNROLL, wait, 0)
    p = p_ref[...]
    y = x1_ref[...]
    for k in range(TOP_K):
        y = y + p[:, k:k + 1] * _tiles_to_rows(ybuf.at[slot, k], tm)
    o_ref[...] = _rms(y, g_ref[...])


def _combine(dest3, y_rows, x1, pnat, g_final):
    T, D = x1.shape
    tm = COMBINE_TM
    nt = T // tm
    return pl.pallas_call(
        _combine_kernel,
        grid=(nt,),
        in_specs=[
            pl.BlockSpec((None, TOP_K, tm), lambda i: (i, 0, 0), memory_space=pltpu.SMEM),
            pl.BlockSpec((None, TOP_K, tm), lambda i: (jnp.minimum(i + 1, nt - 1), 0, 0), memory_space=pltpu.SMEM),
            pl.BlockSpec(memory_space=pl.ANY),
            pl.BlockSpec((tm, D), lambda i: (i, 0)),
            pl.BlockSpec((tm, LANES), lambda i: (i, 0)),
            pl.BlockSpec((1, D), lambda i: (0, 0)),
        ],
        out_specs=pl.BlockSpec((tm, D), lambda i: (i, 0)),
        out_shape=jax.ShapeDtypeStruct((T, D), F32),
        scratch_shapes=[pltpu.VMEM((2, TOP_K, tm * SUBLANES, LANES), F32), pltpu.SemaphoreType.DMA((2,))],
        compiler_params=_cparams(("arbitrary",)),
        name="moe_combine",
    )(dest3, dest3, y_rows, x1, pnat, g_final)


def _rope_tables(S, d, tm):
    per = tm // d

    def cos_sin(shape, ax, f):
        rho = lax.broadcasted_iota(jnp.int32, shape, ax)
        w = rho % tm
        pos = ((rho // tm) * per + w % per) * d + w // per
        inv = jnp.float32(ROPE_THETA) ** (-f.astype(F32) / ROT_HALF)
        ang = pos.astype(F32) * inv
        return jnp.cos(ang), jnp.sin(ang)

    sh = (ROT_HALF, S)
    cq, sq = cos_sin(sh, 1, lax.broadcasted_iota(jnp.int32, sh, 0))
    sh = (S, ROT_HALF)
    cos, sin = cos_sin(sh, 0, lax.broadcasted_iota(jnp.int32, sh, 1))
    lane = lax.broadcasted_iota(jnp.int32, (ROT_HALF, LANES), 1) % HEAD_DIM
    f = lax.broadcasted_iota(jnp.int32, (ROT_HALF, LANES), 0)
    lo = ((lane == f)).astype(F32)
    hi = ((lane == f + ROT_HALF)).astype(F32)
    spread = lambda a, m: jnp.dot(a, m, precision=lax.Precision.HIGHEST)
    rest = (lax.broadcasted_iota(jnp.int32, (1, LANES), 1) % HEAD_DIM >= 2 * ROT_HALF).astype(F32)
    ck = spread(cos, lo + hi) + rest
    s1 = spread(-sin, lo)
    s2 = spread(sin, hi)
    return cq, sq, ck, s1, s2


def _layer(x, ln_mix_g, w_in, b_gate_a, b_gate_b, w_o_a, w_o_b, w_out, ln_ffn_g, w_router, b_router,
           w_gate, b_gate, w_up, b_up, w_down, b_down, ln_out_g):
    B, S, D = x.shape
    T = B * S
    qa0, ka0, va0 = 0, WIDTH_A, 2 * WIDTH_A
    qb0, kb0, vb0 = 3 * WIDTH_A, 3 * WIDTH_A + WIDTH_B, 3 * WIDTH_A + 2 * WIDTH_B
    g0 = 3 * WIDTH_A + 3 * WIDTH_B
    cols = lambda s, w: w_in[:, s:s + w]
    gmix = ln_mix_g.reshape(1, D)

    obs, lses = [], []
    o_a = None
    for gi, (_, d) in enumerate(DIL_PAIRS):
        off = gi * WIDTH_G
        wq, wk, wv = cols(qb0 + off, WIDTH_G), cols(kb0 + off, WIDTH_G), cols(vb0 + off, WIDTH_G)
        na = 0
        if gi == 0:
            na = WIDTH_A
            wq = jnp.concatenate([cols(qa0, WIDTH_A), wq], axis=1)
            wk = jnp.concatenate([cols(ka0, WIDTH_A), wk], axis=1)
            wv = jnp.concatenate([cols(va0, WIDTH_A), wv], axis=1)
        outs = _project(x, gmix, wq.T.astype(BF16), wk.astype(BF16), wv.T.astype(BF16), d, na)
        if gi == 0:
            qTa, vTa, ka, kmean = outs[:4]
            outs = outs[4:]
            o_a = _moba(qTa, ka, vTa, kmean.reshape(B, S // MOBA_BLOCK, WIDTH_A))
        qTb, vTb, kb = outs
        o_g, lse_g = _dilated(qTb, kb, vTb)
        obs.append(o_g)
        lses.append(lse_g)

    x1, h2, idxT, rankT, pnat, cnt = _post(
        x.reshape(T, D), o_a.reshape(T, WIDTH_A), obs, lses, gmix,
        cols(g0, 2 * D).astype(BF16), b_gate_a.reshape(1, D), b_gate_b.reshape(1, D),
        w_o_a.astype(BF16), w_o_b.astype(BF16), w_out.astype(BF16), ln_ffn_g.reshape(1, D),
        w_router.T, b_router.reshape(N_EXPERTS, 1))

    br = EXPERT_ROWS
    counts = cnt[:, 0].astype(jnp.int32)
    padded = (counts + br - 1) // br * br
    pend = jnp.cumsum(padded)
    pstart = pend - padded
    eids = jnp.arange(N_EXPERTS, dtype=jnp.int32)[:, None, None]
    dest = jnp.sum(jnp.where(idxT[None, :TOP_K] == eids, pstart[:, None, None], 0), axis=0) + rankT[:TOP_K]
    n_rows = T * TOP_K + N_EXPERTS * br

    dest_d = dest.reshape(TOP_K, T // DISPATCH_TM, DISPATCH_TM).transpose(1, 0, 2)
    x_rows = _dispatch(pend.astype(jnp.int32), padded.astype(jnp.int32), dest_d, h2, n_rows)
    y_rows = _experts(pstart.astype(jnp.int32), padded.astype(jnp.int32), x_rows, w_gate, b_gate, w_up, b_up,
                      w_down, b_down)
    dest_c = dest.reshape(TOP_K, T // COMBINE_TM, COMBINE_TM).transpose(1, 0, 2)
    out = _combine(dest_c, y_rows, x1, pnat, ln_out_g.reshape(1, D))
    return out.reshape(B, S, D)


def kernel(x, ln_mix_g, w_in, b_gate_a, b_gate_b, w_o_a, w_o_b, w_out, ln_ffn_g, w_router, b_router,
           w_gate, b_gate, w_up, b_up, w_down, b_down, ln_final_g):
    depth = ln_mix_g.shape[0]
    assert depth == 1, "the final RMSNorm is fused into the last layer's combine"
    return _layer(x, ln_mix_g[0], w_in[0], b_gate_a[0], b_gate_b[0], w_o_a[0], w_o_b[0], w_out[0],
                  ln_ffn_g[0], w_router[0], b_router[0], w_gate[0], b_gate[0], w_up[0], b_up[0],
                  w_down[0], b_down[0], ln_final_g)
```

```python
import functools

import jax
import jax.numpy as jnp
from jax import lax
from jax.experimental import pallas as pl
from jax.experimental.pallas import tpu as pltpu

D_MODEL = 1024
HEAD_DIM = 64
ROT_HALF = HEAD_DIM // 8
ROPE_THETA = 500000.0
N_HEADS_A = 8
MOBA_BLOCK = 256
MOBA_TOPK = 3
MOBA_QBLOCKS = 2
DIL_PAIRS = ((128, 1), (512, 4), (2048, 16))
DIL_BAND = 128
HEADS_PER_GROUP_B = 4
WIDTH_A = N_HEADS_A * HEAD_DIM
WIDTH_G = HEADS_PER_GROUP_B * HEAD_DIM
WIDTH_B = WIDTH_G * len(DIL_PAIRS)
N_EXPERTS = 32
TOP_K = 4
SWIGLU_LIMIT = 7.0
SWIGLU_ALPHA = 1.702
NORM_EPS = 1e-5
SCALE = HEAD_DIM ** -0.5
LOG2E = 1.4426950408889634

LANES = 128
PROJ_TM = 512
PERM_TILE = 512
DIL_TL = 512
POST_TM = 512
EXPERT_ROWS = 256
EXPERT_PAIR = 2
DISPATCH_TM = 512
COMBINE_TM = 256
SUBLANES = 8
ROW_UNROLL = 8
VMEM_LIMIT = 56 * 1024 * 1024
NEG = -1e30

BF16 = jnp.bfloat16
F32 = jnp.float32


def _dot(a, b):
    return jnp.dot(a, b, preferred_element_type=F32)


def _dot_nt(a, b):
    return lax.dot_general(a, b, (((1,), (1,)), ((), ())), preferred_element_type=F32)


def _rms(x, g):
    ms = jnp.mean(x * x, axis=-1, keepdims=True)
    return x * lax.rsqrt(ms + NORM_EPS) * g


def _rows_to_tiles(ref, val):
    n = val.shape[0]
    for c in range(val.shape[1] // LANES):
        ref[pl.ds(c, n, stride=SUBLANES), :] = val[:, c * LANES:(c + 1) * LANES]


def _tiles_to_rows(ref, n):
    return jnp.concatenate([ref[pl.ds(c, n, stride=SUBLANES), :] for c in range(SUBLANES)], axis=1)


def _cparams(sem):
    return pltpu.CompilerParams(dimension_semantics=sem, vmem_limit_bytes=VMEM_LIMIT)


def _deinterleave(h, d):
    tm = h.shape[0]
    sub = min(tm, PERM_TILE)
    cs = sub // d
    ri = lax.broadcasted_iota(jnp.int32, (sub, sub), 0)
    ui = lax.broadcasted_iota(jnp.int32, (sub, sub), 1)
    perm = (ui == (ri % cs) * d + ri // cs).astype(BF16)
    slabs = [_dot(perm, h[s0:s0 + sub]).astype(BF16) for s0 in range(0, tm, sub)]
    return jnp.concatenate([sl[r * cs:(r + 1) * cs] for r in range(d) for sl in slabs], axis=0)


def _proj_kernel(x_ref, g_ref, wq_ref, wk_ref, wv_ref, cq_ref, sq_ref, ck_ref, s1_ref, s2_ref, *outs, na, d):
    tm = x_ref.shape[0]
    h = _rms(x_ref[...], g_ref[...]).astype(BF16)
    if d > 1:
        h = _deinterleave(h, d)
    nq = wq_ref.shape[0]
    heads = nq // HEAD_DIM

    qT = _dot_nt(wq_ref[...], h)
    q3 = qT.reshape(heads, HEAD_DIM, tm)
    c = cq_ref[...][None]
    s = sq_ref[...][None]
    x1 = q3[:, 0:ROT_HALF]
    x2 = q3[:, ROT_HALF:2 * ROT_HALF]
    q3 = jnp.concatenate([x1 * c - x2 * s, x2 * c + x1 * s, q3[:, 2 * ROT_HALF:]], axis=1)
    hid = lax.broadcasted_iota(jnp.int32, (heads, 1, 1), 0)
    q3 = q3 * jnp.where(hid < na // HEAD_DIM, SCALE * LOG2E, SCALE)
    qT = q3.reshape(nq, tm).astype(BF16)

    vT = _dot_nt(wv_ref[...], h).astype(BF16)

    kk = _dot(h, wk_ref[...])
    ck, s1, s2 = ck_ref[...], s1_ref[...], s2_ref[...]
    kparts = []
    for gi in range(kk.shape[1] // LANES):
        kg = kk[:, gi * LANES:(gi + 1) * LANES]
        kparts.append(kg * ck + pltpu.roll(kg, LANES - ROT_HALF, 1) * s1 + pltpu.roll(kg, ROT_HALF, 1) * s2)

    if na:
        qa_ref, va_ref, ka_ref, km_ref, qb_ref, vb_ref, kb_ref = outs
        for blk in range(tm // MOBA_BLOCK):
            sl = slice(blk * MOBA_BLOCK, (blk + 1) * MOBA_BLOCK)
            qa_ref[blk] = qT[0:na, sl]
            va_ref[blk] = vT[0:na, sl]
        lane = lax.broadcasted_iota(jnp.int32, (tm, LANES), 1)
        row = lax.broadcasted_iota(jnp.int32, (tm, LANES), 0)
        blkid = pl.program_id(1) * (tm // MOBA_BLOCK) + row // MOBA_BLOCK
        for gi in range(na // LANES):
            kg = kparts[gi]
            for e in range(2):
                in_head = (lane >= HEAD_DIM * e) & (lane < HEAD_DIM * (e + 1))
                onehot = (lane - HEAD_DIM * (1 - e)) == blkid
                col = (2 * gi + e) * LANES
                ka_ref[:, col:col + LANES] = jnp.where(in_head, kg, onehot.astype(F32)).astype(BF16)
            km = kg.reshape(tm // MOBA_BLOCK, MOBA_BLOCK, LANES).sum(axis=1) * (1.0 / MOBA_BLOCK)
            km_ref[:, gi * LANES:(gi + 1) * LANES] = km
    else:
        qb_ref, vb_ref, kb_ref = outs
    per = tm // d
    for r in range(d):
        for cb in range(per // DIL_BAND):
            sl = slice(r * per + cb * DIL_BAND, r * per + (cb + 1) * DIL_BAND)
            qb_ref[r, cb] = qT[na:na + WIDTH_G, sl]
            vb_ref[r, cb] = vT[na:na + WIDTH_G, sl]
        for gi in range(WIDTH_G // LANES):
            kb_ref[r, :, gi * LANES:(gi + 1) * LANES] = kparts[na // LANES + gi][r * per:(r + 1) * per].astype(BF16)


def _project(x, g, wqT, wk, wvT, d, na):
    B, S, D = x.shape
    L = S // d
    tm = PROJ_TM * max(1, d * DIL_BAND // PROJ_TM)
    per = tm // d
    cq, sq, ck, s1, s2 = _rope_tables(S, d, tm)
    grid = (B, S // tm)
    full = lambda a: pl.BlockSpec(a.shape, lambda b, i: (0,) * a.ndim)
    in_specs = [
        pl.BlockSpec((None, tm, D), lambda b, i: (b, i, 0)),
        full(g), full(wqT), full(wk), full(wvT),
        pl.BlockSpec((ROT_HALF, tm), lambda b, i: (0, i)),
        pl.BlockSpec((ROT_HALF, tm), lambda b, i: (0, i)),
        pl.BlockSpec((tm, LANES), lambda b, i: (i, 0)),
        pl.BlockSpec((tm, LANES), lambda b, i: (i, 0)),
        pl.BlockSpec((tm, LANES), lambda b, i: (i, 0)),
    ]
    nsub = per // DIL_BAND
    out_shape = [
        jax.ShapeDtypeStruct((B, d, L // DIL_BAND, WIDTH_G, DIL_BAND), BF16),
        jax.ShapeDtypeStruct((B, d, L // DIL_BAND, WIDTH_G, DIL_BAND), BF16),
        jax.ShapeDtypeStruct((B, d, L, WIDTH_G), BF16),
    ]
    out_specs = [
        pl.BlockSpec((None, d, nsub, WIDTH_G, DIL_BAND), lambda b, i: (b, 0, i, 0, 0)),
        pl.BlockSpec((None, d, nsub, WIDTH_G, DIL_BAND), lambda b, i: (b, 0, i, 0, 0)),
        pl.BlockSpec((None, d, per, WIDTH_G), lambda b, i: (b, 0, i, 0)),
    ]
    if na:
        nblk = tm // MOBA_BLOCK
        out_shape = [
            jax.ShapeDtypeStruct((B, S // MOBA_BLOCK, na, MOBA_BLOCK), BF16),
            jax.ShapeDtypeStruct((B, S // MOBA_BLOCK, na, MOBA_BLOCK), BF16),
            jax.ShapeDtypeStruct((B, S, 2 * na), BF16),
            jax.ShapeDtypeStruct((B, S // tm, nblk, na), F32),
        ] + out_shape
        out_specs = [
            pl.BlockSpec((None, nblk, na, MOBA_BLOCK), lambda b, i: (b, i, 0, 0)),
            pl.BlockSpec((None, nblk, na, MOBA_BLOCK), lambda b, i: (b, i, 0, 0)),
            pl.BlockSpec((None, tm, 2 * na), lambda b, i: (b, i, 0)),
            pl.BlockSpec((None, None, nblk, na), lambda b, i: (b, i, 0, 0)),
        ] + out_specs
    return pl.pallas_call(
        functools.partial(_proj_kernel, na=na, d=d),
        grid=grid, in_specs=in_specs, out_specs=out_specs, out_shape=out_shape,
        compiler_params=_cparams(("arbitrary",) * 2),
        name=f"proj_d{d}",
    )(x, g, wqT, wk, wvT, cq, sq, ck, s1, s2)


def _moba_kernel(q_ref, k_ref, v_ref, km_ref, o_ref, qa_scr, s_scr, p_scr):
    g = pl.program_id(2)
    nb = km_ref.shape[0]
    blk = MOBA_BLOCK
    tq = MOBA_QBLOCKS * blk
    q2 = jnp.concatenate([q_ref[i] for i in range(MOBA_QBLOCKS)], axis=1)
    bid = lax.broadcasted_iota(jnp.int32, (nb, tq), 0)
    qblk = MOBA_QBLOCKS * g + lax.broadcasted_iota(jnp.int32, (nb, tq), 1) // blk
    key_i = lax.broadcasted_iota(jnp.int32, (blk, tq), 0)
    qry_i = lax.broadcasted_iota(jnp.int32, (blk, tq), 1)
    km = km_ref[...].astype(BF16)
    zq = jnp.zeros((HEAD_DIM, tq), BF16)
    zb = jnp.zeros((HEAD_DIM - nb, tq), BF16)

    for h in range(2):
        hl = slice(LANES * h, LANES * (h + 1))
        qh = q2[HEAD_DIM * h:HEAD_DIM * (h + 1)]
        q_plain = jnp.concatenate([qh, zq] if h == 0 else [zq, qh], axis=0)
        gt = jnp.where(bid < qblk, _dot(km, q_plain), -jnp.inf)
        sel = jnp.zeros(gt.shape, jnp.bool_)
        for _ in range(MOBA_TOPK):
            mx = jnp.max(gt, axis=0, keepdims=True)
            first = jnp.min(jnp.where((gt == mx) & (mx > -jnp.inf), bid, nb), axis=0, keepdims=True)
            pick = bid == first
            sel = sel | pick
            gt = jnp.where(pick, -jnp.inf, gt)
        bias = jnp.where(sel, 0.0, NEG).astype(BF16)
        qa_scr[h] = jnp.concatenate([qh, bias, zb] if h == 0 else [bias, zb, qh], axis=0)
        bias_d = jnp.where(sel | (bid == qblk), 0.0, NEG).astype(BF16)
        q_diag = jnp.concatenate([qh, bias_d, zb] if h == 0 else [bias_d, zb, qh], axis=0)
        for i in range(MOBA_QBLOCKS):
            j = MOBA_QBLOCKS * g + i
            sd = _dot(k_ref[pl.ds(pl.multiple_of(j * blk, blk), blk), hl], q_diag)
            own = (qry_i >= i * blk) & (qry_i < (i + 1) * blk)
            s_scr[h, i] = jnp.where(own & (key_i > qry_i - i * blk), NEG, sd)

    def score_blocks(c):
        return [jnp.where(c < g, MOBA_QBLOCKS * c + i, nb - 1) for i in range(MOBA_QBLOCKS)]

    def value_blocks(it):
        return [jnp.where(it <= 1, MOBA_QBLOCKS * g + i, MOBA_QBLOCKS * (it - 2) + i) for i in range(MOBA_QBLOCKS)]

    def values(it, h, alpha, acc):
        vl = slice(HEAD_DIM * h, HEAD_DIM * (h + 1))
        acc = alpha * acc
        for i, j in enumerate(value_blocks(it)):
            acc = acc + _dot(v_ref[j, vl, :], p_scr[h, i])
        return acc

    p_scr[...] = jnp.zeros(p_scr.shape, BF16)

    def body(it, carry):
        alphas, ms, ls, accs = carry
        a_new, m_new, l_new, acc_new = [], [], [], []
        for h in range(2):
            hl = slice(LANES * h, LANES * (h + 1))
            qa = qa_scr[h]
            s_next = [_dot(k_ref[pl.ds(pl.multiple_of(j * blk, blk), blk), hl], qa) for j in score_blocks(it)]
            s_cur = [s_scr[h, i] for i in range(MOBA_QBLOCKS)]
            acc_new.append(values(it, h, alphas[h], accs[h]))
            mn = ms[h]
            for sc in s_cur:
                mn = jnp.maximum(mn, jnp.max(sc, axis=0, keepdims=True))
            alpha = jnp.exp2(ms[h] - mn)
            l = alpha * ls[h]
            for i in range(MOBA_QBLOCKS):
                p = jnp.exp2(s_cur[i] - mn)
                l = l + jnp.sum(p, axis=0, keepdims=True)
                s_scr[h, i] = s_next[i]
                p_scr[h, i] = p.astype(BF16)
            m_new.append(mn)
            l_new.append(l)
            a_new.append(alpha)
        return (tuple(a_new), tuple(m_new), tuple(l_new), tuple(acc_new))

    one = jnp.ones((1, tq), F32)
    neg = jnp.full((1, tq), NEG, F32)
    zl = jnp.zeros((1, tq), F32)
    za = jnp.zeros((HEAD_DIM, tq), F32)
    alphas, _, ls, accs = lax.fori_loop(0, g + 1, body, ((one, one), (neg, neg), (zl, zl), (za, za)))
    accs = [values(g + 1, h, alphas[h], accs[h]) for h in range(2)]
    oT = jnp.concatenate([accs[0] / ls[0], accs[1] / ls[1]], axis=0)
    o_ref[...] = oT.T.astype(o_ref.dtype)


def _moba(qT, k, vT, kmean):
    B, nb, wa, blk = qT.shape
    S = nb * blk
    hp = wa // LANES
    nq = MOBA_QBLOCKS
    assert nb % 16 == 0 and nb <= HEAD_DIM, "block-mask rows must fit the spare half of a head pair"
    return pl.pallas_call(
        _moba_kernel,
        grid=(B, hp, nb // nq),
        in_specs=[
            pl.BlockSpec((None, nq, LANES, blk), lambda b, p, n: (b, n, p, 0)),
            pl.BlockSpec((None, S, 2 * LANES), lambda b, p, n: (b, 0, p)),
            pl.BlockSpec((None, nb, LANES, blk), lambda b, p, n: (b, 0, p, 0)),
            pl.BlockSpec((None, nb, LANES), lambda b, p, n: (b, 0, p)),
        ],
        out_specs=pl.BlockSpec((None, nq * blk, LANES), lambda b, p, n: (b, n, p)),
        out_shape=jax.ShapeDtypeStruct((B, S, wa), BF16),
        scratch_shapes=[pltpu.VMEM((2, LANES, nq * blk), BF16), pltpu.VMEM((2, nq, blk, nq * blk), F32),
                        pltpu.VMEM((2, nq, blk, nq * blk), BF16)],
        compiler_params=_cparams(("arbitrary",) * 3),
        name="moba_attn",
    )(qT, k, vT, kmean)


def _dil_kernel(q_ref, k_ref, kp_ref, v_ref, vp_ref, o_ref, lse_ref):
    t = pl.program_id(2)
    band = DIL_BAND
    nsub = q_ref.shape[0]
    key_i = lax.broadcasted_iota(jnp.int32, (band, band), 0)
    qry_i = lax.broadcasted_iota(jnp.int32, (band, band), 1)
    rowid = lax.broadcasted_iota(jnp.int32, (LANES, band), 0)
    own_ok = key_i <= qry_i
    prev_ok = key_i >= qry_i
    for c in range(nsub):
        o_parts, l_parts = [], []
        for hp in range(WIDTH_G // LANES):
            cols = slice(hp * LANES, (hp + 1) * LANES)
            k_own = k_ref[c * band:(c + 1) * band, cols]
            k_prev = kp_ref[:, cols] if c == 0 else k_ref[(c - 1) * band:c * band, cols]
            q2 = q_ref[c, cols, :]
            for h in range(2):
                qh = jnp.where((rowid >= HEAD_DIM * h) & (rowid < HEAD_DIM * (h + 1)), q2, jnp.zeros_like(q2))
                pmask = (prev_ok & (t > 0)) if c == 0 else prev_ok
                s_own = jnp.where(own_ok, _dot(k_own, qh), -jnp.inf)
                s_prev = jnp.where(pmask, _dot(k_prev, qh), -jnp.inf)
                m = jnp.maximum(jnp.max(s_own, axis=0, keepdims=True), jnp.max(s_prev, axis=0, keepdims=True))
                p_own = jnp.exp(s_own - m)
                p_prev = jnp.exp(s_prev - m)
                l = jnp.sum(p_own, axis=0, keepdims=True) + jnp.sum(p_prev, axis=0, keepdims=True)
                rows = slice(hp * LANES + h * HEAD_DIM, hp * LANES + (h + 1) * HEAD_DIM)
                v_own = v_ref[c, rows, :]
                v_prev = vp_ref[rows, :] if c == 0 else v_ref[c - 1, rows, :]
                oT = _dot(v_own, p_own.astype(BF16)) + _dot(v_prev, p_prev.astype(BF16))
                o_parts.append(oT / l)
                l_parts.append(jnp.broadcast_to(m + jnp.log(l), (HEAD_DIM, band)))
        o_ref[c * band:(c + 1) * band, :] = jnp.concatenate(o_parts, axis=0).T
        lse_ref[c * band:(c + 1) * band, :] = jnp.concatenate(l_parts, axis=0).T


def _dilated(qT, k, vT):
    B, d, nblk, wg, band = qT.shape
    L = nblk * band
    tl = min(DIL_TL, L)
    nsub = tl // band
    prev = lambda t: jnp.maximum(t * nsub - 1, 0)
    return pl.pallas_call(
        _dil_kernel,
        grid=(B, d, L // tl),
        in_specs=[
            pl.BlockSpec((None, None, nsub, wg, band), lambda b, r, t: (b, r, t, 0, 0)),
            pl.BlockSpec((None, None, tl, wg), lambda b, r, t: (b, r, t, 0)),
            pl.BlockSpec((None, None, band, wg), lambda b, r, t: (b, r, prev(t), 0)),
            pl.BlockSpec((None, None, nsub, wg, band), lambda b, r, t: (b, r, t, 0, 0)),
            pl.BlockSpec((None, None, None, wg, band), lambda b, r, t: (b, r, prev(t), 0, 0)),
        ],
        out_specs=[
            pl.BlockSpec((None, None, tl, wg), lambda b, r, t: (b, r, t, 0)),
            pl.BlockSpec((None, None, tl, wg), lambda b, r, t: (b, r, t, 0)),
        ],
        out_shape=[jax.ShapeDtypeStruct((B, d, L, wg), F32)] * 2,
        compiler_params=_cparams(("arbitrary",) * 3),
        name=f"dilated_d{d}",
    )(qT, k, k, vT, vT)


def _interleave(ref, scr):
    d, per, w = ref.shape
    if d == 1:
        return ref[0]
    for r in range(d):
        for sl in range(w // LANES):
            scr[sl, pl.ds(r, per, stride=d), :] = ref[r, :, sl * LANES:(sl + 1) * LANES]
    return jnp.concatenate([scr[sl] for sl in range(w // LANES)], axis=1)


def _post_kernel(x_ref, oa_ref, o1_ref, o2_ref, o3_ref, l1_ref, l2_ref, l3_ref, gmix_ref, wg_ref, bga_ref,
                 bgb_ref, woa_ref, wob_ref, wout_ref, gffn_ref, wr_ref, br_ref,
                 x1_ref, h2_ref, idx_ref, rank_ref, pnat_ref, cnt_ref, carry_scr, il_scr):
    i = pl.program_id(0)
    tm, D = x_ref.shape
    ne = wr_ref.shape[0]

    @pl.when(i == 0)
    def _():
        carry_scr[...] = jnp.zeros_like(carry_scr)

    x = x_ref[...]
    h = _rms(x, gmix_ref[...]).astype(BF16)
    gates = _dot(h, wg_ref[...])
    ga = gates[:, :D] + bga_ref[...]
    gb = gates[:, D:] + bgb_ref[...]
    l1, l2, l3 = [_interleave(r, il_scr.at[n]) for n, r in enumerate((l1_ref, l2_ref, l3_ref))]
    o1, o2, o3 = [_interleave(r, il_scr.at[3 + n]) for n, r in enumerate((o1_ref, o2_ref, o3_ref))]
    mx = jnp.maximum(jnp.maximum(l1, l2), l3)
    e1, e2, e3 = jnp.exp(l1 - mx), jnp.exp(l2 - mx), jnp.exp(l3 - mx)
    ob = (e1 * o1 + e2 * o2 + e3 * o3) / (e1 + e2 + e3)
    ya = _dot(oa_ref[...], woa_ref[...])
    yb = _dot(ob.astype(BF16), wob_ref[...])
    mix = jax.nn.sigmoid(ga) * ya + jax.nn.sigmoid(gb) * yb
    x1 = x + _dot(mix.astype(BF16), wout_ref[...])
    x1_ref[...] = x1
    h2 = _rms(x1, gffn_ref[...])
    _rows_to_tiles(h2_ref, h2)

    wr = wr_ref[...]
    wr_hi = wr.astype(BF16)
    wr_lo = (wr - wr_hi.astype(F32)).astype(BF16)
    h2_hi = h2.astype(BF16)
    h2_lo = (h2 - h2_hi.astype(F32)).astype(BF16)
    logits = _dot_nt(wr_hi, h2_hi) + (_dot_nt(wr_hi, h2_lo) + _dot_nt(wr_lo, h2_hi)) + br_ref[...]
    eid = lax.broadcasted_iota(jnp.int32, (ne, tm), 0)
    g = logits
    vals, idxs, picks = [], [], []
    for _ in range(TOP_K):
        m = jnp.max(g, axis=0, keepdims=True)
        first = jnp.min(jnp.where(g == m, eid, ne), axis=0, keepdims=True)
        pick = eid == first
        vals.append(m)
        idxs.append(first)
        picks.append(pick)
        g = jnp.where(pick, -jnp.inf, g)
    es = [jnp.exp(v - vals[0]) for v in vals]
    den = es[0] + es[1] + es[2] + es[3]
    probs = [e / den for e in es]

    onehot = jnp.zeros((ne, tm), F32)
    for pick in picks:
        onehot = onehot + pick.astype(F32)
    earlier = (lax.broadcasted_iota(jnp.int32, (tm, tm), 0) < lax.broadcasted_iota(jnp.int32, (tm, tm), 1))
    prefix = _dot(onehot.astype(BF16), earlier.astype(BF16)) + carry_scr[:, 0:1]
    ranks = [jnp.sum(jnp.where(pick, prefix, 0.0), axis=0, keepdims=True) for pick in picks]
    carry_scr[...] = carry_scr[...] + jnp.sum(onehot, axis=1, keepdims=True)
    cnt_ref[...] = carry_scr[...]

    zi = jnp.zeros((8 - TOP_K, tm), jnp.int32)
    idx_ref[...] = jnp.concatenate(idxs + [zi], axis=0)
    rank_ref[...] = jnp.concatenate([r.astype(jnp.int32) for r in ranks] + [zi], axis=0)
    pnat_ref[...] = jnp.concatenate(probs + [jnp.zeros((LANES - TOP_K, tm), F32)], axis=0).T


def _post(x2, oa, obs, lses, gmix, wg, bga, bgb, woa, wob, wout, gffn, wrT, br):
    T, D = x2.shape
    tm = POST_TM
    ne = wrT.shape[0]
    row = lambda w: pl.BlockSpec((tm, w), lambda i: (i, 0))

    def grouped(a):
        _, d, L, w = a.shape
        nt = L * d // tm
        return pl.BlockSpec((None, d, tm // d, w), lambda i: (i // nt, 0, i % nt, 0))

    full = lambda a: pl.BlockSpec(a.shape, lambda i: (0,) * a.ndim)
    col = pl.BlockSpec((8, tm), lambda i: (0, i))
    return pl.pallas_call(
        _post_kernel,
        grid=(T // tm,),
        in_specs=[row(D), row(WIDTH_A)] + [grouped(a) for a in (*obs, *lses)]
                 + [full(a) for a in (gmix, wg, bga, bgb, woa, wob, wout, gffn, wrT, br)],
        out_specs=[row(D), pl.BlockSpec((tm * SUBLANES, LANES), lambda i: (i, 0)), col, col, row(LANES),
                   pl.BlockSpec((ne, LANES), lambda i: (0, 0))],
        out_shape=[
            jax.ShapeDtypeStruct((T, D), F32), jax.ShapeDtypeStruct((T * SUBLANES, LANES), F32),
            jax.ShapeDtypeStruct((8, T), jnp.int32), jax.ShapeDtypeStruct((8, T), jnp.int32),
            jax.ShapeDtypeStruct((T, LANES), F32),
            jax.ShapeDtypeStruct((ne, LANES), F32),
        ],
        scratch_shapes=[pltpu.VMEM((ne, LANES), F32), pltpu.VMEM((6, WIDTH_G // LANES, tm, LANES), F32)],
        compiler_params=_cparams(("arbitrary",)),
        name="post_mix_router",
    )(x2, oa, *obs, *lses, gmix, wg, bga, bgb, woa, wob, wout, gffn, wrT, br)


def _dispatch_kernel(pend_ref, padded_ref, dest_ref, h_ref, xr_ref, zero_scr, sem):
    i = pl.program_id(0)
    tm = h_ref.shape[0] // SUBLANES
    br = zero_scr.shape[0]

    def zero_copy(blk_start):
        return pltpu.make_async_copy(zero_scr, xr_ref.at[pl.ds(pl.multiple_of(blk_start * SUBLANES, br), br), :], sem)

    @pl.when(i == 0)
    def _():
        zero_scr[...] = jnp.zeros_like(zero_scr)
        n_used = pend_ref[N_EXPERTS - 1] // EXPERT_ROWS
        n_blk = xr_ref.shape[0] // br

        def tail(fn):
            def go(e, c):
                @pl.when(padded_ref[e] > 0)
                def _():
                    fn(zero_copy(pend_ref[e] - EXPERT_ROWS))
                return c
            return go

        def unused(fn):
            def go(b, c):
                fn(zero_copy(b * EXPERT_ROWS))
                return c
            return go

        lax.fori_loop(0, N_EXPERTS, tail(lambda cp: cp.start()), 0)
        lax.fori_loop(n_used, n_blk, unused(lambda cp: cp.start()), 0)
        lax.fori_loop(0, N_EXPERTS, tail(lambda cp: cp.wait()), 0)
        lax.fori_loop(n_used, n_blk, unused(lambda cp: cp.wait()), 0)

    def row_copy(t, k):
        src = h_ref.at[pl.ds(pl.multiple_of(t * SUBLANES, SUBLANES), SUBLANES), :]
        dst = xr_ref.at[pl.ds(pl.multiple_of(dest_ref[k, t] * SUBLANES, SUBLANES), SUBLANES), :]
        return pltpu.make_async_copy(src, dst, sem)

    def start(tb, c):
        for u in range(ROW_UNROLL):
            for k in range(TOP_K):
                row_copy(tb * ROW_UNROLL + u, k).start(priority=(u * TOP_K + k) % 2)
        return c

    def wait(tb, c):
        for u in range(ROW_UNROLL):
            for k in range(TOP_K):
                row_copy(tb * ROW_UNROLL + u, k).wait()
        return c

    lax.fori_loop(0, tm // ROW_UNROLL, start, 0)
    lax.fori_loop(0, tm // ROW_UNROLL, wait, 0)


def _dispatch(pend, padded, dest3, h2t, n_rows):
    tm = DISPATCH_TM
    return pl.pallas_call(
        _dispatch_kernel,
        grid_spec=pltpu.PrefetchScalarGridSpec(
            num_scalar_prefetch=2,
            grid=(h2t.shape[0] // (tm * SUBLANES),),
            in_specs=[
                pl.BlockSpec((None, TOP_K, tm), lambda i, pe, pa: (i, 0, 0), memory_space=pltpu.SMEM),
                pl.BlockSpec((tm * SUBLANES, LANES), lambda i, pe, pa: (i, 0)),
            ],
            out_specs=pl.BlockSpec(memory_space=pl.ANY),
            scratch_shapes=[pltpu.VMEM((EXPERT_ROWS * SUBLANES, LANES), F32), pltpu.SemaphoreType.DMA(())],
        ),
        out_shape=jax.ShapeDtypeStruct((n_rows * SUBLANES, LANES), F32),
        compiler_params=_cparams(("arbitrary",)),
        name="moe_dispatch",
    )(pend, padded, dest3, h2t)


def _expert_kernel(pstart_ref, padded_ref, x_hbm, wg_ref, bg_ref, wu_ref, bu_ref, wd_ref, bd_ref, y_hbm,
                   w_scr, xbuf, ybuf, xsem, ysem):
    e = pl.program_id(0)
    tb = EXPERT_ROWS * SUBLANES
    nblk = padded_ref[e] // EXPERT_ROWS
    npair = nblk // EXPERT_PAIR
    row0 = pstart_ref[e]

    def rows(first_blk, nb_):
        return pl.ds(pl.multiple_of((row0 + first_blk * EXPERT_ROWS) * SUBLANES, tb), nb_ * tb)

    def x_copy(first_blk, nb_, slot):
        return pltpu.make_async_copy(x_hbm.at[rows(first_blk, nb_), :], xbuf.at[slot, pl.ds(0, nb_ * tb), :],
                                     xsem.at[slot])

    def y_copy(first_blk, nb_, slot):
        return pltpu.make_async_copy(ybuf.at[slot, pl.ds(0, nb_ * tb), :], y_hbm.at[rows(first_blk, nb_), :],
                                     ysem.at[slot])

    def ffn(slot, nb_):
        n = nb_ * EXPERT_ROWS
        x = _tiles_to_rows(xbuf.at[slot, pl.ds(0, nb_ * tb), :], n).astype(BF16)
        g = _dot(x, w_scr[0]) + bg_ref[...]
        u = _dot(x, w_scr[1]) + bu_ref[...]
        g = jnp.minimum(g, SWIGLU_LIMIT)
        u = jnp.clip(u, -SWIGLU_LIMIT, SWIGLU_LIMIT)
        a = g * jax.nn.sigmoid(SWIGLU_ALPHA * g) * (u + 1.0)
        _rows_to_tiles(ybuf.at[slot, pl.ds(0, nb_ * tb), :], _dot(a.astype(BF16), w_scr[2]) + bd_ref[...])

    @pl.when(nblk > 0)
    def _():
        has_tail = nblk > npair * EXPERT_PAIR
        nstep = npair + has_tail.astype(jnp.int32)

        def start_x(s, slot):
            @pl.when(s < npair)
            def _():
                x_copy(s * EXPERT_PAIR, EXPERT_PAIR, slot).start()

            @pl.when((s == npair) & has_tail)
            def _():
                x_copy(s * EXPERT_PAIR, 1, slot).start()

        def wait_y(s, slot):
            @pl.when(s < npair)
            def _():
                y_copy(s * EXPERT_PAIR, EXPERT_PAIR, slot).wait()

            @pl.when((s == npair) & has_tail)
            def _():
                y_copy(s * EXPERT_PAIR, 1, slot).wait()

        start_x(0, 0)
        w_scr[0] = wg_ref[...].astype(BF16)
        w_scr[1] = wu_ref[...].astype(BF16)
        w_scr[2] = wd_ref[...].astype(BF16)

        def body(s, c):
            slot = s % 2
            start_x(s + 1, 1 - slot)
            x_copy(s * EXPERT_PAIR, EXPERT_PAIR, slot).wait()

            @pl.when(s >= 2)
            def _():
                y_copy((s - 2) * EXPERT_PAIR, EXPERT_PAIR, slot).wait()

            ffn(slot, EXPERT_PAIR)
            y_copy(s * EXPERT_PAIR, EXPERT_PAIR, slot).start()
            return c

        lax.fori_loop(0, npair, body, 0)

        @pl.when(has_tail)
        def _():
            slot = npair % 2
            x_copy(npair * EXPERT_PAIR, 1, slot).wait()

            @pl.when(npair >= 2)
            def _():
                y_copy((npair - 2) * EXPERT_PAIR, EXPERT_PAIR, slot).wait()

            ffn(slot, 1)
            y_copy(npair * EXPERT_PAIR, 1, slot).start()

        @pl.when(nstep >= 2)
        def _():
            wait_y(nstep - 2, nstep % 2)

        wait_y(nstep - 1, (nstep - 1) % 2)

    @pl.when(e == pl.num_programs(0) - 1)
    def _():
        n_used = (pstart_ref[e] + padded_ref[e]) // EXPERT_ROWS
        n_all = y_hbm.shape[0] // tb
        ybuf[0, pl.ds(0, tb), :] = jnp.zeros((tb, LANES), F32)

        def zero_copy(b):
            return pltpu.make_async_copy(ybuf.at[0, pl.ds(0, tb), :],
                                         y_hbm.at[pl.ds(pl.multiple_of(b * tb, tb), tb), :], ysem.at[0])

        lax.fori_loop(n_used, n_all, lambda b, c: (zero_copy(b).start(), c)[1], 0)
        lax.fori_loop(n_used, n_all, lambda b, c: (zero_copy(b).wait(), c)[1], 0)


def _experts(pstart, padded, x_rows, w_gate, b_gate, w_up, b_up, w_down, b_down):
    E, D, F = w_gate.shape
    tb = EXPERT_PAIR * EXPERT_ROWS * SUBLANES
    wspec = lambda shape: pl.BlockSpec((None,) + shape, lambda e, ps, pa: (e, 0, 0))
    return pl.pallas_call(
        _expert_kernel,
        grid_spec=pltpu.PrefetchScalarGridSpec(
            num_scalar_prefetch=2,
            grid=(E,),
            in_specs=[
                pl.BlockSpec(memory_space=pl.ANY),
                wspec((D, F)), wspec((1, F)), wspec((D, F)), wspec((1, F)), wspec((F, D)), wspec((1, D)),
            ],
            out_specs=pl.BlockSpec(memory_space=pl.ANY),
            scratch_shapes=[pltpu.VMEM((3, D, F), BF16), pltpu.VMEM((2, tb, LANES), F32),
                            pltpu.VMEM((2, tb, LANES), F32), pltpu.SemaphoreType.DMA((2,)),
                            pltpu.SemaphoreType.DMA((2,))],
        ),
        out_shape=jax.ShapeDtypeStruct(x_rows.shape, F32),
        compiler_params=_cparams(("arbitrary",)),
        name="moe_experts",
    )(pstart, padded, x_rows, w_gate, b_gate.reshape(E, 1, F), w_up, b_up.reshape(E, 1, F),
      w_down, b_down.reshape(E, 1, D))


def _combine_kernel(dest_ref, dnext_ref, y_ref, x1_ref, p_ref, g_ref, o_ref, ybuf, sem):
    i = pl.program_id(0)
    tm = x1_ref.shape[0]
    slot = i % 2

    def row_copy(d_ref, sl, t, k):
        src = y_ref.at[pl.ds(pl.multiple_of(d_ref[k, t] * SUBLANES, SUBLANES), SUBLANES), :]
        dst = ybuf.at[sl, k, pl.ds(pl.multiple_of(t * SUBLANES, SUBLANES), SUBLANES), :]
        return pltpu.make_async_copy(src, dst, sem.at[sl])

    def request(d_ref, sl):
        def go(tb, c):
            for u in range(ROW_UNROLL):
                for k in range(TOP_K):
                    row_copy(d_ref, sl, tb * ROW_UNROLL + u, k).start(priority=(u * TOP_K + k) % 2)
            return c
        lax.fori_loop(0, tm // ROW_UNROLL, go, 0)

    @pl.when(i == 0)
    def _():
        request(dest_ref, 0)

    @pl.when(i + 1 < pl.num_programs(0))
    def _():
        request(dnext_ref, 1 - slot)

    def wait(tb, c):
        for u in range(ROW_UNROLL):
            for k in range(TOP_K):
                row_copy(dest_ref, slot, tb * ROW_UNROLL + u, k).wait()
        return c

    lax.fori_loop(0, tm // ROW_UNROLL, wait, 0)
    p = p_ref[...]
    y = x1_ref[...]
    for k in range(TOP_K):
        y = y + p[:, k:k + 1] * _tiles_to_rows(ybuf.at[slot, k], tm)
    o_ref[...] = _rms(y, g_ref[...])


def _combine(dest3, y_rows, x1, pnat, g_final):
    T, D = x1.shape
    tm = COMBINE_TM
    nt = T // tm
    return pl.pallas_call(
        _combine_kernel,
        grid=(nt,),
        in_specs=[
            pl.BlockSpec((None, TOP_K, tm), lambda i: (i, 0, 0), memory_space=pltpu.SMEM),
            pl.BlockSpec((None, TOP_K, tm), lambda i: (jnp.minimum(i + 1, nt - 1), 0, 0), memory_space=pltpu.SMEM),
            pl.BlockSpec(memory_space=pl.ANY),
            pl.BlockSpec((tm, D), lambda i: (i, 0)),
            pl.BlockSpec((tm, LANES), lambda i: (i, 0)),
            pl.BlockSpec((1, D), lambda i: (0, 0)),
        ],
        out_specs=pl.BlockSpec((tm, D), lambda i: (i, 0)),
        out_shape=jax.ShapeDtypeStruct((T, D), F32),
        scratch_shapes=[pltpu.VMEM((2, TOP_K, tm * SUBLANES, LANES), F32), pltpu.SemaphoreType.DMA((2,))],
        compiler_params=_cparams(("arbitrary",)),
        name="moe_combine",
    )(dest3, dest3, y_rows, x1, pnat, g_final)


def _rope_tables(S, d, tm):
    per = tm // d

    def cos_sin(shape, ax, f):
        rho = lax.broadcasted_iota(jnp.int32, shape, ax)
        w = rho % tm
        pos = ((rho // tm) * per + w % per) * d + w // per
        inv = jnp.float32(ROPE_THETA) ** (-f.astype(F32) / ROT_HALF)
        ang = pos.astype(F32) * inv
        return jnp.cos(ang), jnp.sin(ang)

    sh = (ROT_HALF, S)
    cq, sq = cos_sin(sh, 1, lax.broadcasted_iota(jnp.int32, sh, 0))
    sh = (S, ROT_HALF)
    cos, sin = cos_sin(sh, 0, lax.broadcasted_iota(jnp.int32, sh, 1))
    lane = lax.broadcasted_iota(jnp.int32, (ROT_HALF, LANES), 1) % HEAD_DIM
    f = lax.broadcasted_iota(jnp.int32, (ROT_HALF, LANES), 0)
    lo = ((lane == f)).astype(F32)
    hi = ((lane == f + ROT_HALF)).astype(F32)
    spread = lambda a, m: jnp.dot(a, m, precision=lax.Precision.HIGHEST)
    rest = (lax.broadcasted_iota(jnp.int32, (1, LANES), 1) % HEAD_DIM >= 2 * ROT_HALF).astype(F32)
    ck = spread(cos, lo + hi) + rest
    s1 = spread(-sin, lo)
    s2 = spread(sin, hi)
    return cq, sq, ck, s1, s2


def _layer(x, ln_mix_g, w_in, b_gate_a, b_gate_b, w_o_a, w_o_b, w_out, ln_ffn_g, w_router, b_router,
           w_gate, b_gate, w_up, b_up, w_down, b_down, ln_out_g):
    B, S, D = x.shape
    T = B * S
    qa0, ka0, va0 = 0, WIDTH_A, 2 * WIDTH_A
    qb0, kb0, vb0 = 3 * WIDTH_A, 3 * WIDTH_A + WIDTH_B, 3 * WIDTH_A + 2 * WIDTH_B
    g0 = 3 * WIDTH_A + 3 * WIDTH_B
    cols = lambda s, w: w_in[:, s:s + w]
    gmix = ln_mix_g.reshape(1, D)

    obs, lses = [], []
    o_a = None
    for gi, (_, d) in enumerate(DIL_PAIRS):
        off = gi * WIDTH_G
        wq, wk, wv = cols(qb0 + off, WIDTH_G), cols(kb0 + off, WIDTH_G), cols(vb0 + off, WIDTH_G)
        na = 0
        if gi == 0:
            na = WIDTH_A
            wq = jnp.concatenate([cols(qa0, WIDTH_A), wq], axis=1)
            wk = jnp.concatenate([cols(ka0, WIDTH_A), wk], axis=1)
            wv = jnp.concatenate([cols(va0, WIDTH_A), wv], axis=1)
        outs = _project(x, gmix, wq.T.astype(BF16), wk.astype(BF16), wv.T.astype(BF16), d, na)
        if gi == 0:
            qTa, vTa, ka, kmean = outs[:4]
            outs = outs[4:]
            o_a = _moba(qTa, ka, vTa, kmean.reshape(B, S // MOBA_BLOCK, WIDTH_A))
        qTb, vTb, kb = outs
        o_g, lse_g = _dilated(qTb, kb, vTb)
        obs.append(o_g)
        lses.append(lse_g)

    x1, h2, idxT, rankT, pnat, cnt = _post(
        x.reshape(T, D), o_a.reshape(T, WIDTH_A), obs, lses, gmix,
        cols(g0, 2 * D).astype(BF16), b_gate_a.reshape(1, D), b_gate_b.reshape(1, D),
        w_o_a.astype(BF16), w_o_b.astype(BF16), w_out.astype(BF16), ln_ffn_g.reshape(1, D),
        w_router.T, b_router.reshape(N_EXPERTS, 1))

    br = EXPERT_ROWS
    counts = cnt[:, 0].astype(jnp.int32)
    padded = (counts + br - 1) // br * br
    pend = jnp.cumsum(padded)
    pstart = pend - padded
    eids = jnp.arange(N_EXPERTS, dtype=jnp.int32)[:, None, None]
    dest = jnp.sum(jnp.where(idxT[None, :TOP_K] == eids, pstart[:, None, None], 0), axis=0) + rankT[:TOP_K]
    n_rows = T * TOP_K + N_EXPERTS * br

    dest_d = dest.reshape(TOP_K, T // DISPATCH_TM, DISPATCH_TM).transpose(1, 0, 2)
    x_rows = _dispatch(pend.astype(jnp.int32), padded.astype(jnp.int32), dest_d, h2, n_rows)
    y_rows = _experts(pstart.astype(jnp.int32), padded.astype(jnp.int32), x_rows, w_gate, b_gate, w_up, b_up,
                      w_down, b_down)
    dest_c = dest.reshape(TOP_K, T // COMBINE_TM, COMBINE_TM).transpose(1, 0, 2)
    out = _combine(dest_c, y_rows, x1, pnat, ln_out_g.reshape(1, D))
    return out.reshape(B, S, D)


def kernel(x, ln_mix_g, w_in, b_gate_a, b_gate_b, w_o_a, w_o_b, w_out, ln_ffn_g, w_router, b_router,
           w_gate, b_gate, w_up, b_up, w_down, b_down, ln_final_g):
    depth = ln_mix_g.shape[0]
    assert depth == 1, "the final RMSNorm is fused into the last layer's combine"
    return _layer(x, ln_mix_g[0], w_in[0], b_gate_a[0], b_gate_b[0], w_o_a[0], w_o_b[0], w_out[0],
                  ln_ffn_g[0], w_router[0], b_router[0], w_gate[0], b_gate[0], w_up[0], b_up[0],
                  w_down[0], b_down[0], ln_final_g)
```

```python
import functools

import jax
import jax.numpy as jnp
from jax import lax
from jax.experimental import pallas as pl
from jax.experimental.pallas import tpu as pltpu

D_MODEL = 1024
HEAD_DIM = 64
ROT_HALF = HEAD_DIM // 8
ROPE_THETA = 500000.0
N_HEADS_A = 8
MOBA_BLOCK = 256
MOBA_TOPK = 3
MOBA_QBLOCKS = 2
DIL_PAIRS = ((128, 1), (512, 4), (2048, 16))
DIL_BAND = 128
HEADS_PER_GROUP_B = 4
WIDTH_A = N_HEADS_A * HEAD_DIM
WIDTH_G = HEADS_PER_GROUP_B * HEAD_DIM
WIDTH_B = WIDTH_G * len(DIL_PAIRS)
N_EXPERTS = 32
TOP_K = 4
SWIGLU_LIMIT = 7.0
SWIGLU_ALPHA = 1.702
NORM_EPS = 1e-5
SCALE = HEAD_DIM ** -0.5
LOG2E = 1.4426950408889634

LANES = 128
PROJ_TM = 1024
PROJ_TM_DILATED = 2048
PERM_TILE = 512
DIL_TL = 512
POST_TM = 512
EXPERT_ROWS = 256
EXPERT_PAIR = 2
DISPATCH_TM = 512
COMBINE_TM = 256
SUBLANES = 8
ROW_UNROLL = 8
VMEM_LIMIT = 56 * 1024 * 1024
NEG = -1e30

BF16 = jnp.bfloat16
F32 = jnp.float32


def _dot(a, b):
    return jnp.dot(a, b, preferred_element_type=F32)


def _dot_nt(a, b):
    return lax.dot_general(a, b, (((1,), (1,)), ((), ())), preferred_element_type=F32)


def _rms(x, g):
    ms = jnp.mean(x * x, axis=-1, keepdims=True)
    return x * lax.rsqrt(ms + NORM_EPS) * g


def _rows_to_tiles(ref, val):
    n = val.shape[0]
    for c in range(val.shape[1] // LANES):
        ref[pl.ds(c, n, stride=SUBLANES), :] = val[:, c * LANES:(c + 1) * LANES]


def _tiles_to_rows(ref, n):
    return jnp.concatenate([ref[pl.ds(c, n, stride=SUBLANES), :] for c in range(SUBLANES)], axis=1)


def _cparams(sem):
    return pltpu.CompilerParams(dimension_semantics=sem, vmem_limit_bytes=VMEM_LIMIT)


def _deinterleave(h, d):
    tm = h.shape[0]
    sub = min(tm, PERM_TILE)
    cs = sub // d
    ri = lax.broadcasted_iota(jnp.int32, (sub, sub), 0)
    ui = lax.broadcasted_iota(jnp.int32, (sub, sub), 1)
    perm = (ui == (ri % cs) * d + ri // cs).astype(BF16)
    slabs = [_dot(perm, h[s0:s0 + sub]).astype(BF16) for s0 in range(0, tm, sub)]
    return jnp.concatenate([sl[r * cs:(r + 1) * cs] for r in range(d) for sl in slabs], axis=0)


def _proj_kernel(x_ref, g_ref, wq_ref, wk_ref, wv_ref, cq_ref, sq_ref, ck_ref, s1_ref, s2_ref, *outs, na, d):
    tm = x_ref.shape[0]
    h = _rms(x_ref[...], g_ref[...]).astype(BF16)
    if d > 1:
        h = _deinterleave(h, d)
    nq = wq_ref.shape[0]
    heads = nq // HEAD_DIM

    qT = _dot_nt(wq_ref[...], h)
    q3 = qT.reshape(heads, HEAD_DIM, tm)
    c = cq_ref[...][None]
    s = sq_ref[...][None]
    x1 = q3[:, 0:ROT_HALF]
    x2 = q3[:, ROT_HALF:2 * ROT_HALF]
    q3 = jnp.concatenate([x1 * c - x2 * s, x2 * c + x1 * s, q3[:, 2 * ROT_HALF:]], axis=1)
    hid = lax.broadcasted_iota(jnp.int32, (heads, 1, 1), 0)
    q3 = q3 * jnp.where(hid < na // HEAD_DIM, SCALE * LOG2E, SCALE)
    qT = q3.reshape(nq, tm).astype(BF16)

    vT = _dot_nt(wv_ref[...], h).astype(BF16)

    kk = _dot(h, wk_ref[...])
    ck, s1, s2 = ck_ref[...], s1_ref[...], s2_ref[...]
    kparts = []
    for gi in range(kk.shape[1] // LANES):
        kg = kk[:, gi * LANES:(gi + 1) * LANES]
        kparts.append(kg * ck + pltpu.roll(kg, LANES - ROT_HALF, 1) * s1 + pltpu.roll(kg, ROT_HALF, 1) * s2)

    if na:
        qa_ref, va_ref, ka_ref, km_ref, qb_ref, vb_ref, kb_ref = outs
        for blk in range(tm // MOBA_BLOCK):
            sl = slice(blk * MOBA_BLOCK, (blk + 1) * MOBA_BLOCK)
            qa_ref[blk] = qT[0:na, sl]
            va_ref[blk] = vT[0:na, sl]
        lane = lax.broadcasted_iota(jnp.int32, (tm, LANES), 1)
        row = lax.broadcasted_iota(jnp.int32, (tm, LANES), 0)
        blkid = pl.program_id(1) * (tm // MOBA_BLOCK) + row // MOBA_BLOCK
        for gi in range(na // LANES):
            kg = kparts[gi]
            for e in range(2):
                in_head = (lane >= HEAD_DIM * e) & (lane < HEAD_DIM * (e + 1))
                onehot = (lane - HEAD_DIM * (1 - e)) == blkid
                col = (2 * gi + e) * LANES
                ka_ref[:, col:col + LANES] = jnp.where(in_head, kg, onehot.astype(F32)).astype(BF16)
            km = kg.reshape(tm // MOBA_BLOCK, MOBA_BLOCK, LANES).sum(axis=1) * (1.0 / MOBA_BLOCK)
            km_ref[:, gi * LANES:(gi + 1) * LANES] = km
    else:
        qb_ref, vb_ref, kb_ref = outs
    per = tm // d
    for r in range(d):
        for cb in range(per // DIL_BAND):
            sl = slice(r * per + cb * DIL_BAND, r * per + (cb + 1) * DIL_BAND)
            qb_ref[r, cb] = qT[na:na + WIDTH_G, sl]
            vb_ref[r, cb] = vT[na:na + WIDTH_G, sl]
        for gi in range(WIDTH_G // LANES):
            kb_ref[r, :, gi * LANES:(gi + 1) * LANES] = kparts[na // LANES + gi][r * per:(r + 1) * per].astype(BF16)


def _project(x, g, wqT, wk, wvT, d, na):
    B, S, D = x.shape
    L = S // d
    tm = PROJ_TM if d == 1 else max(PROJ_TM_DILATED, d * DIL_BAND)
    per = tm // d
    cq, sq, ck, s1, s2 = _rope_tables(S, d, tm)
    grid = (B, S // tm)
    full = lambda a: pl.BlockSpec(a.shape, lambda b, i: (0,) * a.ndim)
    in_specs = [
        pl.BlockSpec((None, tm, D), lambda b, i: (b, i, 0)),
        full(g), full(wqT), full(wk), full(wvT),
        pl.BlockSpec((ROT_HALF, tm), lambda b, i: (0, i)),
        pl.BlockSpec((ROT_HALF, tm), lambda b, i: (0, i)),
        pl.BlockSpec((tm, LANES), lambda b, i: (i, 0)),
        pl.BlockSpec((tm, LANES), lambda b, i: (i, 0)),
        pl.BlockSpec((tm, LANES), lambda b, i: (i, 0)),
    ]
    nsub = per // DIL_BAND
    out_shape = [
        jax.ShapeDtypeStruct((B, d, L // DIL_BAND, WIDTH_G, DIL_BAND), BF16),
        jax.ShapeDtypeStruct((B, d, L // DIL_BAND, WIDTH_G, DIL_BAND), BF16),
        jax.ShapeDtypeStruct((B, d, L, WIDTH_G), BF16),
    ]
    out_specs = [
        pl.BlockSpec((None, d, nsub, WIDTH_G, DIL_BAND), lambda b, i: (b, 0, i, 0, 0)),
        pl.BlockSpec((None, d, nsub, WIDTH_G, DIL_BAND), lambda b, i: (b, 0, i, 0, 0)),
        pl.BlockSpec((None, d, per, WIDTH_G), lambda b, i: (b, 0, i, 0)),
    ]
    if na:
        nblk = tm // MOBA_BLOCK
        out_shape = [
            jax.ShapeDtypeStruct((B, S // MOBA_BLOCK, na, MOBA_BLOCK), BF16),
            jax.ShapeDtypeStruct((B, S // MOBA_BLOCK, na, MOBA_BLOCK), BF16),
            jax.ShapeDtypeStruct((B, S, 2 * na), BF16),
            jax.ShapeDtypeStruct((B, S // tm, nblk, na), F32),
        ] + out_shape
        out_specs = [
            pl.BlockSpec((None, nblk, na, MOBA_BLOCK), lambda b, i: (b, i, 0, 0)),
            pl.BlockSpec((None, nblk, na, MOBA_BLOCK), lambda b, i: (b, i, 0, 0)),
            pl.BlockSpec((None, tm, 2 * na), lambda b, i: (b, i, 0)),
            pl.BlockSpec((None, None, nblk, na), lambda b, i: (b, i, 0, 0)),
        ] + out_specs
    return pl.pallas_call(
        functools.partial(_proj_kernel, na=na, d=d),
        grid=grid, in_specs=in_specs, out_specs=out_specs, out_shape=out_shape,
        compiler_params=_cparams(("arbitrary",) * 2),
        name=f"proj_d{d}",
    )(x, g, wqT, wk, wvT, cq, sq, ck, s1, s2)


def _moba_kernel(q_ref, k_ref, v_ref, km_ref, o_ref, qa_scr, s_scr, p_scr):
    g = pl.program_id(2)
    nb = km_ref.shape[0]
    blk = MOBA_BLOCK
    tq = MOBA_QBLOCKS * blk
    q2 = jnp.concatenate([q_ref[i] for i in range(MOBA_QBLOCKS)], axis=1)
    bid = lax.broadcasted_iota(jnp.int32, (nb, tq), 0)
    qblk = MOBA_QBLOCKS * g + lax.broadcasted_iota(jnp.int32, (nb, tq), 1) // blk
    key_i = lax.broadcasted_iota(jnp.int32, (blk, tq), 0)
    qry_i = lax.broadcasted_iota(jnp.int32, (blk, tq), 1)
    km = km_ref[...].astype(BF16)
    zq = jnp.zeros((HEAD_DIM, tq), BF16)
    zb = jnp.zeros((HEAD_DIM - nb, tq), BF16)

    for h in range(2):
        hl = slice(LANES * h, LANES * (h + 1))
        qh = q2[HEAD_DIM * h:HEAD_DIM * (h + 1)]
        q_plain = jnp.concatenate([qh, zq] if h == 0 else [zq, qh], axis=0)
        gt = jnp.where(bid < qblk, _dot(km, q_plain), -jnp.inf)
        sel = jnp.zeros(gt.shape, jnp.bool_)
        for _ in range(MOBA_TOPK):
            mx = jnp.max(gt, axis=0, keepdims=True)
            first = jnp.min(jnp.where((gt == mx) & (mx > -jnp.inf), bid, nb), axis=0, keepdims=True)
            pick = bid == first
            sel = sel | pick
            gt = jnp.where(pick, -jnp.inf, gt)
        bias = jnp.where(sel, 0.0, NEG).astype(BF16)
        qa_scr[h] = jnp.concatenate([qh, bias, zb] if h == 0 else [bias, zb, qh], axis=0)
        bias_d = jnp.where(sel | (bid == qblk), 0.0, NEG).astype(BF16)
        q_diag = jnp.concatenate([qh, bias_d, zb] if h == 0 else [bias_d, zb, qh], axis=0)
        for i in range(MOBA_QBLOCKS):
            j = MOBA_QBLOCKS * g + i
            sd = _dot(k_ref[pl.ds(pl.multiple_of(j * blk, blk), blk), hl], q_diag)
            own = (qry_i >= i * blk) & (qry_i < (i + 1) * blk)
            s_scr[h, i] = jnp.where(own & (key_i > qry_i - i * blk), NEG, sd)

    def score_blocks(c):
        return [jnp.where(c < g, MOBA_QBLOCKS * c + i, nb - 1) for i in range(MOBA_QBLOCKS)]

    def value_blocks(it):
        return [jnp.where(it <= 1, MOBA_QBLOCKS * g + i, MOBA_QBLOCKS * (it - 2) + i) for i in range(MOBA_QBLOCKS)]

    def values(it, h, alpha, acc):
        vl = slice(HEAD_DIM * h, HEAD_DIM * (h + 1))
        acc = alpha * acc
        for i, j in enumerate(value_blocks(it)):
            acc = acc + _dot(v_ref[j, vl, :], p_scr[h, i])
        return acc

    p_scr[...] = jnp.zeros(p_scr.shape, BF16)

    def body(it, carry):
        alphas, ms, ls, accs = carry
        a_new, m_new, l_new, acc_new = [], [], [], []
        for h in range(2):
            hl = slice(LANES * h, LANES * (h + 1))
            qa = qa_scr[h]
            s_next = [_dot(k_ref[pl.ds(pl.multiple_of(j * blk, blk), blk), hl], qa) for j in score_blocks(it)]
            s_cur = [s_scr[h, i] for i in range(MOBA_QBLOCKS)]
            acc_new.append(values(it, h, alphas[h], accs[h]))
            mn = ms[h]
            for sc in s_cur:
                mn = jnp.maximum(mn, jnp.max(sc, axis=0, keepdims=True))
            alpha = jnp.exp2(ms[h] - mn)
            l = alpha * ls[h]
            for i in range(MOBA_QBLOCKS):
                p = jnp.exp2(s_cur[i] - mn)
                l = l + jnp.sum(p, axis=0, keepdims=True)
                s_scr[h, i] = s_next[i]
                p_scr[h, i] = p.astype(BF16)
            m_new.append(mn)
            l_new.append(l)
            a_new.append(alpha)
        return (tuple(a_new), tuple(m_new), tuple(l_new), tuple(acc_new))

    one = jnp.ones((1, tq), F32)
    neg = jnp.full((1, tq), NEG, F32)
    zl = jnp.zeros((1, tq), F32)
    za = jnp.zeros((HEAD_DIM, tq), F32)
    n_trips = lax.shift_right_logical(g + 2, 1)
    alphas, _, ls, accs = lax.fori_loop(0, n_trips, lambda t, c: body(2 * t + 1, body(2 * t, c)),
                                        ((one, one), (neg, neg), (zl, zl), (za, za)))
    accs = [values(2 * n_trips, h, alphas[h], accs[h]) for h in range(2)]
    oT = jnp.concatenate([accs[0] / ls[0], accs[1] / ls[1]], axis=0)
    o_ref[...] = oT.T.astype(o_ref.dtype)


def _moba(qT, k, vT, kmean):
    B, nb, wa, blk = qT.shape
    S = nb * blk
    hp = wa // LANES
    nq = MOBA_QBLOCKS
    assert nb % 16 == 0 and nb <= HEAD_DIM, "block-mask rows must fit the spare half of a head pair"
    return pl.pallas_call(
        _moba_kernel,
        grid=(B, hp, nb // nq),
        in_specs=[
            pl.BlockSpec((None, nq, LANES, blk), lambda b, p, n: (b, n, p, 0)),
            pl.BlockSpec((None, S, 2 * LANES), lambda b, p, n: (b, 0, p)),
            pl.BlockSpec((None, nb, LANES, blk), lambda b, p, n: (b, 0, p, 0)),
            pl.BlockSpec((None, nb, LANES), lambda b, p, n: (b, 0, p)),
        ],
        out_specs=pl.BlockSpec((None, nq * blk, LANES), lambda b, p, n: (b, n, p)),
        out_shape=jax.ShapeDtypeStruct((B, S, wa), BF16),
        scratch_shapes=[pltpu.VMEM((2, LANES, nq * blk), BF16), pltpu.VMEM((2, nq, blk, nq * blk), F32),
                        pltpu.VMEM((2, nq, blk, nq * blk), BF16)],
        compiler_params=_cparams(("arbitrary",) * 3),
        name="moba_attn",
    )(qT, k, vT, kmean)


def _dil_kernel(q_ref, k_ref, kp_ref, v_ref, vp_ref, o_ref, lse_ref):
    t = pl.program_id(2)
    band = DIL_BAND
    nsub = q_ref.shape[0]
    key_i = lax.broadcasted_iota(jnp.int32, (band, band), 0)
    qry_i = lax.broadcasted_iota(jnp.int32, (band, band), 1)
    rowid = lax.broadcasted_iota(jnp.int32, (LANES, band), 0)
    own_ok = key_i <= qry_i
    prev_ok = key_i >= qry_i
    for c in range(nsub):
        o_parts, l_parts = [], []
        for hp in range(WIDTH_G // LANES):
            cols = slice(hp * LANES, (hp + 1) * LANES)
            k_own = k_ref[c * band:(c + 1) * band, cols]
            k_prev = kp_ref[:, cols] if c == 0 else k_ref[(c - 1) * band:c * band, cols]
            q2 = q_ref[c, cols, :]
            for h in range(2):
                qh = jnp.where((rowid >= HEAD_DIM * h) & (rowid < HEAD_DIM * (h + 1)), q2, jnp.zeros_like(q2))
                pmask = (prev_ok & (t > 0)) if c == 0 else prev_ok
                s_own = jnp.where(own_ok, _dot(k_own, qh), -jnp.inf)
                s_prev = jnp.where(pmask, _dot(k_prev, qh), -jnp.inf)
                m = jnp.maximum(jnp.max(s_own, axis=0, keepdims=True), jnp.max(s_prev, axis=0, keepdims=True))
                p_own = jnp.exp(s_own - m)
                p_prev = jnp.exp(s_prev - m)
                l = jnp.sum(p_own, axis=0, keepdims=True) + jnp.sum(p_prev, axis=0, keepdims=True)
                rows = slice(hp * LANES + h * HEAD_DIM, hp * LANES + (h + 1) * HEAD_DIM)
                v_own = v_ref[c, rows, :]
                v_prev = vp_ref[rows, :] if c == 0 else v_ref[c - 1, rows, :]
                oT = _dot(v_own, p_own.astype(BF16)) + _dot(v_prev, p_prev.astype(BF16))
                o_parts.append(oT / l)
                l_parts.append(jnp.broadcast_to(m + jnp.log(l), (HEAD_DIM, band)))
        o_ref[c * band:(c + 1) * band, :] = jnp.concatenate(o_parts, axis=0).T
        lse_ref[c * band:(c + 1) * band, :] = jnp.concatenate(l_parts, axis=0).T


def _dilated(qT, k, vT):
    B, d, nblk, wg, band = qT.shape
    L = nblk * band
    tl = min(DIL_TL, L)
    nsub = tl // band
    prev = lambda t: jnp.maximum(t * nsub - 1, 0)
    return pl.pallas_call(
        _dil_kernel,
        grid=(B, d, L // tl),
        in_specs=[
            pl.BlockSpec((None, None, nsub, wg, band), lambda b, r, t: (b, r, t, 0, 0)),
            pl.BlockSpec((None, None, tl, wg), lambda b, r, t: (b, r, t, 0)),
            pl.BlockSpec((None, None, band, wg), lambda b, r, t: (b, r, prev(t), 0)),
            pl.BlockSpec((None, None, nsub, wg, band), lambda b, r, t: (b, r, t, 0, 0)),
            pl.BlockSpec((None, None, None, wg, band), lambda b, r, t: (b, r, prev(t), 0, 0)),
        ],
        out_specs=[
            pl.BlockSpec((None, None, tl, wg), lambda b, r, t: (b, r, t, 0)),
            pl.BlockSpec((None, None, tl, wg), lambda b, r, t: (b, r, t, 0)),
        ],
        out_shape=[jax.ShapeDtypeStruct((B, d, L, wg), F32)] * 2,
        compiler_params=_cparams(("arbitrary",) * 3),
        name=f"dilated_d{d}",
    )(qT, k, k, vT, vT)


def _interleave(ref, scr):
    d, per, w = ref.shape
    if d == 1:
        return ref[0]
    for r in range(d):
        for sl in range(w // LANES):
            scr[sl, pl.ds(r, per, stride=d), :] = ref[r, :, sl * LANES:(sl + 1) * LANES]
    return jnp.concatenate([scr[sl] for sl in range(w // LANES)], axis=1)


def _post_kernel(x_ref, oa_ref, o1_ref, o2_ref, o3_ref, l1_ref, l2_ref, l3_ref, gmix_ref, wg_ref, bga_ref,
                 bgb_ref, woa_ref, wob_ref, wout_ref, gffn_ref, wr_ref, br_ref,
                 x1_ref, h2_ref, idx_ref, rank_ref, pnat_ref, cnt_ref, carry_scr, il_scr):
    i = pl.program_id(0)
    tm, D = x_ref.shape
    ne = wr_ref.shape[0]

    @pl.when(i == 0)
    def _():
        carry_scr[...] = jnp.zeros_like(carry_scr)

    x = x_ref[...]
    h = _rms(x, gmix_ref[...]).astype(BF16)
    gates = _dot(h, wg_ref[...])
    ga = gates[:, :D] + bga_ref[...]
    gb = gates[:, D:] + bgb_ref[...]
    l1, l2, l3 = [_interleave(r, il_scr.at[n]) for n, r in enumerate((l1_ref, l2_ref, l3_ref))]
    o1, o2, o3 = [_interleave(r, il_scr.at[3 + n]) for n, r in enumerate((o1_ref, o2_ref, o3_ref))]
    mx = jnp.maximum(jnp.maximum(l1, l2), l3)
    e1, e2, e3 = jnp.exp(l1 - mx), jnp.exp(l2 - mx), jnp.exp(l3 - mx)
    ob = (e1 * o1 + e2 * o2 + e3 * o3) / (e1 + e2 + e3)
    ya = _dot(oa_ref[...], woa_ref[...])
    yb = _dot(ob.astype(BF16), wob_ref[...])
    mix = jax.nn.sigmoid(ga) * ya + jax.nn.sigmoid(gb) * yb
    x1 = x + _dot(mix.astype(BF16), wout_ref[...])
    x1_ref[...] = x1
    h2 = _rms(x1, gffn_ref[...])
    _rows_to_tiles(h2_ref, h2)

    wr = wr_ref[...]
    wr_hi = wr.astype(BF16)
    wr_lo = (wr - wr_hi.astype(F32)).astype(BF16)
    h2_hi = h2.astype(BF16)
    h2_lo = (h2 - h2_hi.astype(F32)).astype(BF16)
    logits = _dot_nt(wr_hi, h2_hi) + (_dot_nt(wr_hi, h2_lo) + _dot_nt(wr_lo, h2_hi)) + br_ref[...]
    eid = lax.broadcasted_iota(jnp.int32, (ne, tm), 0)
    g = logits
    vals, idxs, picks = [], [], []
    for _ in range(TOP_K):
        m = jnp.max(g, axis=0, keepdims=True)
        first = jnp.min(jnp.where(g == m, eid, ne), axis=0, keepdims=True)
        pick = eid == first
        vals.append(m)
        idxs.append(first)
        picks.append(pick)
        g = jnp.where(pick, -jnp.inf, g)
    es = [jnp.exp(v - vals[0]) for v in vals]
    den = es[0] + es[1] + es[2] + es[3]
    probs = [e / den for e in es]

    onehot = jnp.zeros((ne, tm), F32)
    for pick in picks:
        onehot = onehot + pick.astype(F32)
    earlier = (lax.broadcasted_iota(jnp.int32, (tm, tm), 0) < lax.broadcasted_iota(jnp.int32, (tm, tm), 1))
    prefix = _dot(onehot.astype(BF16), earlier.astype(BF16)) + carry_scr[:, 0:1]
    ranks = [jnp.sum(jnp.where(pick, prefix, 0.0), axis=0, keepdims=True) for pick in picks]
    carry_scr[...] = carry_scr[...] + jnp.sum(onehot, axis=1, keepdims=True)
    cnt_ref[...] = carry_scr[...]

    zi = jnp.zeros((8 - TOP_K, tm), jnp.int32)
    idx_ref[...] = jnp.concatenate(idxs + [zi], axis=0)
    rank_ref[...] = jnp.concatenate([r.astype(jnp.int32) for r in ranks] + [zi], axis=0)
    pnat_ref[...] = jnp.concatenate(probs + [jnp.zeros((LANES - TOP_K, tm), F32)], axis=0).T


def _post(x2, oa, obs, lses, gmix, wg, bga, bgb, woa, wob, wout, gffn, wrT, br):
    T, D = x2.shape
    tm = POST_TM
    ne = wrT.shape[0]
    row = lambda w: pl.BlockSpec((tm, w), lambda i: (i, 0))

    def grouped(a):
        _, d, L, w = a.shape
        nt = L * d // tm
        return pl.BlockSpec((None, d, tm // d, w), lambda i: (i // nt, 0, i % nt, 0))

    full = lambda a: pl.BlockSpec(a.shape, lambda i: (0,) * a.ndim)
    col = pl.BlockSpec((8, tm), lambda i: (0, i))
    return pl.pallas_call(
        _post_kernel,
        grid=(T // tm,),
        in_specs=[row(D), row(WIDTH_A)] + [grouped(a) for a in (*obs, *lses)]
                 + [full(a) for a in (gmix, wg, bga, bgb, woa, wob, wout, gffn, wrT, br)],
        out_specs=[row(D), pl.BlockSpec((tm * SUBLANES, LANES), lambda i: (i, 0)), col, col, row(LANES),
                   pl.BlockSpec((ne, LANES), lambda i: (0, 0))],
        out_shape=[
            jax.ShapeDtypeStruct((T, D), F32), jax.ShapeDtypeStruct((T * SUBLANES, LANES), F32),
            jax.ShapeDtypeStruct((8, T), jnp.int32), jax.ShapeDtypeStruct((8, T), jnp.int32),
            jax.ShapeDtypeStruct((T, LANES), F32),
            jax.ShapeDtypeStruct((ne, LANES), F32),
        ],
        scratch_shapes=[pltpu.VMEM((ne, LANES), F32), pltpu.VMEM((6, WIDTH_G // LANES, tm, LANES), F32)],
        compiler_params=_cparams(("arbitrary",)),
        name="post_mix_router",
    )(x2, oa, *obs, *lses, gmix, wg, bga, bgb, woa, wob, wout, gffn, wrT, br)


def _dispatch_kernel(pend_ref, padded_ref, dest_ref, h_ref, xr_ref, zero_scr, sem):
    i = pl.program_id(0)
    tm = h_ref.shape[0] // SUBLANES
    br = zero_scr.shape[0]

    def zero_copy(blk_start):
        return pltpu.make_async_copy(zero_scr, xr_ref.at[pl.ds(pl.multiple_of(blk_start * SUBLANES, br), br), :], sem)

    @pl.when(i == 0)
    def _():
        zero_scr[...] = jnp.zeros_like(zero_scr)
        n_used = pend_ref[N_EXPERTS - 1] // EXPERT_ROWS
        n_blk = xr_ref.shape[0] // br

        def tail(fn):
            def go(e, c):
                @pl.when(padded_ref[e] > 0)
                def _():
                    fn(zero_copy(pend_ref[e] - EXPERT_ROWS))
                return c
            return go

        def unused(fn):
            def go(b, c):
                fn(zero_copy(b * EXPERT_ROWS))
                return c
            return go

        lax.fori_loop(0, N_EXPERTS, tail(lambda cp: cp.start()), 0)
        lax.fori_loop(n_used, n_blk, unused(lambda cp: cp.start()), 0)
        lax.fori_loop(0, N_EXPERTS, tail(lambda cp: cp.wait()), 0)
        lax.fori_loop(n_used, n_blk, unused(lambda cp: cp.wait()), 0)

    def row_copy(t, k):
        src = h_ref.at[pl.ds(pl.multiple_of(t * SUBLANES, SUBLANES), SUBLANES), :]
        dst = xr_ref.at[pl.ds(pl.multiple_of(dest_ref[k, t] * SUBLANES, SUBLANES), SUBLANES), :]
        return pltpu.make_async_copy(src, dst, sem)

    def start(tb, c):
        for u in range(ROW_UNROLL):
            for k in range(TOP_K):
                row_copy(tb * ROW_UNROLL + u, k).start(priority=(u * TOP_K + k) % 2)
        return c

    def wait(tb, c):
        for u in range(ROW_UNROLL):
            for k in range(TOP_K):
                row_copy(tb * ROW_UNROLL + u, k).wait()
        return c

    lax.fori_loop(0, tm // ROW_UNROLL, start, 0)
    lax.fori_loop(0, tm // ROW_UNROLL, wait, 0)


def _dispatch(pend, padded, dest3, h2t, n_rows):
    tm = DISPATCH_TM
    return pl.pallas_call(
        _dispatch_kernel,
        grid_spec=pltpu.PrefetchScalarGridSpec(
            num_scalar_prefetch=2,
            grid=(h2t.shape[0] // (tm * SUBLANES),),
            in_specs=[
                pl.BlockSpec((None, TOP_K, tm), lambda i, pe, pa: (i, 0, 0), memory_space=pltpu.SMEM),
                pl.BlockSpec((tm * SUBLANES, LANES), lambda i, pe, pa: (i, 0)),
            ],
            out_specs=pl.BlockSpec(memory_space=pl.ANY),
            scratch_shapes=[pltpu.VMEM((EXPERT_ROWS * SUBLANES, LANES), F32), pltpu.SemaphoreType.DMA(())],
        ),
        out_shape=jax.ShapeDtypeStruct((n_rows * SUBLANES, LANES), F32),
        compiler_params=_cparams(("arbitrary",)),
        name="moe_dispatch",
    )(pend, padded, dest3, h2t)


def _expert_kernel(pstart_ref, padded_ref, x_hbm, wg_ref, bg_ref, wu_ref, bu_ref, wd_ref, bd_ref, y_hbm,
                   w_scr, xbuf, ybuf, xsem, ysem):
    e = pl.program_id(0)
    tb = EXPERT_ROWS * SUBLANES
    nblk = padded_ref[e] // EXPERT_ROWS
    npair = nblk // EXPERT_PAIR
    row0 = pstart_ref[e]

    def rows(first_blk, nb_):
        return pl.ds(pl.multiple_of((row0 + first_blk * EXPERT_ROWS) * SUBLANES, tb), nb_ * tb)

    def x_copy(first_blk, nb_, slot):
        return pltpu.make_async_copy(x_hbm.at[rows(first_blk, nb_), :], xbuf.at[slot, pl.ds(0, nb_ * tb), :],
                                     xsem.at[slot])

    def y_copy(first_blk, nb_, slot):
        return pltpu.make_async_copy(ybuf.at[slot, pl.ds(0, nb_ * tb), :], y_hbm.at[rows(first_blk, nb_), :],
                                     ysem.at[slot])

    def ffn(slot, nb_):
        n = nb_ * EXPERT_ROWS
        x = _tiles_to_rows(xbuf.at[slot, pl.ds(0, nb_ * tb), :], n).astype(BF16)
        g = _dot(x, w_scr[0]) + bg_ref[...]
        u = _dot(x, w_scr[1]) + bu_ref[...]
        g = jnp.minimum(g, SWIGLU_LIMIT)
        u = jnp.clip(u, -SWIGLU_LIMIT, SWIGLU_LIMIT)
        a = g * jax.nn.sigmoid(SWIGLU_ALPHA * g) * (u + 1.0)
        _rows_to_tiles(ybuf.at[slot, pl.ds(0, nb_ * tb), :], _dot(a.astype(BF16), w_scr[2]) + bd_ref[...])

    @pl.when(nblk > 0)
    def _():
        has_tail = nblk > npair * EXPERT_PAIR
        nstep = npair + has_tail.astype(jnp.int32)

        def start_x(s, slot):
            @pl.when(s < npair)
            def _():
                x_copy(s * EXPERT_PAIR, EXPERT_PAIR, slot).start()

            @pl.when((s == npair) & has_tail)
            def _():
                x_copy(s * EXPERT_PAIR, 1, slot).start()

        def wait_y(s, slot):
            @pl.when(s < npair)
            def _():
                y_copy(s * EXPERT_PAIR, EXPERT_PAIR, slot).wait()

            @pl.when((s == npair) & has_tail)
            def _():
                y_copy(s * EXPERT_PAIR, 1, slot).wait()

        start_x(0, 0)
        w_scr[0] = wg_ref[...].astype(BF16)
        w_scr[1] = wu_ref[...].astype(BF16)
        w_scr[2] = wd_ref[...].astype(BF16)

        def body(s, c):
            slot = s % 2
            start_x(s + 1, 1 - slot)
            x_copy(s * EXPERT_PAIR, EXPERT_PAIR, slot).wait()

            @pl.when(s >= 2)
            def _():
                y_copy((s - 2) * EXPERT_PAIR, EXPERT_PAIR, slot).wait()

            ffn(slot, EXPERT_PAIR)
            y_copy(s * EXPERT_PAIR, EXPERT_PAIR, slot).start()
            return c

        lax.fori_loop(0, npair, body, 0)

        @pl.when(has_tail)
        def _():
            slot = npair % 2
            x_copy(npair * EXPERT_PAIR, 1, slot).wait()

            @pl.when(npair >= 2)
            def _():
                y_copy((npair - 2) * EXPERT_PAIR, EXPERT_PAIR, slot).wait()

            ffn(slot, 1)
            y_copy(npair * EXPERT_PAIR, 1, slot).start()

        @pl.when(nstep >= 2)
        def _():
            wait_y(nstep - 2, nstep % 2)

        wait_y(nstep - 1, (nstep - 1) % 2)

    @pl.when(e == pl.num_programs(0) - 1)
    def _():
        n_used = (pstart_ref[e] + padded_ref[e]) // EXPERT_ROWS
        n_all = y_hbm.shape[0] // tb
        ybuf[0, pl.ds(0, tb), :] = jnp.zeros((tb, LANES), F32)

        def zero_copy(b):
            return pltpu.make_async_copy(ybuf.at[0, pl.ds(0, tb), :],
                                         y_hbm.at[pl.ds(pl.multiple_of(b * tb, tb), tb), :], ysem.at[0])

        lax.fori_loop(n_used, n_all, lambda b, c: (zero_copy(b).start(), c)[1], 0)
        lax.fori_loop(n_used, n_all, lambda b, c: (zero_copy(b).wait(), c)[1], 0)


def _experts(pstart, padded, x_rows, w_gate, b_gate, w_up, b_up, w_down, b_down):
    E, D, F = w_gate.shape
    tb = EXPERT_PAIR * EXPERT_ROWS * SUBLANES
    wspec = lambda shape: pl.BlockSpec((None,) + shape, lambda e, ps, pa: (e, 0, 0))
    return pl.pallas_call(
        _expert_kernel,
        grid_spec=pltpu.PrefetchScalarGridSpec(
            num_scalar_prefetch=2,
            grid=(E,),
            in_specs=[
                pl.BlockSpec(memory_space=pl.ANY),
                wspec((D, F)), wspec((1, F)), wspec((D, F)), wspec((1, F)), wspec((F, D)), wspec((1, D)),
            ],
            out_specs=pl.BlockSpec(memory_space=pl.ANY),
            scratch_shapes=[pltpu.VMEM((3, D, F), BF16), pltpu.VMEM((2, tb, LANES), F32),
                            pltpu.VMEM((2, tb, LANES), F32), pltpu.SemaphoreType.DMA((2,)),
                            pltpu.SemaphoreType.DMA((2,))],
        ),
        out_shape=jax.ShapeDtypeStruct(x_rows.shape, F32),
        compiler_params=_cparams(("arbitrary",)),
        name="moe_experts",
    )(pstart, padded, x_rows, w_gate, b_gate.reshape(E, 1, F), w_up, b_up.reshape(E, 1, F),
      w_down, b_down.reshape(E, 1, D))


def _combine_kernel(dest_ref, dnext_ref, y_ref, x1_ref, p_ref, g_ref, o_ref, ybuf, sem):
    i = pl.program_id(0)
    tm = x1_ref.shape[0]
    slot = i % 2

    def row_copy(d_ref, sl, t, k):
        src = y_ref.at[pl.ds(pl.multiple_of(d_ref[k, t] * SUBLANES, SUBLANES), SUBLANES), :]
        dst = ybuf.at[sl, k, pl.ds(pl.multiple_of(t * SUBLANES, SUBLANES), SUBLANES), :]
        return pltpu.make_async_copy(src, dst, sem.at[sl])

    def request(d_ref, sl):
        def go(tb, c):
            for u in range(ROW_UNROLL):
                for k in range(TOP_K):
                    row_copy(d_ref, sl, tb * ROW_UNROLL + u, k).start(priority=(u * TOP_K + k) % 2)
            return c
        lax.fori_loop(0, tm // ROW_UNROLL, go, 0)

    @pl.when(i == 0)
    def _():
        request(dest_ref, 0)

    @pl.when(i + 1 < pl.num_programs(0))
    def _():
        request(dnext_ref, 1 - slot)

    def wait(tb, c):
        for u in range(ROW_UNROLL):
            for k in range(TOP_K):
                row_copy(dest_ref, slot, tb * ROW_UNROLL + u, k).wait()
        return c

    lax.fori_loop(0, tm // ROW_UNROLL, wait, 0)
    p = p_ref[...]
    y = x1_ref[...]
    for k in range(TOP_K):
        y = y + p[:, k:k + 1] * _tiles_to_rows(ybuf.at[slot, k], tm)
    o_ref[...] = _rms(y, g_ref[...])


def _combine(dest3, y_rows, x1, pnat, g_final):
    T, D = x1.shape
    tm = COMBINE_TM
    nt = T // tm
    return pl.pallas_call(
        _combine_kernel,
        grid=(nt,),
        in_specs=[
            pl.BlockSpec((None, TOP_K, tm), lambda i: (i, 0, 0), memory_space=pltpu.SMEM),
            pl.BlockSpec((None, TOP_K, tm), lambda i: (jnp.minimum(i + 1, nt - 1), 0, 0), memory_space=pltpu.SMEM),
            pl.BlockSpec(memory_space=pl.ANY),
            pl.BlockSpec((tm, D), lambda i: (i, 0)),
            pl.BlockSpec((tm, LANES), lambda i: (i, 0)),
            pl.BlockSpec((1, D), lambda i: (0, 0)),
        ],
        out_specs=pl.BlockSpec((tm, D), lambda i: (i, 0)),
        out_shape=jax.ShapeDtypeStruct((T, D), F32),
        scratch_shapes=[pltpu.VMEM((2, TOP_K, tm * SUBLANES, LANES), F32), pltpu.SemaphoreType.DMA((2,))],
        compiler_params=_cparams(("arbitrary",)),
        name="moe_combine",
    )(dest3, dest3, y_rows, x1, pnat, g_final)


def _rope_tables(S, d, tm):
    per = tm // d

    def cos_sin(shape, ax, f):
        rho = lax.broadcasted_iota(jnp.int32, shape, ax)
        w = rho % tm
        pos = ((rho // tm) * per + w % per) * d + w // per
        inv = jnp.float32(ROPE_THETA) ** (-f.astype(F32) / ROT_HALF)
        ang = pos.astype(F32) * inv
        return jnp.cos(ang), jnp.sin(ang)

    sh = (ROT_HALF, S)
    cq, sq = cos_sin(sh, 1, lax.broadcasted_iota(jnp.int32, sh, 0))
    sh = (S, ROT_HALF)
    cos, sin = cos_sin(sh, 0, lax.broadcasted_iota(jnp.int32, sh, 1))
    lane = lax.broadcasted_iota(jnp.int32, (ROT_HALF, LANES), 1) % HEAD_DIM
    f = lax.broadcasted_iota(jnp.int32, (ROT_HALF, LANES), 0)
    lo = ((lane == f)).astype(F32)
    hi = ((lane == f + ROT_HALF)).astype(F32)
    spread = lambda a, m: jnp.dot(a, m, precision=lax.Precision.HIGHEST)
    rest = (lax.broadcasted_iota(jnp.int32, (1, LANES), 1) % HEAD_DIM >= 2 * ROT_HALF).astype(F32)
    ck = spread(cos, lo + hi) + rest
    s1 = spread(-sin, lo)
    s2 = spread(sin, hi)
    return cq, sq, ck, s1, s2


def _layer(x, ln_mix_g, w_in, b_gate_a, b_gate_b, w_o_a, w_o_b, w_out, ln_ffn_g, w_router, b_router,
           w_gate, b_gate, w_up, b_up, w_down, b_down, ln_out_g):
    B, S, D = x.shape
    T = B * S
    qa0, ka0, va0 = 0, WIDTH_A, 2 * WIDTH_A
    qb0, kb0, vb0 = 3 * WIDTH_A, 3 * WIDTH_A + WIDTH_B, 3 * WIDTH_A + 2 * WIDTH_B
    g0 = 3 * WIDTH_A + 3 * WIDTH_B
    cols = lambda s, w: w_in[:, s:s + w]
    gmix = ln_mix_g.reshape(1, D)

    obs, lses = [], []
    o_a = None
    for gi, (_, d) in enumerate(DIL_PAIRS):
        off = gi * WIDTH_G
        wq, wk, wv = cols(qb0 + off, WIDTH_G), cols(kb0 + off, WIDTH_G), cols(vb0 + off, WIDTH_G)
        na = 0
        if gi == 0:
            na = WIDTH_A
            wq = jnp.concatenate([cols(qa0, WIDTH_A), wq], axis=1)
            wk = jnp.concatenate([cols(ka0, WIDTH_A), wk], axis=1)
            wv = jnp.concatenate([cols(va0, WIDTH_A), wv], axis=1)
        outs = _project(x, gmix, wq.T.astype(BF16), wk.astype(BF16), wv.T.astype(BF16), d, na)
        if gi == 0:
            qTa, vTa, ka, kmean = outs[:4]
            outs = outs[4:]
            o_a = _moba(qTa, ka, vTa, kmean.reshape(B, S // MOBA_BLOCK, WIDTH_A))
        qTb, vTb, kb = outs
        o_g, lse_g = _dilated(qTb, kb, vTb)
        obs.append(o_g)
        lses.append(lse_g)

    x1, h2, idxT, rankT, pnat, cnt = _post(
        x.reshape(T, D), o_a.reshape(T, WIDTH_A), obs, lses, gmix,
        cols(g0, 2 * D).astype(BF16), b_gate_a.reshape(1, D), b_gate_b.reshape(1, D),
        w_o_a.astype(BF16), w_o_b.astype(BF16), w_out.astype(BF16), ln_ffn_g.reshape(1, D),
        w_router.T, b_router.reshape(N_EXPERTS, 1))

    br = EXPERT_ROWS
    counts = cnt[:, 0].astype(jnp.int32)
    padded = (counts + br - 1) // br * br
    pend = jnp.cumsum(padded)
    pstart = pend - padded
    eids = jnp.arange(N_EXPERTS, dtype=jnp.int32)[:, None, None]
    dest = jnp.sum(jnp.where(idxT[None, :TOP_K] == eids, pstart[:, None, None], 0), axis=0) + rankT[:TOP_K]
    n_rows = T * TOP_K + N_EXPERTS * br

    dest_d = dest.reshape(TOP_K, T // DISPATCH_TM, DISPATCH_TM).transpose(1, 0, 2)
    x_rows = _dispatch(pend.astype(jnp.int32), padded.astype(jnp.int32), dest_d, h2, n_rows)
    y_rows = _experts(pstart.astype(jnp.int32), padded.astype(jnp.int32), x_rows, w_gate, b_gate, w_up, b_up,
                      w_down, b_down)
    dest_c = dest.reshape(TOP_K, T // COMBINE_TM, COMBINE_TM).transpose(1, 0, 2)
    out = _combine(dest_c, y_rows, x1, pnat, ln_out_g.reshape(1, D))
    return out.reshape(B, S, D)


def kernel(x, ln_mix_g, w_in, b_gate_a, b_gate_b, w_o_a, w_o_b, w_out, ln_ffn_g, w_router, b_router,
           w_gate, b_gate, w_up, b_up, w_down, b_down, ln_final_g):
    depth = ln_mix_g.shape[0]
    assert depth == 1, "the final RMSNorm is fused into the last layer's combine"
    return _layer(x, ln_mix_g[0], w_in[0], b_gate_a[0], b_gate_b[0], w_o_a[0], w_o_b[0], w_out[0],
                  ln_ffn_g[0], w_router[0], b_router[0], w_gate[0], b_gate[0], w_up[0], b_up[0],
                  w_down[0], b_down[0], ln_final_g)
```

```python
import functools

import jax
import jax.numpy as jnp
from jax import lax
from jax.experimental import pallas as pl
from jax.experimental.pallas import tpu as pltpu

D_MODEL = 1024
HEAD_DIM = 64
ROT_HALF = HEAD_DIM // 8
ROPE_THETA = 500000.0
N_HEADS_A = 8
MOBA_BLOCK = 256
MOBA_TOPK = 3
MOBA_QBLOCKS = 2
MOBA_UNROLL = 4
DIL_PAIRS = ((128, 1), (512, 4), (2048, 16))
DIL_BAND = 128
HEADS_PER_GROUP_B = 4
WIDTH_A = N_HEADS_A * HEAD_DIM
WIDTH_G = HEADS_PER_GROUP_B * HEAD_DIM
WIDTH_B = WIDTH_G * len(DIL_PAIRS)
N_EXPERTS = 32
TOP_K = 4
SWIGLU_LIMIT = 7.0
SWIGLU_ALPHA = 1.702
NORM_EPS = 1e-5
SCALE = HEAD_DIM ** -0.5
LOG2E = 1.4426950408889634

LANES = 128
PROJ_TM = 1024
PROJ_TM_DILATED = 2048
PERM_TILE = 512
DIL_TL = 2048
POST_TM = 512
EXPERT_ROWS = 256
EXPERT_PAIR = 2
DISPATCH_TM = 512
COMBINE_TM = 256
SUBLANES = 8
ROW_UNROLL = 8
VMEM_LIMIT = 56 * 1024 * 1024
NEG = -1e30

BF16 = jnp.bfloat16
F32 = jnp.float32


def _dot(a, b):
    return jnp.dot(a, b, preferred_element_type=F32)


def _dot_nt(a, b):
    return lax.dot_general(a, b, (((1,), (1,)), ((), ())), preferred_element_type=F32)


def _rms(x, g):
    ms = jnp.mean(x * x, axis=-1, keepdims=True)
    return x * lax.rsqrt(ms + NORM_EPS) * g


def _rows_to_tiles(ref, val):
    n = val.shape[0]
    for c in range(val.shape[1] // LANES):
        ref[pl.ds(c, n, stride=SUBLANES), :] = val[:, c * LANES:(c + 1) * LANES]


def _tiles_to_rows(ref, n):
    return jnp.concatenate([ref[pl.ds(c, n, stride=SUBLANES), :] for c in range(SUBLANES)], axis=1)


def _cparams(sem):
    return pltpu.CompilerParams(dimension_semantics=sem, vmem_limit_bytes=VMEM_LIMIT)


def _deinterleave(h, d):
    tm = h.shape[0]
    sub = min(tm, PERM_TILE)
    cs = sub // d
    ri = lax.broadcasted_iota(jnp.int32, (sub, sub), 0)
    ui = lax.broadcasted_iota(jnp.int32, (sub, sub), 1)
    perm = (ui == (ri % cs) * d + ri // cs).astype(BF16)
    slabs = [_dot(perm, h[s0:s0 + sub]).astype(BF16) for s0 in range(0, tm, sub)]
    return jnp.concatenate([sl[r * cs:(r + 1) * cs] for r in range(d) for sl in slabs], axis=0)


def _proj_kernel(x_ref, g_ref, wq_ref, wk_ref, wv_ref, cq_ref, sq_ref, ck_ref, s1_ref, s2_ref, *outs, na, d):
    tm = x_ref.shape[0]
    h = _rms(x_ref[...], g_ref[...]).astype(BF16)
    if d > 1:
        h = _deinterleave(h, d)
    nq = wq_ref.shape[0]
    heads = nq // HEAD_DIM

    qT = _dot_nt(wq_ref[...], h)
    q3 = qT.reshape(heads, HEAD_DIM, tm)
    c = cq_ref[...][None]
    s = sq_ref[...][None]
    x1 = q3[:, 0:ROT_HALF]
    x2 = q3[:, ROT_HALF:2 * ROT_HALF]
    q3 = jnp.concatenate([x1 * c - x2 * s, x2 * c + x1 * s, q3[:, 2 * ROT_HALF:]], axis=1)
    hid = lax.broadcasted_iota(jnp.int32, (heads, 1, 1), 0)
    q3 = q3 * jnp.where(hid < na // HEAD_DIM, SCALE * LOG2E, SCALE)
    qT = q3.reshape(nq, tm).astype(BF16)

    vT = _dot_nt(wv_ref[...], h).astype(BF16)

    kk = _dot(h, wk_ref[...])
    ck, s1, s2 = ck_ref[...], s1_ref[...], s2_ref[...]
    kparts = []
    for gi in range(kk.shape[1] // LANES):
        kg = kk[:, gi * LANES:(gi + 1) * LANES]
        kparts.append(kg * ck + pltpu.roll(kg, LANES - ROT_HALF, 1) * s1 + pltpu.roll(kg, ROT_HALF, 1) * s2)

    if na:
        qa_ref, va_ref, ka_ref, km_ref, qb_ref, vb_ref, kb_ref = outs
        for blk in range(tm // MOBA_BLOCK):
            sl = slice(blk * MOBA_BLOCK, (blk + 1) * MOBA_BLOCK)
            qa_ref[blk] = qT[0:na, sl]
            va_ref[blk] = vT[0:na, sl]
        lane = lax.broadcasted_iota(jnp.int32, (tm, LANES), 1)
        row = lax.broadcasted_iota(jnp.int32, (tm, LANES), 0)
        blkid = pl.program_id(1) * (tm // MOBA_BLOCK) + row // MOBA_BLOCK
        for gi in range(na // LANES):
            kg = kparts[gi]
            for e in range(2):
                in_head = (lane >= HEAD_DIM * e) & (lane < HEAD_DIM * (e + 1))
                onehot = (lane - HEAD_DIM * (1 - e)) == blkid
                col = (2 * gi + e) * LANES
                ka_ref[:, col:col + LANES] = jnp.where(in_head, kg, onehot.astype(F32)).astype(BF16)
            km = kg.reshape(tm // MOBA_BLOCK, MOBA_BLOCK, LANES).sum(axis=1) * (1.0 / MOBA_BLOCK)
            km_ref[:, gi * LANES:(gi + 1) * LANES] = km
    else:
        qb_ref, vb_ref, kb_ref = outs
    per = tm // d
    for r in range(d):
        for cb in range(per // DIL_BAND):
            sl = slice(r * per + cb * DIL_BAND, r * per + (cb + 1) * DIL_BAND)
            qb_ref[r, cb] = qT[na:na + WIDTH_G, sl]
            vb_ref[r, cb] = vT[na:na + WIDTH_G, sl]
        for gi in range(WIDTH_G // LANES):
            kb_ref[r, :, gi * LANES:(gi + 1) * LANES] = kparts[na // LANES + gi][r * per:(r + 1) * per].astype(BF16)


def _project(x, g, wqT, wk, wvT, d, na):
    B, S, D = x.shape
    L = S // d
    tm = PROJ_TM if d == 1 else max(PROJ_TM_DILATED, d * DIL_BAND)
    per = tm // d
    cq, sq, ck, s1, s2 = _rope_tables(S, d, tm)
    grid = (B, S // tm)
    full = lambda a: pl.BlockSpec(a.shape, lambda b, i: (0,) * a.ndim)
    in_specs = [
        pl.BlockSpec((None, tm, D), lambda b, i: (b, i, 0)),
        full(g), full(wqT), full(wk), full(wvT),
        pl.BlockSpec((ROT_HALF, tm), lambda b, i: (0, i)),
        pl.BlockSpec((ROT_HALF, tm), lambda b, i: (0, i)),
        pl.BlockSpec((tm, LANES), lambda b, i: (i, 0)),
        pl.BlockSpec((tm, LANES), lambda b, i: (i, 0)),
        pl.BlockSpec((tm, LANES), lambda b, i: (i, 0)),
    ]
    nsub = per // DIL_BAND
    out_shape = [
        jax.ShapeDtypeStruct((B, d, L // DIL_BAND, WIDTH_G, DIL_BAND), BF16),
        jax.ShapeDtypeStruct((B, d, L // DIL_BAND, WIDTH_G, DIL_BAND), BF16),
        jax.ShapeDtypeStruct((B, d, L, WIDTH_G), BF16),
    ]
    out_specs = [
        pl.BlockSpec((None, d, nsub, WIDTH_G, DIL_BAND), lambda b, i: (b, 0, i, 0, 0)),
        pl.BlockSpec((None, d, nsub, WIDTH_G, DIL_BAND), lambda b, i: (b, 0, i, 0, 0)),
        pl.BlockSpec((None, d, per, WIDTH_G), lambda b, i: (b, 0, i, 0)),
    ]
    if na:
        nblk = tm // MOBA_BLOCK
        out_shape = [
            jax.ShapeDtypeStruct((B, S // MOBA_BLOCK, na, MOBA_BLOCK), BF16),
            jax.ShapeDtypeStruct((B, S // MOBA_BLOCK, na, MOBA_BLOCK), BF16),
            jax.ShapeDtypeStruct((B, S, 2 * na), BF16),
            jax.ShapeDtypeStruct((B, S // tm, nblk, na), F32),
        ] + out_shape
        out_specs = [
            pl.BlockSpec((None, nblk, na, MOBA_BLOCK), lambda b, i: (b, i, 0, 0)),
            pl.BlockSpec((None, nblk, na, MOBA_BLOCK), lambda b, i: (b, i, 0, 0)),
            pl.BlockSpec((None, tm, 2 * na), lambda b, i: (b, i, 0)),
            pl.BlockSpec((None, None, nblk, na), lambda b, i: (b, i, 0, 0)),
        ] + out_specs
    return pl.pallas_call(
        functools.partial(_proj_kernel, na=na, d=d),
        grid=grid, in_specs=in_specs, out_specs=out_specs, out_shape=out_shape,
        compiler_params=_cparams(("arbitrary",) * 2),
        name=f"proj_d{d}",
    )(x, g, wqT, wk, wvT, cq, sq, ck, s1, s2)


def _moba_kernel(q_ref, k_ref, v_ref, km_ref, o_ref, qa_scr, s_scr, p_scr):
    g = pl.program_id(2)
    nb = km_ref.shape[0]
    blk = MOBA_BLOCK
    tq = MOBA_QBLOCKS * blk
    q2 = jnp.concatenate([q_ref[i] for i in range(MOBA_QBLOCKS)], axis=1)
    bid = lax.broadcasted_iota(jnp.int32, (nb, tq), 0)
    qblk = MOBA_QBLOCKS * g + lax.broadcasted_iota(jnp.int32, (nb, tq), 1) // blk
    key_i = lax.broadcasted_iota(jnp.int32, (blk, tq), 0)
    qry_i = lax.broadcasted_iota(jnp.int32, (blk, tq), 1)
    km = km_ref[...].astype(BF16)
    zq = jnp.zeros((HEAD_DIM, tq), BF16)
    zb = jnp.zeros((HEAD_DIM - nb, tq), BF16)

    for h in range(2):
        hl = slice(LANES * h, LANES * (h + 1))
        qh = q2[HEAD_DIM * h:HEAD_DIM * (h + 1)]
        q_plain = jnp.concatenate([qh, zq] if h == 0 else [zq, qh], axis=0)
        gt = jnp.where(bid < qblk, _dot(km, q_plain), -jnp.inf)
        sel = jnp.zeros(gt.shape, jnp.bool_)
        for _ in range(MOBA_TOPK):
            mx = jnp.max(gt, axis=0, keepdims=True)
            first = jnp.min(jnp.where((gt == mx) & (mx > -jnp.inf), bid, nb), axis=0, keepdims=True)
            pick = bid == first
            sel = sel | pick
            gt = jnp.where(pick, -jnp.inf, gt)
        bias = jnp.where(sel, 0.0, NEG).astype(BF16)
        qa_scr[h] = jnp.concatenate([qh, bias, zb] if h == 0 else [bias, zb, qh], axis=0)
        bias_d = jnp.where(sel | (bid == qblk), 0.0, NEG).astype(BF16)
        q_diag = jnp.concatenate([qh, bias_d, zb] if h == 0 else [bias_d, zb, qh], axis=0)
        for i in range(MOBA_QBLOCKS):
            j = MOBA_QBLOCKS * g + i
            sd = _dot(k_ref[pl.ds(pl.multiple_of(j * blk, blk), blk), hl], q_diag)
            own = (qry_i >= i * blk) & (qry_i < (i + 1) * blk)
            s_scr[h, i] = jnp.where(own & (key_i > qry_i - i * blk), NEG, sd)

    def score_blocks(c):
        return [jnp.where(c < g, MOBA_QBLOCKS * c + i, nb - 1) for i in range(MOBA_QBLOCKS)]

    def value_blocks(it):
        return [jnp.where(it <= 1, MOBA_QBLOCKS * g + i, MOBA_QBLOCKS * (it - 2) + i) for i in range(MOBA_QBLOCKS)]

    def values(it, h, alpha, acc):
        vl = slice(HEAD_DIM * h, HEAD_DIM * (h + 1))
        acc = alpha * acc
        for i, j in enumerate(value_blocks(it)):
            acc = acc + _dot(v_ref[j, vl, :], p_scr[h, i])
        return acc

    p_scr[...] = jnp.zeros(p_scr.shape, BF16)

    def body(it, carry):
        alphas, ms, ls, accs = carry
        a_new, m_new, l_new, acc_new = [], [], [], []
        for h in range(2):
            hl = slice(LANES * h, LANES * (h + 1))
            qa = qa_scr[h]
            s_next = [_dot(k_ref[pl.ds(pl.multiple_of(j * blk, blk), blk), hl], qa) for j in score_blocks(it)]
            s_cur = [s_scr[h, i] for i in range(MOBA_QBLOCKS)]
            acc_new.append(values(it, h, alphas[h], accs[h]))
            mn = ms[h]
            for sc in s_cur:
                mn = jnp.maximum(mn, jnp.max(sc, axis=0, keepdims=True))
            alpha = jnp.exp2(ms[h] - mn)
            l = alpha * ls[h]
            for i in range(MOBA_QBLOCKS):
                p = jnp.exp2(s_cur[i] - mn)
                l = l + jnp.sum(p, axis=0, keepdims=True)
                s_scr[h, i] = s_next[i]
                p_scr[h, i] = p.astype(BF16)
            m_new.append(mn)
            l_new.append(l)
            a_new.append(alpha)
        return (tuple(a_new), tuple(m_new), tuple(l_new), tuple(acc_new))

    one = jnp.ones((1, tq), F32)
    neg = jnp.full((1, tq), NEG, F32)
    zl = jnp.zeros((1, tq), F32)
    za = jnp.zeros((HEAD_DIM, tq), F32)
    n_it = g + 1
    n_main = n_it // MOBA_UNROLL

    def trip(t, c):
        for u in range(MOBA_UNROLL):
            c = body(MOBA_UNROLL * t + u, c)
        return c

    carry = lax.fori_loop(0, n_main, trip, ((one, one), (neg, neg), (zl, zl), (za, za)))
    alphas, _, ls, accs = lax.fori_loop(MOBA_UNROLL * n_main, n_it, body, carry)
    accs = [values(n_it, h, alphas[h], accs[h]) for h in range(2)]
    oT = jnp.concatenate([accs[0] / ls[0], accs[1] / ls[1]], axis=0)
    o_ref[...] = oT.T.astype(o_ref.dtype)


def _moba(qT, k, vT, kmean):
    B, nb, wa, blk = qT.shape
    S = nb * blk
    hp = wa // LANES
    nq = MOBA_QBLOCKS
    assert nb % 16 == 0 and nb <= HEAD_DIM, "block-mask rows must fit the spare half of a head pair"
    return pl.pallas_call(
        _moba_kernel,
        grid=(B, hp, nb // nq),
        in_specs=[
            pl.BlockSpec((None, nq, LANES, blk), lambda b, p, n: (b, n, p, 0)),
            pl.BlockSpec((None, S, 2 * LANES), lambda b, p, n: (b, 0, p)),
            pl.BlockSpec((None, nb, LANES, blk), lambda b, p, n: (b, 0, p, 0)),
            pl.BlockSpec((None, nb, LANES), lambda b, p, n: (b, 0, p)),
        ],
        out_specs=pl.BlockSpec((None, nq * blk, LANES), lambda b, p, n: (b, n, p)),
        out_shape=jax.ShapeDtypeStruct((B, S, wa), BF16),
        scratch_shapes=[pltpu.VMEM((2, LANES, nq * blk), BF16), pltpu.VMEM((2, nq, blk, nq * blk), F32),
                        pltpu.VMEM((2, nq, blk, nq * blk), BF16)],
        compiler_params=_cparams(("arbitrary",) * 3),
        name="moba_attn",
    )(qT, k, vT, kmean)


def _dil_kernel(q_ref, k_ref, kp_ref, v_ref, vp_ref, o_ref, lse_ref):
    t = pl.program_id(2)
    band = DIL_BAND
    nres, nsub = q_ref.shape[0], q_ref.shape[1]
    key_i = lax.broadcasted_iota(jnp.int32, (band, band), 0)
    qry_i = lax.broadcasted_iota(jnp.int32, (band, band), 1)
    rowid = lax.broadcasted_iota(jnp.int32, (LANES, band), 0)
    own_ok = key_i <= qry_i
    prev_ok = key_i >= qry_i
    for r in range(nres):
        for c in range(nsub):
            o_parts, l_parts = [], []
            for hp in range(WIDTH_G // LANES):
                cols = slice(hp * LANES, (hp + 1) * LANES)
                k_own = k_ref[r, c * band:(c + 1) * band, cols]
                k_prev = kp_ref[r, :, cols] if c == 0 else k_ref[r, (c - 1) * band:c * band, cols]
                q2 = q_ref[r, c, cols, :]
                for h in range(2):
                    qh = jnp.where((rowid >= HEAD_DIM * h) & (rowid < HEAD_DIM * (h + 1)), q2, jnp.zeros_like(q2))
                    pmask = (prev_ok & (t > 0)) if c == 0 else prev_ok
                    s_own = jnp.where(own_ok, _dot(k_own, qh), -jnp.inf)
                    s_prev = jnp.where(pmask, _dot(k_prev, qh), -jnp.inf)
                    m = jnp.maximum(jnp.max(s_own, axis=0, keepdims=True), jnp.max(s_prev, axis=0, keepdims=True))
                    p_own = jnp.exp(s_own - m)
                    p_prev = jnp.exp(s_prev - m)
                    l = jnp.sum(p_own, axis=0, keepdims=True) + jnp.sum(p_prev, axis=0, keepdims=True)
                    rows = slice(hp * LANES + h * HEAD_DIM, hp * LANES + (h + 1) * HEAD_DIM)
                    v_own = v_ref[r, c, rows, :]
                    v_prev = vp_ref[r, rows, :] if c == 0 else v_ref[r, c - 1, rows, :]
                    oT = _dot(v_own, p_own.astype(BF16)) + _dot(v_prev, p_prev.astype(BF16))
                    o_parts.append(oT / l)
                    l_parts.append(jnp.broadcast_to(m + jnp.log(l), (HEAD_DIM, band)))
            o_ref[r, c * band:(c + 1) * band, :] = jnp.concatenate(o_parts, axis=0).T
            lse_ref[r, c * band:(c + 1) * band, :] = jnp.concatenate(l_parts, axis=0).T


def _dilated(qT, k, vT):
    B, d, nblk, wg, band = qT.shape
    L = nblk * band
    tl = min(DIL_TL, L)
    nsub = tl // band
    nres = min(d, max(1, DIL_TL // L))
    prev = lambda t: jnp.maximum(t * nsub - 1, 0)
    return pl.pallas_call(
        _dil_kernel,
        grid=(B, d // nres, L // tl),
        in_specs=[
            pl.BlockSpec((None, nres, nsub, wg, band), lambda b, r, t: (b, r, t, 0, 0)),
            pl.BlockSpec((None, nres, tl, wg), lambda b, r, t: (b, r, t, 0)),
            pl.BlockSpec((None, nres, band, wg), lambda b, r, t: (b, r, prev(t), 0)),
            pl.BlockSpec((None, nres, nsub, wg, band), lambda b, r, t: (b, r, t, 0, 0)),
            pl.BlockSpec((None, nres, None, wg, band), lambda b, r, t: (b, r, prev(t), 0, 0)),
        ],
        out_specs=[
            pl.BlockSpec((None, nres, tl, wg), lambda b, r, t: (b, r, t, 0)),
            pl.BlockSpec((None, nres, tl, wg), lambda b, r, t: (b, r, t, 0)),
        ],
        out_shape=[jax.ShapeDtypeStruct((B, d, L, wg), F32)] * 2,
        compiler_params=_cparams(("arbitrary",) * 3),
        name=f"dilated_d{d}",
    )(qT, k, k, vT, vT)


def _interleave(ref, scr):
    d, per, w = ref.shape
    if d == 1:
        return ref[0]
    for r in range(d):
        for sl in range(w // LANES):
            scr[sl, pl.ds(r, per, stride=d), :] = ref[r, :, sl * LANES:(sl + 1) * LANES]
    return jnp.concatenate([scr[sl] for sl in range(w // LANES)], axis=1)


def _post_kernel(x_ref, oa_ref, o1_ref, o2_ref, o3_ref, l1_ref, l2_ref, l3_ref, gmix_ref, wg_ref, bga_ref,
                 bgb_ref, woa_ref, wob_ref, wout_ref, gffn_ref, wr_ref, br_ref,
                 x1_ref, h2_ref, idx_ref, rank_ref, pnat_ref, cnt_ref, carry_scr, il_scr):
    i = pl.program_id(0)
    tm, D = x_ref.shape
    ne = wr_ref.shape[0]

    @pl.when(i == 0)
    def _():
        carry_scr[...] = jnp.zeros_like(carry_scr)

    x = x_ref[...]
    h = _rms(x, gmix_ref[...]).astype(BF16)
    gates = _dot(h, wg_ref[...])
    ga = gates[:, :D] + bga_ref[...]
    gb = gates[:, D:] + bgb_ref[...]
    l1, l2, l3 = [_interleave(r, il_scr.at[n]) for n, r in enumerate((l1_ref, l2_ref, l3_ref))]
    o1, o2, o3 = [_interleave(r, il_scr.at[3 + n]) for n, r in enumerate((o1_ref, o2_ref, o3_ref))]
    mx = jnp.maximum(jnp.maximum(l1, l2), l3)
    e1, e2, e3 = jnp.exp(l1 - mx), jnp.exp(l2 - mx), jnp.exp(l3 - mx)
    ob = (e1 * o1 + e2 * o2 + e3 * o3) / (e1 + e2 + e3)
    ya = _dot(oa_ref[...], woa_ref[...])
    yb = _dot(ob.astype(BF16), wob_ref[...])
    mix = jax.nn.sigmoid(ga) * ya + jax.nn.sigmoid(gb) * yb
    x1 = x + _dot(mix.astype(BF16), wout_ref[...])
    x1_ref[...] = x1
    h2 = _rms(x1, gffn_ref[...])
    _rows_to_tiles(h2_ref, h2)

    wr = wr_ref[...]
    wr_hi = wr.astype(BF16)
    wr_lo = (wr - wr_hi.astype(F32)).astype(BF16)
    h2_hi = h2.astype(BF16)
    h2_lo = (h2 - h2_hi.astype(F32)).astype(BF16)
    logits = _dot_nt(wr_hi, h2_hi) + (_dot_nt(wr_hi, h2_lo) + _dot_nt(wr_lo, h2_hi)) + br_ref[...]
    eid = lax.broadcasted_iota(jnp.int32, (ne, tm), 0)
    g = logits
    vals, idxs, picks = [], [], []
    for _ in range(TOP_K):
        m = jnp.max(g, axis=0, keepdims=True)
        first = jnp.min(jnp.where(g == m, eid, ne), axis=0, keepdims=True)
        pick = eid == first
        vals.append(m)
        idxs.append(first)
        picks.append(pick)
        g = jnp.where(pick, -jnp.inf, g)
    es = [jnp.exp(v - vals[0]) for v in vals]
    den = es[0] + es[1] + es[2] + es[3]
    probs = [e / den for e in es]

    onehot = jnp.zeros((ne, tm), F32)
    for pick in picks:
        onehot = onehot + pick.astype(F32)
    earlier = (lax.broadcasted_iota(jnp.int32, (tm, tm), 0) < lax.broadcasted_iota(jnp.int32, (tm, tm), 1))
    prefix = _dot(onehot.astype(BF16), earlier.astype(BF16)) + carry_scr[:, 0:1]
    ranks = [jnp.sum(jnp.where(pick, prefix, 0.0), axis=0, keepdims=True) for pick in picks]
    carry_scr[...] = carry_scr[...] + jnp.sum(onehot, axis=1, keepdims=True)
    cnt_ref[...] = carry_scr[...]

    zi = jnp.zeros((8 - TOP_K, tm), jnp.int32)
    idx_ref[...] = jnp.concatenate(idxs + [zi], axis=0)
    rank_ref[...] = jnp.concatenate([r.astype(jnp.int32) for r in ranks] + [zi], axis=0)
    pnat_ref[...] = jnp.concatenate(probs + [jnp.zeros((LANES - TOP_K, tm), F32)], axis=0).T


def _post(x2, oa, obs, lses, gmix, wg, bga, bgb, woa, wob, wout, gffn, wrT, br):
    T, D = x2.shape
    tm = POST_TM
    ne = wrT.shape[0]
    row = lambda w: pl.BlockSpec((tm, w), lambda i: (i, 0))

    def grouped(a):
        _, d, L, w = a.shape
        nt = L * d // tm
        return pl.BlockSpec((None, d, tm // d, w), lambda i: (i // nt, 0, i % nt, 0))

    full = lambda a: pl.BlockSpec(a.shape, lambda i: (0,) * a.ndim)
    col = pl.BlockSpec((8, tm), lambda i: (0, i))
    return pl.pallas_call(
        _post_kernel,
        grid=(T // tm,),
        in_specs=[row(D), row(WIDTH_A)] + [grouped(a) for a in (*obs, *lses)]
                 + [full(a) for a in (gmix, wg, bga, bgb, woa, wob, wout, gffn, wrT, br)],
        out_specs=[row(D), pl.BlockSpec((tm * SUBLANES, LANES), lambda i: (i, 0)), col, col, row(LANES),
                   pl.BlockSpec((ne, LANES), lambda i: (0, 0))],
        out_shape=[
            jax.ShapeDtypeStruct((T, D), F32), jax.ShapeDtypeStruct((T * SUBLANES, LANES), F32),
            jax.ShapeDtypeStruct((8, T), jnp.int32), jax.ShapeDtypeStruct((8, T), jnp.int32),
            jax.ShapeDtypeStruct((T, LANES), F32),
            jax.ShapeDtypeStruct((ne, LANES), F32),
        ],
        scratch_shapes=[pltpu.VMEM((ne, LANES), F32), pltpu.VMEM((6, WIDTH_G // LANES, tm, LANES), F32)],
        compiler_params=_cparams(("arbitrary",)),
        name="post_mix_router",
    )(x2, oa, *obs, *lses, gmix, wg, bga, bgb, woa, wob, wout, gffn, wrT, br)


def _dispatch_kernel(pend_ref, padded_ref, dest_ref, h_ref, xr_ref, zero_scr, sem):
    i = pl.program_id(0)
    tm = h_ref.shape[0] // SUBLANES
    br = zero_scr.shape[0]

    def zero_copy(blk_start):
        return pltpu.make_async_copy(zero_scr, xr_ref.at[pl.ds(pl.multiple_of(blk_start * SUBLANES, br), br), :], sem)

    @pl.when(i == 0)
    def _():
        zero_scr[...] = jnp.zeros_like(zero_scr)
        n_used = pend_ref[N_EXPERTS - 1] // EXPERT_ROWS
        n_blk = xr_ref.shape[0] // br

        def tail(fn):
            def go(e, c):
                @pl.when(padded_ref[e] > 0)
                def _():
                    fn(zero_copy(pend_ref[e] - EXPERT_ROWS))
                return c
            return go

        def unused(fn):
            def go(b, c):
                fn(zero_copy(b * EXPERT_ROWS))
                return c
            return go

        lax.fori_loop(0, N_EXPERTS, tail(lambda cp: cp.start()), 0)
        lax.fori_loop(n_used, n_blk, unused(lambda cp: cp.start()), 0)
        lax.fori_loop(0, N_EXPERTS, tail(lambda cp: cp.wait()), 0)
        lax.fori_loop(n_used, n_blk, unused(lambda cp: cp.wait()), 0)

    def row_copy(t, k):
        src = h_ref.at[pl.ds(pl.multiple_of(t * SUBLANES, SUBLANES), SUBLANES), :]
        dst = xr_ref.at[pl.ds(pl.multiple_of(dest_ref[k, t] * SUBLANES, SUBLANES), SUBLANES), :]
        return pltpu.make_async_copy(src, dst, sem)

    def start(tb, c):
        for u in range(ROW_UNROLL):
            for k in range(TOP_K):
                row_copy(tb * ROW_UNROLL + u, k).start(priority=(u * TOP_K + k) % 2)
        return c

    def wait(tb, c):
        for u in range(ROW_UNROLL):
            for k in range(TOP_K):
                row_copy(tb * ROW_UNROLL + u, k).wait()
        return c

    lax.fori_loop(0, tm // ROW_UNROLL, start, 0)
    lax.fori_loop(0, tm // ROW_UNROLL, wait, 0)


def _dispatch(pend, padded, dest3, h2t, n_rows):
    tm = DISPATCH_TM
    return pl.pallas_call(
        _dispatch_kernel,
        grid_spec=pltpu.PrefetchScalarGridSpec(
            num_scalar_prefetch=2,
            grid=(h2t.shape[0] // (tm * SUBLANES),),
            in_specs=[
                pl.BlockSpec((None, TOP_K, tm), lambda i, pe, pa: (i, 0, 0), memory_space=pltpu.SMEM),
                pl.BlockSpec((tm * SUBLANES, LANES), lambda i, pe, pa: (i, 0)),
            ],
            out_specs=pl.BlockSpec(memory_space=pl.ANY),
            scratch_shapes=[pltpu.VMEM((EXPERT_ROWS * SUBLANES, LANES), F32), pltpu.SemaphoreType.DMA(())],
        ),
        out_shape=jax.ShapeDtypeStruct((n_rows * SUBLANES, LANES), F32),
        compiler_params=_cparams(("arbitrary",)),
        name="moe_dispatch",
    )(pend, padded, dest3, h2t)


def _expert_kernel(pstart_ref, padded_ref, x_hbm, wg_ref, bg_ref, wu_ref, bu_ref, wd_ref, bd_ref, y_hbm,
                   w_scr, xbuf, ybuf, xsem, ysem):
    e = pl.program_id(0)
    tb = EXPERT_ROWS * SUBLANES
    nblk = padded_ref[e] // EXPERT_ROWS
    npair = nblk // EXPERT_PAIR
    row0 = pstart_ref[e]

    def rows(first_blk, nb_):
        return pl.ds(pl.multiple_of((row0 + first_blk * EXPERT_ROWS) * SUBLANES, tb), nb_ * tb)

    def x_copy(first_blk, nb_, slot):
        return pltpu.make_async_copy(x_hbm.at[rows(first_blk, nb_), :], xbuf.at[slot, pl.ds(0, nb_ * tb), :],
                                     xsem.at[slot])

    def y_copy(first_blk, nb_, slot):
        return pltpu.make_async_copy(ybuf.at[slot, pl.ds(0, nb_ * tb), :], y_hbm.at[rows(first_blk, nb_), :],
                                     ysem.at[slot])

    def ffn(slot, nb_):
        n = nb_ * EXPERT_ROWS
        x = _tiles_to_rows(xbuf.at[slot, pl.ds(0, nb_ * tb), :], n).astype(BF16)
        g = _dot(x, w_scr[0]) + bg_ref[...]
        u = _dot(x, w_scr[1]) + bu_ref[...]
        g = jnp.minimum(g, SWIGLU_LIMIT)
        u = jnp.clip(u, -SWIGLU_LIMIT, SWIGLU_LIMIT)
        a = g * jax.nn.sigmoid(SWIGLU_ALPHA * g) * (u + 1.0)
        _rows_to_tiles(ybuf.at[slot, pl.ds(0, nb_ * tb), :], _dot(a.astype(BF16), w_scr[2]) + bd_ref[...])

    @pl.when(nblk > 0)
    def _():
        has_tail = nblk > npair * EXPERT_PAIR
        nstep = npair + has_tail.astype(jnp.int32)

        def start_x(s, slot):
            @pl.when(s < npair)
            def _():
                x_copy(s * EXPERT_PAIR, EXPERT_PAIR, slot).start()

            @pl.when((s == npair) & has_tail)
            def _():
                x_copy(s * EXPERT_PAIR, 1, slot).start()

        def wait_y(s, slot):
            @pl.when(s < npair)
            def _():
                y_copy(s * EXPERT_PAIR, EXPERT_PAIR, slot).wait()

            @pl.when((s == npair) & has_tail)
            def _():
                y_copy(s * EXPERT_PAIR, 1, slot).wait()

        start_x(0, 0)
        w_scr[0] = wg_ref[...].astype(BF16)
        w_scr[1] = wu_ref[...].astype(BF16)
        w_scr[2] = wd_ref[...].astype(BF16)

        def body(s, c):
            slot = s % 2
            start_x(s + 1, 1 - slot)
            x_copy(s * EXPERT_PAIR, EXPERT_PAIR, slot).wait()

            @pl.when(s >= 2)
            def _():
                y_copy((s - 2) * EXPERT_PAIR, EXPERT_PAIR, slot).wait()

            ffn(slot, EXPERT_PAIR)
            y_copy(s * EXPERT_PAIR, EXPERT_PAIR, slot).start()
            return c

        lax.fori_loop(0, npair, body, 0)

        @pl.when(has_tail)
        def _():
            slot = npair % 2
            x_copy(npair * EXPERT_PAIR, 1, slot).wait()

            @pl.when(npair >= 2)
            def _():
                y_copy((npair - 2) * EXPERT_PAIR, EXPERT_PAIR, slot).wait()

            ffn(slot, 1)
            y_copy(npair * EXPERT_PAIR, 1, slot).start()

        @pl.when(nstep >= 2)
        def _():
            wait_y(nstep - 2, nstep % 2)

        wait_y(nstep - 1, (nstep - 1) % 2)

    @pl.when(e == pl.num_programs(0) - 1)
    def _():
        n_used = (pstart_ref[e] + padded_ref[e]) // EXPERT_ROWS
        n_all = y_hbm.shape[0] // tb
        ybuf[0, pl.ds(0, tb), :] = jnp.zeros((tb, LANES), F32)

        def zero_copy(b):
            return pltpu.make_async_copy(ybuf.at[0, pl.ds(0, tb), :],
                                         y_hbm.at[pl.ds(pl.multiple_of(b * tb, tb), tb), :], ysem.at[0])

        lax.fori_loop(n_used, n_all, lambda b, c: (zero_copy(b).start(), c)[1], 0)
        lax.fori_loop(n_used, n_all, lambda b, c: (zero_copy(b).wait(), c)[1], 0)


def _experts(pstart, padded, x_rows, w_gate, b_gate, w_up, b_up, w_down, b_down):
    E, D, F = w_gate.shape
    tb = EXPERT_PAIR * EXPERT_ROWS * SUBLANES
    wspec = lambda shape: pl.BlockSpec((None,) + shape, lambda e, ps, pa: (e, 0, 0))
    return pl.pallas_call(
        _expert_kernel,
        grid_spec=pltpu.PrefetchScalarGridSpec(
            num_scalar_prefetch=2,
            grid=(E,),
            in_specs=[
                pl.BlockSpec(memory_space=pl.ANY),
                wspec((D, F)), wspec((1, F)), wspec((D, F)), wspec((1, F)), wspec((F, D)), wspec((1, D)),
            ],
            out_specs=pl.BlockSpec(memory_space=pl.ANY),
            scratch_shapes=[pltpu.VMEM((3, D, F), BF16), pltpu.VMEM((2, tb, LANES), F32),
                            pltpu.VMEM((2, tb, LANES), F32), pltpu.SemaphoreType.DMA((2,)),
                            pltpu.SemaphoreType.DMA((2,))],
        ),
        out_shape=jax.ShapeDtypeStruct(x_rows.shape, F32),
        compiler_params=_cparams(("arbitrary",)),
        name="moe_experts",
    )(pstart, padded, x_rows, w_gate, b_gate.reshape(E, 1, F), w_up, b_up.reshape(E, 1, F),
      w_down, b_down.reshape(E, 1, D))


def _combine_kernel(dest_ref, dnext_ref, y_ref, x1_ref, p_ref, g_ref, o_ref, ybuf, sem):
    i = pl.program_id(0)
    tm = x1_ref.shape[0]
    slot = i % 2

    def row_copy(d_ref, sl, t, k):
        src = y_ref.at[pl.ds(pl.multiple_of(d_ref[k, t] * SUBLANES, SUBLANES), SUBLANES), :]
        dst = ybuf.at[sl, k, pl.ds(pl.multiple_of(t * SUBLANES, SUBLANES), SUBLANES), :]
        return pltpu.make_async_copy(src, dst, sem.at[sl])

    def request(d_ref, sl):
        def go(tb, c):
            for u in range(ROW_UNROLL):
                for k in range(TOP_K):
                    row_copy(d_ref, sl, tb * ROW_UNROLL + u, k).start(priority=(u * TOP_K + k) % 2)
            return c
        lax.fori_loop(0, tm // ROW_UNROLL, go, 0)

    @pl.when(i == 0)
    def _():
        request(dest_ref, 0)

    @pl.when(i + 1 < pl.num_programs(0))
    def _():
        request(dnext_ref, 1 - slot)

    def wait(tb, c):
        for u in range(ROW_UNROLL):
            for k in range(TOP_K):
                row_copy(dest_ref, slot, tb * ROW_UNROLL + u, k).wait()
        return c

    lax.fori_loop(0, tm // ROW_UNROLL, wait, 0)
    p = p_ref[...]
    y = x1_ref[...]
    for k in range(TOP_K):
        y = y + p[:, k:k + 1] * _tiles_to_rows(ybuf.at[slot, k], tm)
    o_ref[...] = _rms(y, g_ref[...])


def _combine(dest3, y_rows, x1, pnat, g_final):
    T, D = x1.shape
    tm = COMBINE_TM
    nt = T // tm
    return pl.pallas_call(
        _combine_kernel,
        grid=(nt,),
        in_specs=[
            pl.BlockSpec((None, TOP_K, tm), lambda i: (i, 0, 0), memory_space=pltpu.SMEM),
            pl.BlockSpec((None, TOP_K, tm), lambda i: (jnp.minimum(i + 1, nt - 1), 0, 0), memory_space=pltpu.SMEM),
            pl.BlockSpec(memory_space=pl.ANY),
            pl.BlockSpec((tm, D), lambda i: (i, 0)),
            pl.BlockSpec((tm, LANES), lambda i: (i, 0)),
            pl.BlockSpec((1, D), lambda i: (0, 0)),
        ],
        out_specs=pl.BlockSpec((tm, D), lambda i: (i, 0)),
        out_shape=jax.ShapeDtypeStruct((T, D), F32),
        scratch_shapes=[pltpu.VMEM((2, TOP_K, tm * SUBLANES, LANES), F32), pltpu.SemaphoreType.DMA((2,))],
        compiler_params=_cparams(("arbitrary",)),
        name="moe_combine",
    )(dest3, dest3, y_rows, x1, pnat, g_final)


def _rope_tables(S, d, tm):
    per = tm // d

    def cos_sin(shape, ax, f):
        rho = lax.broadcasted_iota(jnp.int32, shape, ax)
        w = rho % tm
        pos = ((rho // tm) * per + w % per) * d + w // per
        inv = jnp.float32(ROPE_THETA) ** (-f.astype(F32) / ROT_HALF)
        ang = pos.astype(F32) * inv
        return jnp.cos(ang), jnp.sin(ang)

    sh = (ROT_HALF, S)
    cq, sq = cos_sin(sh, 1, lax.broadcasted_iota(jnp.int32, sh, 0))
    sh = (S, ROT_HALF)
    cos, sin = cos_sin(sh, 0, lax.broadcasted_iota(jnp.int32, sh, 1))
    lane = lax.broadcasted_iota(jnp.int32, (ROT_HALF, LANES), 1) % HEAD_DIM
    f = lax.broadcasted_iota(jnp.int32, (ROT_HALF, LANES), 0)
    lo = ((lane == f)).astype(F32)
    hi = ((lane == f + ROT_HALF)).astype(F32)
    spread = lambda a, m: jnp.dot(a, m, precision=lax.Precision.HIGHEST)
    rest = (lax.broadcasted_iota(jnp.int32, (1, LANES), 1) % HEAD_DIM >= 2 * ROT_HALF).astype(F32)
    ck = spread(cos, lo + hi) + rest
    s1 = spread(-sin, lo)
    s2 = spread(sin, hi)
    return cq, sq, ck, s1, s2


def _layer(x, ln_mix_g, w_in, b_gate_a, b_gate_b, w_o_a, w_o_b, w_out, ln_ffn_g, w_router, b_router,
           w_gate, b_gate, w_up, b_up, w_down, b_down, ln_out_g):
    B, S, D = x.shape
    T = B * S
    qa0, ka0, va0 = 0, WIDTH_A, 2 * WIDTH_A
    qb0, kb0, vb0 = 3 * WIDTH_A, 3 * WIDTH_A + WIDTH_B, 3 * WIDTH_A + 2 * WIDTH_B
    g0 = 3 * WIDTH_A + 3 * WIDTH_B
    cols = lambda s, w: w_in[:, s:s + w]
    gmix = ln_mix_g.reshape(1, D)

    obs, lses = [], []
    o_a = None
    for gi, (_, d) in enumerate(DIL_PAIRS):
        off = gi * WIDTH_G
        wq, wk, wv = cols(qb0 + off, WIDTH_G), cols(kb0 + off, WIDTH_G), cols(vb0 + off, WIDTH_G)
        na = 0
        if gi == 0:
            na = WIDTH_A
            wq = jnp.concatenate([cols(qa0, WIDTH_A), wq], axis=1)
            wk = jnp.concatenate([cols(ka0, WIDTH_A), wk], axis=1)
            wv = jnp.concatenate([cols(va0, WIDTH_A), wv], axis=1)
        outs = _project(x, gmix, wq.T.astype(BF16), wk.astype(BF16), wv.T.astype(BF16), d, na)
        if gi == 0:
            qTa, vTa, ka, kmean = outs[:4]
            outs = outs[4:]
            o_a = _moba(qTa, ka, vTa, kmean.reshape(B, S // MOBA_BLOCK, WIDTH_A))
        qTb, vTb, kb = outs
        o_g, lse_g = _dilated(qTb, kb, vTb)
        obs.append(o_g)
        lses.append(lse_g)

    x1, h2, idxT, rankT, pnat, cnt = _post(
        x.reshape(T, D), o_a.reshape(T, WIDTH_A), obs, lses, gmix,
        cols(g0, 2 * D).astype(BF16), b_gate_a.reshape(1, D), b_gate_b.reshape(1, D),
        w_o_a.astype(BF16), w_o_b.astype(BF16), w_out.astype(BF16), ln_ffn_g.reshape(1, D),
        w_router.T, b_router.reshape(N_EXPERTS, 1))

    br = EXPERT_ROWS
    counts = cnt[:, 0].astype(jnp.int32)
    padded = (counts + br - 1) // br * br
    pend = jnp.cumsum(padded)
    pstart = pend - padded
    eids = jnp.arange(N_EXPERTS, dtype=jnp.int32)[:, None, None]
    dest = jnp.sum(jnp.where(idxT[None, :TOP_K] == eids, pstart[:, None, None], 0), axis=0) + rankT[:TOP_K]
    n_rows = T * TOP_K + N_EXPERTS * br

    dest_d = dest.reshape(TOP_K, T // DISPATCH_TM, DISPATCH_TM).transpose(1, 0, 2)
    x_rows = _dispatch(pend.astype(jnp.int32), padded.astype(jnp.int32), dest_d, h2, n_rows)
    y_rows = _experts(pstart.astype(jnp.int32), padded.astype(jnp.int32), x_rows, w_gate, b_gate, w_up, b_up,
                      w_down, b_down)
    dest_c = dest.reshape(TOP_K, T // COMBINE_TM, COMBINE_TM).transpose(1, 0, 2)
    out = _combine(dest_c, y_rows, x1, pnat, ln_out_g.reshape(1, D))
    return out.reshape(B, S, D)


def kernel(x, ln_mix_g, w_in, b_gate_a, b_gate_b, w_o_a, w_o_b, w_out, ln_ffn_g, w_router, b_router,
           w_gate, b_gate, w_up, b_up, w_down, b_down, ln_final_g):
    depth = ln_mix_g.shape[0]
    assert depth == 1, "the final RMSNorm is fused into the last layer's combine"
    return _layer(x, ln_mix_g[0], w_in[0], b_gate_a[0], b_gate_b[0], w_o_a[0], w_o_b[0], w_out[0],
                  ln_ffn_g[0], w_router[0], b_router[0], w_gate[0], b_gate[0], w_up[0], b_up[0],
                  w_down[0], b_down[0], ln_final_g)
```

```python
import functools

import jax
import jax.numpy as jnp
from jax import lax
from jax.experimental import pallas as pl
from jax.experimental.pallas import tpu as pltpu

D_MODEL = 1024
HEAD_DIM = 64
ROT_HALF = HEAD_DIM // 8
ROPE_THETA = 500000.0
N_HEADS_A = 8
MOBA_BLOCK = 256
MOBA_TOPK = 3
MOBA_QBLOCKS = 2
MOBA_UNROLL = 4
DIL_PAIRS = ((128, 1), (512, 4), (2048, 16))
DIL_BAND = 128
HEADS_PER_GROUP_B = 4
WIDTH_A = N_HEADS_A * HEAD_DIM
WIDTH_G = HEADS_PER_GROUP_B * HEAD_DIM
WIDTH_B = WIDTH_G * len(DIL_PAIRS)
N_EXPERTS = 32
TOP_K = 4
SWIGLU_LIMIT = 7.0
SWIGLU_ALPHA = 1.702
NORM_EPS = 1e-5
SCALE = HEAD_DIM ** -0.5
LOG2E = 1.4426950408889634

LANES = 128
PROJ_TM = 1024
PROJ_TM_DILATED = 2048
PERM_TILE = 512
DIL_TL = 2048
POST_TM = 512
EXPERT_ROWS = 256
EXPERT_PAIR = 2
TILE_DMA_PRIORITY = 1
DISPATCH_TM = 512
COMBINE_TM = 256
SUBLANES = 8
ROW_UNROLL = 8
VMEM_LIMIT = 56 * 1024 * 1024
NEG = -1e30

BF16 = jnp.bfloat16
F32 = jnp.float32


def _dot(a, b):
    return jnp.dot(a, b, preferred_element_type=F32)


def _dot_nt(a, b):
    return lax.dot_general(a, b, (((1,), (1,)), ((), ())), preferred_element_type=F32)


def _rms(x, g):
    ms = jnp.mean(x * x, axis=-1, keepdims=True)
    return x * lax.rsqrt(ms + NORM_EPS) * g


def _rows_to_tiles(ref, val):
    n = val.shape[0]
    for c in range(val.shape[1] // LANES):
        ref[pl.ds(c, n, stride=SUBLANES), :] = val[:, c * LANES:(c + 1) * LANES]


def _tiles_to_rows(ref, n):
    return jnp.concatenate([ref[pl.ds(c, n, stride=SUBLANES), :] for c in range(SUBLANES)], axis=1)


def _cparams(sem):
    return pltpu.CompilerParams(dimension_semantics=sem, vmem_limit_bytes=VMEM_LIMIT)


def _deinterleave(h, d):
    tm = h.shape[0]
    sub = min(tm, PERM_TILE)
    cs = sub // d
    ri = lax.broadcasted_iota(jnp.int32, (sub, sub), 0)
    ui = lax.broadcasted_iota(jnp.int32, (sub, sub), 1)
    perm = (ui == (ri % cs) * d + ri // cs).astype(BF16)
    slabs = [_dot(perm, h[s0:s0 + sub]).astype(BF16) for s0 in range(0, tm, sub)]
    return jnp.concatenate([sl[r * cs:(r + 1) * cs] for r in range(d) for sl in slabs], axis=0)


def _proj_kernel(x_ref, g_ref, wq_ref, wk_ref, wv_ref, cq_ref, sq_ref, ck_ref, s1_ref, s2_ref, *outs, na, d):
    tm = x_ref.shape[0]
    h = _rms(x_ref[...], g_ref[...]).astype(BF16)
    if d > 1:
        h = _deinterleave(h, d)
    nq = wq_ref.shape[0]
    heads = nq // HEAD_DIM

    qT = _dot_nt(wq_ref[...], h)
    q3 = qT.reshape(heads, HEAD_DIM, tm)
    c = cq_ref[...][None]
    s = sq_ref[...][None]
    x1 = q3[:, 0:ROT_HALF]
    x2 = q3[:, ROT_HALF:2 * ROT_HALF]
    q3 = jnp.concatenate([x1 * c - x2 * s, x2 * c + x1 * s, q3[:, 2 * ROT_HALF:]], axis=1)
    hid = lax.broadcasted_iota(jnp.int32, (heads, 1, 1), 0)
    q3 = q3 * jnp.where(hid < na // HEAD_DIM, SCALE * LOG2E, SCALE)
    qT = q3.reshape(nq, tm).astype(BF16)

    vT = _dot_nt(wv_ref[...], h).astype(BF16)

    kk = _dot(h, wk_ref[...])
    ck, s1, s2 = ck_ref[...], s1_ref[...], s2_ref[...]
    kparts = []
    for gi in range(kk.shape[1] // LANES):
        kg = kk[:, gi * LANES:(gi + 1) * LANES]
        kparts.append(kg * ck + pltpu.roll(kg, LANES - ROT_HALF, 1) * s1 + pltpu.roll(kg, ROT_HALF, 1) * s2)

    if na:
        qa_ref, va_ref, ka_ref, km_ref, qb_ref, vb_ref, kb_ref = outs
        for blk in range(tm // MOBA_BLOCK):
            sl = slice(blk * MOBA_BLOCK, (blk + 1) * MOBA_BLOCK)
            qa_ref[blk] = qT[0:na, sl]
            va_ref[blk] = vT[0:na, sl]
        lane = lax.broadcasted_iota(jnp.int32, (tm, LANES), 1)
        row = lax.broadcasted_iota(jnp.int32, (tm, LANES), 0)
        blkid = pl.program_id(1) * (tm // MOBA_BLOCK) + row // MOBA_BLOCK
        for gi in range(na // LANES):
            kg = kparts[gi]
            for e in range(2):
                in_head = (lane >= HEAD_DIM * e) & (lane < HEAD_DIM * (e + 1))
                onehot = (lane - HEAD_DIM * (1 - e)) == blkid
                col = (2 * gi + e) * LANES
                ka_ref[:, col:col + LANES] = jnp.where(in_head, kg, onehot.astype(F32)).astype(BF16)
            km = kg.reshape(tm // MOBA_BLOCK, MOBA_BLOCK, LANES).sum(axis=1) * (1.0 / MOBA_BLOCK)
            km_ref[:, gi * LANES:(gi + 1) * LANES] = km
    else:
        qb_ref, vb_ref, kb_ref = outs
    per = tm // d
    for r in range(d):
        for cb in range(per // DIL_BAND):
            sl = slice(r * per + cb * DIL_BAND, r * per + (cb + 1) * DIL_BAND)
            qb_ref[r, cb] = qT[na:na + WIDTH_G, sl]
            vb_ref[r, cb] = vT[na:na + WIDTH_G, sl]
        for gi in range(WIDTH_G // LANES):
            kb_ref[r, :, gi * LANES:(gi + 1) * LANES] = kparts[na // LANES + gi][r * per:(r + 1) * per].astype(BF16)


def _proj_tile(d):
    return PROJ_TM if d == 1 else max(PROJ_TM_DILATED, d * DIL_BAND)


def _project(x, g, wqT, wk, wvT, tabs, gi, d, na):
    B, S, D = x.shape
    L = S // d
    tm = _proj_tile(d)
    per = tm // d
    cq, sq, ck, s1, s2 = tabs
    grid = (B, S // tm)
    full = lambda a: pl.BlockSpec(a.shape, lambda b, i: (0,) * a.ndim)
    in_specs = [
        pl.BlockSpec((None, tm, D), lambda b, i: (b, i, 0)),
        full(g), full(wqT), full(wk), full(wvT),
        pl.BlockSpec((None, ROT_HALF, tm), lambda b, i: (gi, 0, i)),
        pl.BlockSpec((None, ROT_HALF, tm), lambda b, i: (gi, 0, i)),
        pl.BlockSpec((None, tm, LANES), lambda b, i: (gi, i, 0)),
        pl.BlockSpec((None, tm, LANES), lambda b, i: (gi, i, 0)),
        pl.BlockSpec((None, tm, LANES), lambda b, i: (gi, i, 0)),
    ]
    nsub = per // DIL_BAND
    out_shape = [
        jax.ShapeDtypeStruct((B, d, L // DIL_BAND, WIDTH_G, DIL_BAND), BF16),
        jax.ShapeDtypeStruct((B, d, L // DIL_BAND, WIDTH_G, DIL_BAND), BF16),
        jax.ShapeDtypeStruct((B, d, L, WIDTH_G), BF16),
    ]
    out_specs = [
        pl.BlockSpec((None, d, nsub, WIDTH_G, DIL_BAND), lambda b, i: (b, 0, i, 0, 0)),
        pl.BlockSpec((None, d, nsub, WIDTH_G, DIL_BAND), lambda b, i: (b, 0, i, 0, 0)),
        pl.BlockSpec((None, d, per, WIDTH_G), lambda b, i: (b, 0, i, 0)),
    ]
    if na:
        nblk = tm // MOBA_BLOCK
        out_shape = [
            jax.ShapeDtypeStruct((B, S // MOBA_BLOCK, na, MOBA_BLOCK), BF16),
            jax.ShapeDtypeStruct((B, S // MOBA_BLOCK, na, MOBA_BLOCK), BF16),
            jax.ShapeDtypeStruct((B, S, 2 * na), BF16),
            jax.ShapeDtypeStruct((B, S // tm, nblk, na), F32),
        ] + out_shape
        out_specs = [
            pl.BlockSpec((None, nblk, na, MOBA_BLOCK), lambda b, i: (b, i, 0, 0)),
            pl.BlockSpec((None, nblk, na, MOBA_BLOCK), lambda b, i: (b, i, 0, 0)),
            pl.BlockSpec((None, tm, 2 * na), lambda b, i: (b, i, 0)),
            pl.BlockSpec((None, None, nblk, na), lambda b, i: (b, i, 0, 0)),
        ] + out_specs
    return pl.pallas_call(
        functools.partial(_proj_kernel, na=na, d=d),
        grid=grid, in_specs=in_specs, out_specs=out_specs, out_shape=out_shape,
        compiler_params=_cparams(("arbitrary",) * 2),
        name=f"proj_d{d}",
    )(x, g, wqT, wk, wvT, cq, sq, ck, s1, s2)


def _moba_kernel(q_ref, k_ref, v_ref, km_ref, o_ref, qa_scr, s_scr, p_scr):
    n_super = q_ref.shape[0] // MOBA_QBLOCKS
    _moba_prepare(q_ref, k_ref, km_ref, qa_scr, s_scr, 0)

    def trip(g, c):
        _moba_attend(k_ref, v_ref, o_ref, qa_scr, s_scr, p_scr, g)
        _moba_prepare(q_ref, k_ref, km_ref, qa_scr, s_scr, jnp.minimum(g + 1, n_super - 1))
        return c

    lax.fori_loop(0, n_super, trip, 0)


def _moba_prepare(q_ref, k_ref, km_ref, qa_scr, s_scr, g):
    nb = km_ref.shape[0]
    blk = MOBA_BLOCK
    tq = MOBA_QBLOCKS * blk
    q2 = jnp.concatenate([q_ref[MOBA_QBLOCKS * g + i] for i in range(MOBA_QBLOCKS)], axis=1)
    bid = lax.broadcasted_iota(jnp.int32, (nb, tq), 0)
    qblk = MOBA_QBLOCKS * g + lax.broadcasted_iota(jnp.int32, (nb, tq), 1) // blk
    key_i = lax.broadcasted_iota(jnp.int32, (blk, tq), 0)
    qry_i = lax.broadcasted_iota(jnp.int32, (blk, tq), 1)
    km = km_ref[...].astype(BF16)
    zq = jnp.zeros((HEAD_DIM, tq), BF16)
    zb = jnp.zeros((HEAD_DIM - nb, tq), BF16)

    for h in range(2):
        hl = slice(LANES * h, LANES * (h + 1))
        qh = q2[HEAD_DIM * h:HEAD_DIM * (h + 1)]
        q_plain = jnp.concatenate([qh, zq] if h == 0 else [zq, qh], axis=0)
        gt = jnp.where(bid < qblk, _dot(km, q_plain), -jnp.inf)
        sel = jnp.zeros(gt.shape, jnp.bool_)
        for _ in range(MOBA_TOPK):
            mx = jnp.max(gt, axis=0, keepdims=True)
            first = jnp.min(jnp.where((gt == mx) & (mx > -jnp.inf), bid, nb), axis=0, keepdims=True)
            pick = bid == first
            sel = sel | pick
            gt = jnp.where(pick, -jnp.inf, gt)
        bias = jnp.where(sel, 0.0, NEG).astype(BF16)
        qa_scr[h] = jnp.concatenate([qh, bias, zb] if h == 0 else [bias, zb, qh], axis=0)
        bias_d = jnp.where(sel | (bid == qblk), 0.0, NEG).astype(BF16)
        q_diag = jnp.concatenate([qh, bias_d, zb] if h == 0 else [bias_d, zb, qh], axis=0)
        for i in range(MOBA_QBLOCKS):
            j = MOBA_QBLOCKS * g + i
            sd = _dot(k_ref[pl.ds(pl.multiple_of(j * blk, blk), blk), hl], q_diag)
            own = (qry_i >= i * blk) & (qry_i < (i + 1) * blk)
            s_scr[h, i] = jnp.where(own & (key_i > qry_i - i * blk), NEG, sd)


def _moba_attend(k_ref, v_ref, o_ref, qa_scr, s_scr, p_scr, g):
    nb = v_ref.shape[0]
    blk = MOBA_BLOCK
    tq = MOBA_QBLOCKS * blk

    def score_blocks(c):
        return [jnp.where(c < g, MOBA_QBLOCKS * c + i, nb - 1) for i in range(MOBA_QBLOCKS)]

    def value_blocks(it):
        return [jnp.where(it <= 1, MOBA_QBLOCKS * g + i, MOBA_QBLOCKS * (it - 2) + i) for i in range(MOBA_QBLOCKS)]

    def values(it, h, alpha, acc):
        vl = slice(HEAD_DIM * h, HEAD_DIM * (h + 1))
        acc = alpha * acc
        for i, j in enumerate(value_blocks(it)):
            acc = acc + _dot(v_ref[j, vl, :], p_scr[h, i])
        return acc

    p_scr[...] = jnp.zeros(p_scr.shape, BF16)

    def body(it, carry):
        alphas, ms, ls, accs = carry
        a_new, m_new, l_new, acc_new = [], [], [], []
        for h in range(2):
            hl = slice(LANES * h, LANES * (h + 1))
            qa = qa_scr[h]
            s_next = [_dot(k_ref[pl.ds(pl.multiple_of(j * blk, blk), blk), hl], qa) for j in score_blocks(it)]
            s_cur = [s_scr[h, i] for i in range(MOBA_QBLOCKS)]
            acc_new.append(values(it, h, alphas[h], accs[h]))
            mn = ms[h]
            for sc in s_cur:
                mn = jnp.maximum(mn, jnp.max(sc, axis=0, keepdims=True))
            alpha = jnp.exp2(ms[h] - mn)
            l = alpha * ls[h]
            for i in range(MOBA_QBLOCKS):
                p = jnp.exp2(s_cur[i] - mn)
                l = l + jnp.sum(p, axis=0, keepdims=True)
                s_scr[h, i] = s_next[i]
                p_scr[h, i] = p.astype(BF16)
            m_new.append(mn)
            l_new.append(l)
            a_new.append(alpha)
        return (tuple(a_new), tuple(m_new), tuple(l_new), tuple(acc_new))

    one = jnp.ones((1, tq), F32)
    neg = jnp.full((1, tq), NEG, F32)
    zl = jnp.zeros((1, tq), F32)
    za = jnp.zeros((HEAD_DIM, tq), F32)
    n_it = g + 1
    n_main = n_it // MOBA_UNROLL

    def trip(t, c):
        for u in range(MOBA_UNROLL):
            c = body(MOBA_UNROLL * t + u, c)
        return c

    carry = lax.fori_loop(0, n_main, trip, ((one, one), (neg, neg), (zl, zl), (za, za)))
    alphas, _, ls, accs = lax.fori_loop(MOBA_UNROLL * n_main, n_it, body, carry)
    accs = [values(n_it, h, alphas[h], accs[h]) for h in range(2)]
    oT = jnp.concatenate([accs[0] / ls[0], accs[1] / ls[1]], axis=0)
    o_ref[pl.ds(pl.multiple_of(g * tq, tq), tq), :] = oT.T.astype(o_ref.dtype)


def _moba(qT, k, vT, kmean):
    B, nb, wa, blk = qT.shape
    S = nb * blk
    hp = wa // LANES
    nq = MOBA_QBLOCKS
    assert nb % 16 == 0 and nb <= HEAD_DIM, "block-mask rows must fit the spare half of a head pair"
    return pl.pallas_call(
        _moba_kernel,
        grid=(B, hp),
        in_specs=[
            pl.BlockSpec((None, nb, LANES, blk), lambda b, p: (b, 0, p, 0)),
            pl.BlockSpec((None, S, 2 * LANES), lambda b, p: (b, 0, p)),
            pl.BlockSpec((None, nb, LANES, blk), lambda b, p: (b, 0, p, 0)),
            pl.BlockSpec((None, nb, LANES), lambda b, p: (b, 0, p)),
        ],
        out_specs=pl.BlockSpec((None, S, LANES), lambda b, p: (b, 0, p)),
        out_shape=jax.ShapeDtypeStruct((B, S, wa), BF16),
        scratch_shapes=[pltpu.VMEM((2, LANES, nq * blk), BF16), pltpu.VMEM((2, nq, blk, nq * blk), F32),
                        pltpu.VMEM((2, nq, blk, nq * blk), BF16)],
        compiler_params=_cparams(("arbitrary",) * 2),
        name="moba_attn",
    )(qT, k, vT, kmean)


def _dil_kernel(q_ref, k_ref, kp_ref, v_ref, vp_ref, o_ref, lse_ref):
    t = pl.program_id(2)
    band = DIL_BAND
    nres, nsub = q_ref.shape[0], q_ref.shape[1]
    key_i = lax.broadcasted_iota(jnp.int32, (band, band), 0)
    qry_i = lax.broadcasted_iota(jnp.int32, (band, band), 1)
    rowid = lax.broadcasted_iota(jnp.int32, (LANES, band), 0)
    own_ok = key_i <= qry_i
    prev_ok = key_i >= qry_i
    for r in range(nres):
        for c in range(nsub):
            o_parts, l_parts = [], []
            for hp in range(WIDTH_G // LANES):
                cols = slice(hp * LANES, (hp + 1) * LANES)
                k_own = k_ref[r, c * band:(c + 1) * band, cols]
                k_prev = kp_ref[r, :, cols] if c == 0 else k_ref[r, (c - 1) * band:c * band, cols]
                q2 = q_ref[r, c, cols, :]
                for h in range(2):
                    qh = jnp.where((rowid >= HEAD_DIM * h) & (rowid < HEAD_DIM * (h + 1)), q2, jnp.zeros_like(q2))
                    pmask = (prev_ok & (t > 0)) if c == 0 else prev_ok
                    s_own = jnp.where(own_ok, _dot(k_own, qh), -jnp.inf)
                    s_prev = jnp.where(pmask, _dot(k_prev, qh), -jnp.inf)
                    m = jnp.maximum(jnp.max(s_own, axis=0, keepdims=True), jnp.max(s_prev, axis=0, keepdims=True))
                    p_own = jnp.exp(s_own - m)
                    p_prev = jnp.exp(s_prev - m)
                    l = jnp.sum(p_own, axis=0, keepdims=True) + jnp.sum(p_prev, axis=0, keepdims=True)
                    rows = slice(hp * LANES + h * HEAD_DIM, hp * LANES + (h + 1) * HEAD_DIM)
                    v_own = v_ref[r, c, rows, :]
                    v_prev = vp_ref[r, rows, :] if c == 0 else v_ref[r, c - 1, rows, :]
                    oT = _dot(v_own, p_own.astype(BF16)) + _dot(v_prev, p_prev.astype(BF16))
                    o_parts.append(oT / l)
                    l_parts.append(jnp.broadcast_to(m + jnp.log(l), (HEAD_DIM, band)))
            o_ref[r, c * band:(c + 1) * band, :] = jnp.concatenate(o_parts, axis=0).T
            lse_ref[r, c * band:(c + 1) * band, :] = jnp.concatenate(l_parts, axis=0).T


def _dilated(qT, k, vT):
    B, d, nblk, wg, band = qT.shape
    L = nblk * band
    tl = min(DIL_TL, L)
    nsub = tl // band
    nres = min(d, max(1, DIL_TL // L))
    prev = lambda t: jnp.maximum(t * nsub - 1, 0)
    return pl.pallas_call(
        _dil_kernel,
        grid=(B, d // nres, L // tl),
        in_specs=[
            pl.BlockSpec((None, nres, nsub, wg, band), lambda b, r, t: (b, r, t, 0, 0)),
            pl.BlockSpec((None, nres, tl, wg), lambda b, r, t: (b, r, t, 0)),
            pl.BlockSpec((None, nres, band, wg), lambda b, r, t: (b, r, prev(t), 0)),
            pl.BlockSpec((None, nres, nsub, wg, band), lambda b, r, t: (b, r, t, 0, 0)),
            pl.BlockSpec((None, nres, None, wg, band), lambda b, r, t: (b, r, prev(t), 0, 0)),
        ],
        out_specs=[
            pl.BlockSpec((None, nres, tl, wg), lambda b, r, t: (b, r, t, 0)),
            pl.BlockSpec((None, nres, tl, wg), lambda b, r, t: (b, r, t, 0)),
        ],
        out_shape=[jax.ShapeDtypeStruct((B, d, L, wg), F32)] * 2,
        compiler_params=_cparams(("arbitrary",) * 3),
        name=f"dilated_d{d}",
    )(qT, k, k, vT, vT)


def _interleave(ref, scr):
    d, per, w = ref.shape
    if d == 1:
        return ref[0]
    for r in range(d):
        for sl in range(w // LANES):
            scr[sl, pl.ds(r, per, stride=d), :] = ref[r, :, sl * LANES:(sl + 1) * LANES]
    return jnp.concatenate([scr[sl] for sl in range(w // LANES)], axis=1)


def _post_kernel(x_ref, oa_ref, o1_ref, o2_ref, o3_ref, l1_ref, l2_ref, l3_ref, gmix_ref, wg_ref, bga_ref,
                 bgb_ref, woa_ref, wob_ref, wout_ref, gffn_ref, wr_ref, br_ref,
                 x1_ref, h2_ref, idx_ref, rank_ref, pnat_ref, cnt_ref, carry_scr, il_scr):
    i = pl.program_id(0)
    tm, D = x_ref.shape
    ne = wr_ref.shape[0]

    @pl.when(i == 0)
    def _():
        carry_scr[...] = jnp.zeros_like(carry_scr)

    x = x_ref[...]
    h = _rms(x, gmix_ref[...]).astype(BF16)
    gates = _dot(h, wg_ref[...])
    ga = gates[:, :D] + bga_ref[...]
    gb = gates[:, D:] + bgb_ref[...]
    l1, l2, l3 = [_interleave(r, il_scr.at[n]) for n, r in enumerate((l1_ref, l2_ref, l3_ref))]
    o1, o2, o3 = [_interleave(r, il_scr.at[3 + n]) for n, r in enumerate((o1_ref, o2_ref, o3_ref))]
    mx = jnp.maximum(jnp.maximum(l1, l2), l3)
    e1, e2, e3 = jnp.exp(l1 - mx), jnp.exp(l2 - mx), jnp.exp(l3 - mx)
    ob = (e1 * o1 + e2 * o2 + e3 * o3) / (e1 + e2 + e3)
    ya = _dot(oa_ref[...], woa_ref[...])
    yb = _dot(ob.astype(BF16), wob_ref[...])
    mix = jax.nn.sigmoid(ga) * ya + jax.nn.sigmoid(gb) * yb
    x1 = x + _dot(mix.astype(BF16), wout_ref[...])
    x1_ref[...] = x1
    h2 = _rms(x1, gffn_ref[...])
    _rows_to_tiles(h2_ref, h2)

    wr = wr_ref[...]
    wr_hi = wr.astype(BF16)
    wr_lo = (wr - wr_hi.astype(F32)).astype(BF16)
    h2_hi = h2.astype(BF16)
    h2_lo = (h2 - h2_hi.astype(F32)).astype(BF16)
    logits = _dot_nt(wr_hi, h2_hi) + (_dot_nt(wr_hi, h2_lo) + _dot_nt(wr_lo, h2_hi)) + br_ref[...]
    eid = lax.broadcasted_iota(jnp.int32, (ne, tm), 0)
    g = logits
    vals, idxs, picks = [], [], []
    for _ in range(TOP_K):
        m = jnp.max(g, axis=0, keepdims=True)
        first = jnp.min(jnp.where(g == m, eid, ne), axis=0, keepdims=True)
        pick = eid == first
        vals.append(m)
        idxs.append(first)
        picks.append(pick)
        g = jnp.where(pick, -jnp.inf, g)
    es = [jnp.exp(v - vals[0]) for v in vals]
    den = es[0] + es[1] + es[2] + es[3]
    probs = [e / den for e in es]

    onehot = jnp.zeros((ne, tm), F32)
    for pick in picks:
        onehot = onehot + pick.astype(F32)
    earlier = (lax.broadcasted_iota(jnp.int32, (tm, tm), 0) < lax.broadcasted_iota(jnp.int32, (tm, tm), 1))
    prefix = _dot(onehot.astype(BF16), earlier.astype(BF16)) + carry_scr[:, 0:1]
    ranks = [jnp.sum(jnp.where(pick, prefix, 0.0), axis=0, keepdims=True) for pick in picks]
    carry_scr[...] = carry_scr[...] + jnp.sum(onehot, axis=1, keepdims=True)
    cnt_ref[...] = carry_scr[...]

    zi = jnp.zeros((8 - TOP_K, tm), jnp.int32)
    idx_ref[...] = jnp.concatenate(idxs + [zi], axis=0)
    rank_ref[...] = jnp.concatenate([r.astype(jnp.int32) for r in ranks] + [zi], axis=0)
    pnat_ref[...] = jnp.concatenate(probs + [jnp.zeros((LANES - TOP_K, tm), F32)], axis=0).T


def _post(x2, oa, obs, lses, gmix, wg, bga, bgb, woa, wob, wout, gffn, wrT, br):
    T, D = x2.shape
    tm = POST_TM
    ne = wrT.shape[0]
    row = lambda w: pl.BlockSpec((tm, w), lambda i: (i, 0))

    def grouped(a):
        _, d, L, w = a.shape
        nt = L * d // tm
        return pl.BlockSpec((None, d, tm // d, w), lambda i: (i // nt, 0, i % nt, 0))

    full = lambda a: pl.BlockSpec(a.shape, lambda i: (0,) * a.ndim)
    col = pl.BlockSpec((8, tm), lambda i: (0, i))
    return pl.pallas_call(
        _post_kernel,
        grid=(T // tm,),
        in_specs=[row(D), row(WIDTH_A)] + [grouped(a) for a in (*obs, *lses)]
                 + [full(a) for a in (gmix, wg, bga, bgb, woa, wob, wout, gffn, wrT, br)],
        out_specs=[row(D), pl.BlockSpec((tm * SUBLANES, LANES), lambda i: (i, 0)), col, col, row(LANES),
                   pl.BlockSpec((ne, LANES), lambda i: (0, 0))],
        out_shape=[
            jax.ShapeDtypeStruct((T, D), F32), jax.ShapeDtypeStruct((T * SUBLANES, LANES), F32),
            jax.ShapeDtypeStruct((8, T), jnp.int32), jax.ShapeDtypeStruct((8, T), jnp.int32),
            jax.ShapeDtypeStruct((T, LANES), F32),
            jax.ShapeDtypeStruct((ne, LANES), F32),
        ],
        scratch_shapes=[pltpu.VMEM((ne, LANES), F32), pltpu.VMEM((6, WIDTH_G // LANES, tm, LANES), F32)],
        compiler_params=_cparams(("arbitrary",)),
        name="post_mix_router",
    )(x2, oa, *obs, *lses, gmix, wg, bga, bgb, woa, wob, wout, gffn, wrT, br)


def _dispatch_kernel(pend_ref, padded_ref, dest_ref, h_ref, xr_ref, zero_scr, sem):
    i = pl.program_id(0)
    tm = h_ref.shape[0] // SUBLANES
    br = zero_scr.shape[0]

    def zero_copy(blk_start):
        return pltpu.make_async_copy(zero_scr, xr_ref.at[pl.ds(pl.multiple_of(blk_start * SUBLANES, br), br), :], sem)

    @pl.when(i == 0)
    def _():
        zero_scr[...] = jnp.zeros_like(zero_scr)
        n_used = pend_ref[N_EXPERTS - 1] // EXPERT_ROWS
        n_blk = xr_ref.shape[0] // br

        def tail(fn):
            def go(e, c):
                @pl.when(padded_ref[e] > 0)
                def _():
                    fn(zero_copy(pend_ref[e] - EXPERT_ROWS))
                return c
            return go

        def unused(fn):
            def go(b, c):
                fn(zero_copy(b * EXPERT_ROWS))
                return c
            return go

        lax.fori_loop(0, N_EXPERTS, tail(lambda cp: cp.start()), 0)
        lax.fori_loop(n_used, n_blk, unused(lambda cp: cp.start()), 0)
        lax.fori_loop(0, N_EXPERTS, tail(lambda cp: cp.wait()), 0)
        lax.fori_loop(n_used, n_blk, unused(lambda cp: cp.wait()), 0)

    def row_copy(t, k):
        src = h_ref.at[pl.ds(pl.multiple_of(t * SUBLANES, SUBLANES), SUBLANES), :]
        dst = xr_ref.at[pl.ds(pl.multiple_of(dest_ref[k, t] * SUBLANES, SUBLANES), SUBLANES), :]
        return pltpu.make_async_copy(src, dst, sem)

    def start(tb, c):
        for u in range(ROW_UNROLL):
            for k in range(TOP_K):
                row_copy(tb * ROW_UNROLL + u, k).start(priority=(u * TOP_K + k) % 2)
        return c

    def wait(tb, c):
        for u in range(ROW_UNROLL):
            for k in range(TOP_K):
                row_copy(tb * ROW_UNROLL + u, k).wait()
        return c

    lax.fori_loop(0, tm // ROW_UNROLL, start, 0)
    lax.fori_loop(0, tm // ROW_UNROLL, wait, 0)


def _dispatch(pend, padded, dest3, h2t, n_rows):
    tm = DISPATCH_TM
    return pl.pallas_call(
        _dispatch_kernel,
        grid_spec=pltpu.PrefetchScalarGridSpec(
            num_scalar_prefetch=2,
            grid=(h2t.shape[0] // (tm * SUBLANES),),
            in_specs=[
                pl.BlockSpec((None, TOP_K, tm), lambda i, pe, pa: (i, 0, 0), memory_space=pltpu.SMEM),
                pl.BlockSpec((tm * SUBLANES, LANES), lambda i, pe, pa: (i, 0)),
            ],
            out_specs=pl.BlockSpec(memory_space=pl.ANY),
            scratch_shapes=[pltpu.VMEM((EXPERT_ROWS * SUBLANES, LANES), F32), pltpu.SemaphoreType.DMA(())],
        ),
        out_shape=jax.ShapeDtypeStruct((n_rows * SUBLANES, LANES), F32),
        compiler_params=_cparams(("arbitrary",)),
        name="moe_dispatch",
    )(pend, padded, dest3, h2t)


def _expert_kernel(pstart_ref, padded_ref, x_hbm, wg_ref, bg_ref, wu_ref, bu_ref, wd_ref, bd_ref, y_hbm,
                   w_scr, xbuf, ybuf, xsem, ysem):
    e = pl.program_id(0)
    tb = EXPERT_ROWS * SUBLANES
    nblk = padded_ref[e] // EXPERT_ROWS
    npair = nblk // EXPERT_PAIR
    row0 = pstart_ref[e]

    def rows(first_blk, nb_):
        return pl.ds(pl.multiple_of((row0 + first_blk * EXPERT_ROWS) * SUBLANES, tb), nb_ * tb)

    def x_copy(first_blk, nb_, slot):
        return pltpu.make_async_copy(x_hbm.at[rows(first_blk, nb_), :], xbuf.at[slot, pl.ds(0, nb_ * tb), :],
                                     xsem.at[slot])

    def y_copy(first_blk, nb_, slot):
        return pltpu.make_async_copy(ybuf.at[slot, pl.ds(0, nb_ * tb), :], y_hbm.at[rows(first_blk, nb_), :],
                                     ysem.at[slot])

    def ffn(slot, nb_):
        n = nb_ * EXPERT_ROWS
        x = _tiles_to_rows(xbuf.at[slot, pl.ds(0, nb_ * tb), :], n).astype(BF16)
        g = _dot(x, w_scr[0]) + bg_ref[...]
        u = _dot(x, w_scr[1]) + bu_ref[...]
        g = jnp.minimum(g, SWIGLU_LIMIT)
        u = jnp.clip(u, -SWIGLU_LIMIT, SWIGLU_LIMIT)
        a = g * jax.nn.sigmoid(SWIGLU_ALPHA * g) * (u + 1.0)
        _rows_to_tiles(ybuf.at[slot, pl.ds(0, nb_ * tb), :], _dot(a.astype(BF16), w_scr[2]) + bd_ref[...])

    @pl.when(nblk > 0)
    def _():
        has_tail = nblk > npair * EXPERT_PAIR
        nstep = npair + has_tail.astype(jnp.int32)

        def start_x(s, slot):
            @pl.when(s < npair)
            def _():
                x_copy(s * EXPERT_PAIR, EXPERT_PAIR, slot).start(priority=TILE_DMA_PRIORITY)

            @pl.when((s == npair) & has_tail)
            def _():
                x_copy(s * EXPERT_PAIR, 1, slot).start(priority=TILE_DMA_PRIORITY)

        def wait_y(s, slot):
            @pl.when(s < npair)
            def _():
                y_copy(s * EXPERT_PAIR, EXPERT_PAIR, slot).wait()

            @pl.when((s == npair) & has_tail)
            def _():
                y_copy(s * EXPERT_PAIR, 1, slot).wait()

        start_x(0, 0)
        w_scr[0] = wg_ref[...].astype(BF16)
        w_scr[1] = wu_ref[...].astype(BF16)
        w_scr[2] = wd_ref[...].astype(BF16)

        def body(s, c):
            slot = s % 2
            start_x(s + 1, 1 - slot)
            x_copy(s * EXPERT_PAIR, EXPERT_PAIR, slot).wait()

            @pl.when(s >= 2)
            def _():
                y_copy((s - 2) * EXPERT_PAIR, EXPERT_PAIR, slot).wait()

            ffn(slot, EXPERT_PAIR)
            y_copy(s * EXPERT_PAIR, EXPERT_PAIR, slot).start(priority=TILE_DMA_PRIORITY)
            return c

        lax.fori_loop(0, npair, body, 0)

        @pl.when(has_tail)
        def _():
            slot = npair % 2
            x_copy(npair * EXPERT_PAIR, 1, slot).wait()

            @pl.when(npair >= 2)
            def _():
                y_copy((npair - 2) * EXPERT_PAIR, EXPERT_PAIR, slot).wait()

            ffn(slot, 1)
            y_copy(npair * EXPERT_PAIR, 1, slot).start(priority=TILE_DMA_PRIORITY)

        @pl.when(nstep >= 2)
        def _():
            wait_y(nstep - 2, nstep % 2)

        wait_y(nstep - 1, (nstep - 1) % 2)

    @pl.when(e == pl.num_programs(0) - 1)
    def _():
        n_used = (pstart_ref[e] + padded_ref[e]) // EXPERT_ROWS
        n_all = y_hbm.shape[0] // tb
        ybuf[0, pl.ds(0, tb), :] = jnp.zeros((tb, LANES), F32)

        def zero_copy(b):
            return pltpu.make_async_copy(ybuf.at[0, pl.ds(0, tb), :],
                                         y_hbm.at[pl.ds(pl.multiple_of(b * tb, tb), tb), :], ysem.at[0])

        lax.fori_loop(n_used, n_all, lambda b, c: (zero_copy(b).start(), c)[1], 0)
        lax.fori_loop(n_used, n_all, lambda b, c: (zero_copy(b).wait(), c)[1], 0)


def _experts(pstart, padded, x_rows, w_gate, b_gate, w_up, b_up, w_down, b_down):
    E, D, F = w_gate.shape
    tb = EXPERT_PAIR * EXPERT_ROWS * SUBLANES
    wspec = lambda shape: pl.BlockSpec((None,) + shape, lambda e, ps, pa: (e, 0, 0))
    return pl.pallas_call(
        _expert_kernel,
        grid_spec=pltpu.PrefetchScalarGridSpec(
            num_scalar_prefetch=2,
            grid=(E,),
            in_specs=[
                pl.BlockSpec(memory_space=pl.ANY),
                wspec((D, F)), wspec((1, F)), wspec((D, F)), wspec((1, F)), wspec((F, D)), wspec((1, D)),
            ],
            out_specs=pl.BlockSpec(memory_space=pl.ANY),
            scratch_shapes=[pltpu.VMEM((3, D, F), BF16), pltpu.VMEM((2, tb, LANES), F32),
                            pltpu.VMEM((2, tb, LANES), F32), pltpu.SemaphoreType.DMA((2,)),
                            pltpu.SemaphoreType.DMA((2,))],
        ),
        out_shape=jax.ShapeDtypeStruct(x_rows.shape, F32),
        compiler_params=_cparams(("arbitrary",)),
        name="moe_experts",
    )(pstart, padded, x_rows, w_gate, b_gate.reshape(E, 1, F), w_up, b_up.reshape(E, 1, F),
      w_down, b_down.reshape(E, 1, D))


def _combine_kernel(dest_ref, dnext_ref, y_ref, x1_ref, p_ref, g_ref, o_ref, ybuf, sem):
    i = pl.program_id(0)
    tm = x1_ref.shape[0]
    slot = i % 2

    def row_copy(d_ref, sl, t, k):
        src = y_ref.at[pl.ds(pl.multiple_of(d_ref[k, t] * SUBLANES, SUBLANES), SUBLANES), :]
        dst = ybuf.at[sl, k, pl.ds(pl.multiple_of(t * SUBLANES, SUBLANES), SUBLANES), :]
        return pltpu.make_async_copy(src, dst, sem.at[sl])

    def request(d_ref, sl):
        def go(tb, c):
            for u in range(ROW_UNROLL):
                for k in range(TOP_K):
                    row_copy(d_ref, sl, tb * ROW_UNROLL + u, k).start(priority=(u * TOP_K + k) % 2)
            return c
        lax.fori_loop(0, tm // ROW_UNROLL, go, 0)

    @pl.when(i == 0)
    def _():
        request(dest_ref, 0)

    @pl.when(i + 1 < pl.num_programs(0))
    def _():
        request(dnext_ref, 1 - slot)

    def wait(tb, c):
        for u in range(ROW_UNROLL):
            for k in range(TOP_K):
                row_copy(dest_ref, slot, tb * ROW_UNROLL + u, k).wait()
        return c

    lax.fori_loop(0, tm // ROW_UNROLL, wait, 0)
    p = p_ref[...]
    y = x1_ref[...]
    for k in range(TOP_K):
        y = y + p[:, k:k + 1] * _tiles_to_rows(ybuf.at[slot, k], tm)
    o_ref[...] = _rms(y, g_ref[...])


def _combine(dest3, y_rows, x1, pnat, g_final):
    T, D = x1.shape
    tm = COMBINE_TM
    nt = T // tm
    return pl.pallas_call(
        _combine_kernel,
        grid=(nt,),
        in_specs=[
            pl.BlockSpec((None, TOP_K, tm), lambda i: (i, 0, 0), memory_space=pltpu.SMEM),
            pl.BlockSpec((None, TOP_K, tm), lambda i: (jnp.minimum(i + 1, nt - 1), 0, 0), memory_space=pltpu.SMEM),
            pl.BlockSpec(memory_space=pl.ANY),
            pl.BlockSpec((tm, D), lambda i: (i, 0)),
            pl.BlockSpec((tm, LANES), lambda i: (i, 0)),
            pl.BlockSpec((1, D), lambda i: (0, 0)),
        ],
        out_specs=pl.BlockSpec((tm, D), lambda i: (i, 0)),
        out_shape=jax.ShapeDtypeStruct((T, D), F32),
        scratch_shapes=[pltpu.VMEM((2, TOP_K, tm * SUBLANES, LANES), F32), pltpu.SemaphoreType.DMA((2,))],
        compiler_params=_cparams(("arbitrary",)),
        name="moe_combine",
    )(dest3, dest3, y_rows, x1, pnat, g_final)


def _rope_tables(S, groups):
    ng = len(groups)

    def cos_sin(shape, ax, f):
        rho = lax.broadcasted_iota(jnp.int32, shape, ax)
        gid = lax.broadcasted_iota(jnp.int32, shape, 0)
        pos = jnp.zeros(shape, jnp.int32)
        for n, (d, tm) in enumerate(groups):
            per = tm // d
            w = rho % tm
            pos = jnp.where(gid == n, ((rho // tm) * per + w % per) * d + w // per, pos)
        inv = jnp.float32(ROPE_THETA) ** (-f.astype(F32) / ROT_HALF)
        ang = pos.astype(F32) * inv
        return jnp.cos(ang), jnp.sin(ang)

    sh = (ng, ROT_HALF, S)
    cq, sq = cos_sin(sh, 2, lax.broadcasted_iota(jnp.int32, sh, 1))
    sh = (ng, S, ROT_HALF)
    cos, sin = cos_sin(sh, 1, lax.broadcasted_iota(jnp.int32, sh, 2))
    lane = lax.broadcasted_iota(jnp.int32, (ROT_HALF, LANES), 1) % HEAD_DIM
    f = lax.broadcasted_iota(jnp.int32, (ROT_HALF, LANES), 0)
    lo = ((lane == f)).astype(F32)
    hi = ((lane == f + ROT_HALF)).astype(F32)
    spread = lambda a, m: jnp.dot(a, m, precision=lax.Precision.HIGHEST)
    rest = (lax.broadcasted_iota(jnp.int32, (1, 1, LANES), 2) % HEAD_DIM >= 2 * ROT_HALF).astype(F32)
    ck = spread(cos, lo + hi) + rest
    s1 = spread(-sin, lo)
    s2 = spread(sin, hi)
    return cq, sq, ck, s1, s2


def _layer(x, ln_mix_g, w_in, b_gate_a, b_gate_b, w_o_a, w_o_b, w_out, ln_ffn_g, w_router, b_router,
           w_gate, b_gate, w_up, b_up, w_down, b_down, ln_out_g):
    B, S, D = x.shape
    T = B * S
    qa0, ka0, va0 = 0, WIDTH_A, 2 * WIDTH_A
    qb0, kb0, vb0 = 3 * WIDTH_A, 3 * WIDTH_A + WIDTH_B, 3 * WIDTH_A + 2 * WIDTH_B
    g0 = 3 * WIDTH_A + 3 * WIDTH_B
    cols = lambda s, w: w_in[:, s:s + w]
    gmix = ln_mix_g.reshape(1, D)

    tabs = _rope_tables(S, [(d, _proj_tile(d)) for _, d in DIL_PAIRS])
    obs, lses = [], []
    o_a = None
    for gi, (_, d) in enumerate(DIL_PAIRS):
        off = gi * WIDTH_G
        wq, wk, wv = cols(qb0 + off, WIDTH_G), cols(kb0 + off, WIDTH_G), cols(vb0 + off, WIDTH_G)
        na = 0
        if gi == 0:
            na = WIDTH_A
            wq = jnp.concatenate([cols(qa0, WIDTH_A), wq], axis=1)
            wk = jnp.concatenate([cols(ka0, WIDTH_A), wk], axis=1)
            wv = jnp.concatenate([cols(va0, WIDTH_A), wv], axis=1)
        outs = _project(x, gmix, wq.T.astype(BF16), wk.astype(BF16), wv.T.astype(BF16), tabs, gi, d, na)
        if gi == 0:
            qTa, vTa, ka, kmean = outs[:4]
            outs = outs[4:]
            o_a = _moba(qTa, ka, vTa, kmean.reshape(B, S // MOBA_BLOCK, WIDTH_A))
        qTb, vTb, kb = outs
        o_g, lse_g = _dilated(qTb, kb, vTb)
        obs.append(o_g)
        lses.append(lse_g)

    x1, h2, idxT, rankT, pnat, cnt = _post(
        x.reshape(T, D), o_a.reshape(T, WIDTH_A), obs, lses, gmix,
        cols(g0, 2 * D).astype(BF16), b_gate_a.reshape(1, D), b_gate_b.reshape(1, D),
        w_o_a.astype(BF16), w_o_b.astype(BF16), w_out.astype(BF16), ln_ffn_g.reshape(1, D),
        w_router.T, b_router.reshape(N_EXPERTS, 1))

    br = EXPERT_ROWS
    counts = cnt[:, 0].astype(jnp.int32)
    padded = (counts + br - 1) // br * br
    pend = jnp.cumsum(padded)
    pstart = pend - padded
    eids = jnp.arange(N_EXPERTS, dtype=jnp.int32)[:, None, None]
    dest = jnp.sum(jnp.where(idxT[None, :TOP_K] == eids, pstart[:, None, None], 0), axis=0) + rankT[:TOP_K]
    n_rows = T * TOP_K + N_EXPERTS * br

    dest_d = dest.reshape(TOP_K, T // DISPATCH_TM, DISPATCH_TM).transpose(1, 0, 2)
    x_rows = _dispatch(pend.astype(jnp.int32), padded.astype(jnp.int32), dest_d, h2, n_rows)
    y_rows = _experts(pstart.astype(jnp.int32), padded.astype(jnp.int32), x_rows, w_gate, b_gate, w_up, b_up,
                      w_down, b_down)
    dest_c = dest.reshape(TOP_K, T // COMBINE_TM, COMBINE_TM).transpose(1, 0, 2)
    out = _combine(dest_c, y_rows, x1, pnat, ln_out_g.reshape(1, D))
    return out.reshape(B, S, D)


def kernel(x, ln_mix_g, w_in, b_gate_a, b_gate_b, w_o_a, w_o_b, w_out, ln_ffn_g, w_router, b_router,
           w_gate, b_gate, w_up, b_up, w_down, b_down, ln_final_g):
    depth = ln_mix_g.shape[0]
    assert depth == 1, "the final RMSNorm is fused into the last layer's combine"
    return _layer(x, ln_mix_g[0], w_in[0], b_gate_a[0], b_gate_b[0], w_o_a[0], w_o_b[0], w_out[0],
                  ln_ffn_g[0], w_router[0], b_router[0], w_gate[0], b_gate[0], w_up[0], b_up[0],
                  w_down[0], b_down[0], ln_final_g)
```

```python
import functools

import jax
import jax.numpy as jnp
from jax import lax
from jax.experimental import pallas as pl
from jax.experimental.pallas import tpu as pltpu

D_MODEL = 1024
HEAD_DIM = 64
ROT_HALF = HEAD_DIM // 8
ROPE_THETA = 500000.0
N_HEADS_A = 8
MOBA_BLOCK = 256
MOBA_TOPK = 3
MOBA_QBLOCKS = 2
MOBA_UNROLL = 4
MOBA_GATE_BLOCKS = 8
DIL_PAIRS = ((128, 1), (512, 4), (2048, 16))
DIL_BAND = 128
HEADS_PER_GROUP_B = 4
WIDTH_A = N_HEADS_A * HEAD_DIM
WIDTH_G = HEADS_PER_GROUP_B * HEAD_DIM
WIDTH_B = WIDTH_G * len(DIL_PAIRS)
N_EXPERTS = 32
TOP_K = 4
SWIGLU_LIMIT = 7.0
SWIGLU_ALPHA = 1.702
NORM_EPS = 1e-5
SCALE = HEAD_DIM ** -0.5
LOG2E = 1.4426950408889634

LANES = 128
PROJ_TM = 1024
PROJ_TM_DILATED = 2048
PERM_TILE = 512
DIL_TL = 2048
POST_TM = 512
EXPERT_ROWS = 256
EXPERT_PAIR = 2
DISPATCH_TM = 512
COMBINE_TM = 256
SUBLANES = 8
ROW_UNROLL = 8
VMEM_LIMIT = 56 * 1024 * 1024
NEG = -1e30

BF16 = jnp.bfloat16
F32 = jnp.float32


def _dot(a, b):
    return jnp.dot(a, b, preferred_element_type=F32)


def _dot_nt(a, b):
    return lax.dot_general(a, b, (((1,), (1,)), ((), ())), preferred_element_type=F32)


def _rms(x, g):
    ms = jnp.mean(x * x, axis=-1, keepdims=True)
    return x * lax.rsqrt(ms + NORM_EPS) * g


def _rows_to_tiles(ref, val):
    n = val.shape[0]
    for c in range(val.shape[1] // LANES):
        ref[pl.ds(c, n, stride=SUBLANES), :] = val[:, c * LANES:(c + 1) * LANES]


def _tiles_to_rows(ref, n):
    return jnp.concatenate([ref[pl.ds(c, n, stride=SUBLANES), :] for c in range(SUBLANES)], axis=1)


def _cparams(sem):
    return pltpu.CompilerParams(dimension_semantics=sem, vmem_limit_bytes=VMEM_LIMIT)


def _deinterleave(h, d):
    tm = h.shape[0]
    sub = min(tm, PERM_TILE)
    cs = sub // d
    ri = lax.broadcasted_iota(jnp.int32, (sub, sub), 0)
    ui = lax.broadcasted_iota(jnp.int32, (sub, sub), 1)
    perm = (ui == (ri % cs) * d + ri // cs).astype(BF16)
    slabs = [_dot(perm, h[s0:s0 + sub]).astype(BF16) for s0 in range(0, tm, sub)]
    return jnp.concatenate([sl[r * cs:(r + 1) * cs] for r in range(d) for sl in slabs], axis=0)


def _proj_kernel(x_ref, g_ref, wq_ref, wk_ref, wv_ref, cq_ref, sq_ref, ck_ref, s1_ref, s2_ref, *outs, na, d):
    tm = x_ref.shape[0]
    h = _rms(x_ref[...], g_ref[...]).astype(BF16)
    if d > 1:
        h = _deinterleave(h, d)
    nq = wq_ref.shape[0]
    heads = nq // HEAD_DIM

    qT = _dot_nt(wq_ref[...], h)
    q3 = qT.reshape(heads, HEAD_DIM, tm)
    c = cq_ref[...][None]
    s = sq_ref[...][None]
    x1 = q3[:, 0:ROT_HALF]
    x2 = q3[:, ROT_HALF:2 * ROT_HALF]
    q3 = jnp.concatenate([x1 * c - x2 * s, x2 * c + x1 * s, q3[:, 2 * ROT_HALF:]], axis=1)
    hid = lax.broadcasted_iota(jnp.int32, (heads, 1, 1), 0)
    q3 = q3 * jnp.where(hid < na // HEAD_DIM, SCALE * LOG2E, SCALE)
    qT = q3.reshape(nq, tm).astype(BF16)

    vT = _dot_nt(wv_ref[...], h).astype(BF16)

    kk = _dot(h, wk_ref[...])
    ck, s1, s2 = ck_ref[...], s1_ref[...], s2_ref[...]
    kparts = []
    for gi in range(kk.shape[1] // LANES):
        kg = kk[:, gi * LANES:(gi + 1) * LANES]
        kparts.append(kg * ck + pltpu.roll(kg, LANES - ROT_HALF, 1) * s1 + pltpu.roll(kg, ROT_HALF, 1) * s2)

    if na:
        qa_ref, va_ref, ka_ref, km_ref, qb_ref, vb_ref, kb_ref = outs
        for blk in range(tm // MOBA_BLOCK):
            sl = slice(blk * MOBA_BLOCK, (blk + 1) * MOBA_BLOCK)
            qa_ref[blk] = qT[0:na, sl]
            va_ref[blk] = vT[0:na, sl]
        lane = lax.broadcasted_iota(jnp.int32, (tm, LANES), 1)
        row = lax.broadcasted_iota(jnp.int32, (tm, LANES), 0)
        blkid = pl.program_id(1) * (tm // MOBA_BLOCK) + row // MOBA_BLOCK
        for gi in range(na // LANES):
            kg = kparts[gi]
            for e in range(2):
                in_head = (lane >= HEAD_DIM * e) & (lane < HEAD_DIM * (e + 1))
                onehot = (lane - HEAD_DIM * (1 - e)) == blkid
                col = (2 * gi + e) * LANES
                ka_ref[:, col:col + LANES] = jnp.where(in_head, kg, onehot.astype(F32)).astype(BF16)
            km = kg.reshape(tm // MOBA_BLOCK, MOBA_BLOCK, LANES).sum(axis=1) * (1.0 / MOBA_BLOCK)
            km_ref[:, gi * LANES:(gi + 1) * LANES] = km
    else:
        qb_ref, vb_ref, kb_ref = outs
    per = tm // d
    for r in range(d):
        for cb in range(per // DIL_BAND):
            sl = slice(r * per + cb * DIL_BAND, r * per + (cb + 1) * DIL_BAND)
            qb_ref[r, cb] = qT[na:na + WIDTH_G, sl]
            vb_ref[r, cb] = vT[na:na + WIDTH_G, sl]
        for gi in range(WIDTH_G // LANES):
            kb_ref[r, :, gi * LANES:(gi + 1) * LANES] = kparts[na // LANES + gi][r * per:(r + 1) * per].astype(BF16)


def _proj_tile(d):
    return PROJ_TM if d == 1 else max(PROJ_TM_DILATED, d * DIL_BAND)


def _project(x, g, wqT, wk, wvT, tabs, gi, d, na):
    B, S, D = x.shape
    L = S // d
    tm = _proj_tile(d)
    per = tm // d
    cq, sq, ck, s1, s2 = tabs
    grid = (B, S // tm)
    full = lambda a: pl.BlockSpec(a.shape, lambda b, i: (0,) * a.ndim)
    in_specs = [
        pl.BlockSpec((None, tm, D), lambda b, i: (b, i, 0)),
        full(g), full(wqT), full(wk), full(wvT),
        pl.BlockSpec((None, ROT_HALF, tm), lambda b, i: (gi, 0, i)),
        pl.BlockSpec((None, ROT_HALF, tm), lambda b, i: (gi, 0, i)),
        pl.BlockSpec((None, tm, LANES), lambda b, i: (gi, i, 0)),
        pl.BlockSpec((None, tm, LANES), lambda b, i: (gi, i, 0)),
        pl.BlockSpec((None, tm, LANES), lambda b, i: (gi, i, 0)),
    ]
    nsub = per // DIL_BAND
    out_shape = [
        jax.ShapeDtypeStruct((B, d, L // DIL_BAND, WIDTH_G, DIL_BAND), BF16),
        jax.ShapeDtypeStruct((B, d, L // DIL_BAND, WIDTH_G, DIL_BAND), BF16),
        jax.ShapeDtypeStruct((B, d, L, WIDTH_G), BF16),
    ]
    out_specs = [
        pl.BlockSpec((None, d, nsub, WIDTH_G, DIL_BAND), lambda b, i: (b, 0, i, 0, 0)),
        pl.BlockSpec((None, d, nsub, WIDTH_G, DIL_BAND), lambda b, i: (b, 0, i, 0, 0)),
        pl.BlockSpec((None, d, per, WIDTH_G), lambda b, i: (b, 0, i, 0)),
    ]
    if na:
        nblk = tm // MOBA_BLOCK
        out_shape = [
            jax.ShapeDtypeStruct((B, S // MOBA_BLOCK, na, MOBA_BLOCK), BF16),
            jax.ShapeDtypeStruct((B, S // MOBA_BLOCK, na, MOBA_BLOCK), BF16),
            jax.ShapeDtypeStruct((B, S, 2 * na), BF16),
            jax.ShapeDtypeStruct((B, S // tm, nblk, na), F32),
        ] + out_shape
        out_specs = [
            pl.BlockSpec((None, nblk, na, MOBA_BLOCK), lambda b, i: (b, i, 0, 0)),
            pl.BlockSpec((None, nblk, na, MOBA_BLOCK), lambda b, i: (b, i, 0, 0)),
            pl.BlockSpec((None, tm, 2 * na), lambda b, i: (b, i, 0)),
            pl.BlockSpec((None, None, nblk, na), lambda b, i: (b, i, 0, 0)),
        ] + out_specs
    return pl.pallas_call(
        functools.partial(_proj_kernel, na=na, d=d),
        grid=grid, in_specs=in_specs, out_specs=out_specs, out_shape=out_shape,
        compiler_params=_cparams(("arbitrary",) * 2),
        name=f"proj_d{d}",
    )(x, g, wqT, wk, wvT, cq, sq, ck, s1, s2)


def _moba_kernel(q_ref, k_ref, v_ref, km_ref, o_ref, qa_scr, s_scr, p_scr, sel_scr):
    n_super = q_ref.shape[0] // MOBA_QBLOCKS
    _moba_select(q_ref, km_ref, sel_scr)
    _moba_prepare(q_ref, k_ref, sel_scr, qa_scr, s_scr, 0)

    def trip(g, c):
        _moba_attend(k_ref, v_ref, o_ref, qa_scr, s_scr, p_scr, g)
        _moba_prepare(q_ref, k_ref, sel_scr, qa_scr, s_scr, jnp.minimum(g + 1, n_super - 1))
        return c

    lax.fori_loop(0, n_super, trip, 0)


def _moba_select(q_ref, km_ref, sel_scr):
    nb, _, blk = q_ref.shape
    cb = MOBA_GATE_BLOCKS
    wq = cb * blk
    bid = lax.broadcasted_iota(jnp.int32, (nb, wq), 0)
    km = km_ref[...].astype(BF16)
    zq = jnp.zeros((HEAD_DIM, wq), BF16)

    def chunk(c, carry):
        qblk = cb * c + lax.broadcasted_iota(jnp.int32, (nb, wq), 1) // blk
        for h in range(2):
            qh = jnp.concatenate([q_ref[cb * c + i, HEAD_DIM * h:HEAD_DIM * (h + 1), :] for i in range(cb)], axis=1)
            q_plain = jnp.concatenate([qh, zq] if h == 0 else [zq, qh], axis=0)
            gt = jnp.where(bid < qblk, _dot(km, q_plain), -jnp.inf)
            sel = jnp.zeros(gt.shape, jnp.bool_)
            for _ in range(MOBA_TOPK):
                mx = jnp.max(gt, axis=0, keepdims=True)
                first = jnp.min(jnp.where((gt == mx) & (mx > -jnp.inf), bid, nb), axis=0, keepdims=True)
                pick = bid == first
                sel = sel | pick
                gt = jnp.where(pick, -jnp.inf, gt)
            sel_scr[h, :, pl.ds(pl.multiple_of(c * wq, wq), wq)] = sel.astype(F32)
        return carry

    lax.fori_loop(0, nb // cb, chunk, 0)


def _moba_prepare(q_ref, k_ref, sel_scr, qa_scr, s_scr, g):
    nb = q_ref.shape[0]
    blk = MOBA_BLOCK
    tq = MOBA_QBLOCKS * blk
    q2 = jnp.concatenate([q_ref[MOBA_QBLOCKS * g + i] for i in range(MOBA_QBLOCKS)], axis=1)
    bid = lax.broadcasted_iota(jnp.int32, (nb, tq), 0)
    qblk = MOBA_QBLOCKS * g + lax.broadcasted_iota(jnp.int32, (nb, tq), 1) // blk
    key_i = lax.broadcasted_iota(jnp.int32, (blk, tq), 0)
    qry_i = lax.broadcasted_iota(jnp.int32, (blk, tq), 1)
    zb = jnp.zeros((HEAD_DIM - nb, tq), BF16)

    for h in range(2):
        hl = slice(LANES * h, LANES * (h + 1))
        qh = q2[HEAD_DIM * h:HEAD_DIM * (h + 1)]
        sel = sel_scr[h, :, pl.ds(pl.multiple_of(g * tq, tq), tq)] > 0.5
        bias = jnp.where(sel, 0.0, NEG).astype(BF16)
        qa_scr[h] = jnp.concatenate([qh, bias, zb] if h == 0 else [bias, zb, qh], axis=0)
        bias_d = jnp.where(sel | (bid == qblk), 0.0, NEG).astype(BF16)
        q_diag = jnp.concatenate([qh, bias_d, zb] if h == 0 else [bias_d, zb, qh], axis=0)
        for i in range(MOBA_QBLOCKS):
            j = MOBA_QBLOCKS * g + i
            sd = _dot(k_ref[pl.ds(pl.multiple_of(j * blk, blk), blk), hl], q_diag)
            own = (qry_i >= i * blk) & (qry_i < (i + 1) * blk)
            s_scr[h, i] = jnp.where(own & (key_i > qry_i - i * blk), NEG, sd)


def _moba_attend(k_ref, v_ref, o_ref, qa_scr, s_scr, p_scr, g):
    nb = v_ref.shape[0]
    blk = MOBA_BLOCK
    tq = MOBA_QBLOCKS * blk

    def score_blocks(c):
        return [jnp.where(c < g, MOBA_QBLOCKS * c + i, nb - 1) for i in range(MOBA_QBLOCKS)]

    def value_blocks(it):
        return [jnp.where(it <= 1, MOBA_QBLOCKS * g + i, MOBA_QBLOCKS * (it - 2) + i) for i in range(MOBA_QBLOCKS)]

    def values(it, h, alpha, acc):
        vl = slice(HEAD_DIM * h, HEAD_DIM * (h + 1))
        acc = alpha * acc
        for i, j in enumerate(value_blocks(it)):
            acc = acc + _dot(v_ref[j, vl, :], p_scr[h, i])
        return acc

    p_scr[...] = jnp.zeros(p_scr.shape, BF16)

    def body(it, carry):
        alphas, ms, ls, accs = carry
        a_new, m_new, l_new, acc_new = [], [], [], []
        for h in range(2):
            hl = slice(LANES * h, LANES * (h + 1))
            qa = qa_scr[h]
            s_next = [_dot(k_ref[pl.ds(pl.multiple_of(j * blk, blk), blk), hl], qa) for j in score_blocks(it)]
            s_cur = [s_scr[h, i] for i in range(MOBA_QBLOCKS)]
            acc_new.append(values(it, h, alphas[h], accs[h]))
            mn = ms[h]
            for sc in s_cur:
                mn = jnp.maximum(mn, jnp.max(sc, axis=0, keepdims=True))
            alpha = jnp.exp2(ms[h] - mn)
            l = alpha * ls[h]
            for i in range(MOBA_QBLOCKS):
                p = jnp.exp2(s_cur[i] - mn)
                l = l + jnp.sum(p, axis=0, keepdims=True)
                s_scr[h, i] = s_next[i]
                p_scr[h, i] = p.astype(BF16)
            m_new.append(mn)
            l_new.append(l)
            a_new.append(alpha)
        return (tuple(a_new), tuple(m_new), tuple(l_new), tuple(acc_new))

    one = jnp.ones((1, tq), F32)
    neg = jnp.full((1, tq), NEG, F32)
    zl = jnp.zeros((1, tq), F32)
    za = jnp.zeros((HEAD_DIM, tq), F32)
    n_it = g + 1
    n_main = n_it // MOBA_UNROLL

    def trip(t, c):
        for u in range(MOBA_UNROLL):
            c = body(MOBA_UNROLL * t + u, c)
        return c

    carry = lax.fori_loop(0, n_main, trip, ((one, one), (neg, neg), (zl, zl), (za, za)))
    alphas, _, ls, accs = lax.fori_loop(MOBA_UNROLL * n_main, n_it, body, carry)
    accs = [values(n_it, h, alphas[h], accs[h]) for h in range(2)]
    oT = jnp.concatenate([accs[0] / ls[0], accs[1] / ls[1]], axis=0)
    o_ref[pl.ds(pl.multiple_of(g * tq, tq), tq), :] = oT.T.astype(o_ref.dtype)


def _moba(qT, k, vT, kmean):
    B, nb, wa, blk = qT.shape
    S = nb * blk
    hp = wa // LANES
    nq = MOBA_QBLOCKS
    assert nb % 16 == 0 and nb <= HEAD_DIM, "block-mask rows must fit the spare half of a head pair"
    return pl.pallas_call(
        _moba_kernel,
        grid=(B, hp),
        in_specs=[
            pl.BlockSpec((None, nb, LANES, blk), lambda b, p: (b, 0, p, 0)),
            pl.BlockSpec((None, S, 2 * LANES), lambda b, p: (b, 0, p)),
            pl.BlockSpec((None, nb, LANES, blk), lambda b, p: (b, 0, p, 0)),
            pl.BlockSpec((None, nb, LANES), lambda b, p: (b, 0, p)),
        ],
        out_specs=pl.BlockSpec((None, S, LANES), lambda b, p: (b, 0, p)),
        out_shape=jax.ShapeDtypeStruct((B, S, wa), BF16),
        scratch_shapes=[pltpu.VMEM((2, LANES, nq * blk), BF16), pltpu.VMEM((2, nq, blk, nq * blk), F32),
                        pltpu.VMEM((2, nq, blk, nq * blk), BF16), pltpu.VMEM((2, nb, S), F32)],
        compiler_params=_cparams(("arbitrary",) * 2),
        name="moba_attn",
    )(qT, k, vT, kmean)


def _dil_kernel(q_ref, k_ref, kp_ref, v_ref, vp_ref, o_ref, lse_ref):
    t = pl.program_id(2)
    band = DIL_BAND
    nres, nsub = q_ref.shape[0], q_ref.shape[1]
    key_i = lax.broadcasted_iota(jnp.int32, (band, band), 0)
    qry_i = lax.broadcasted_iota(jnp.int32, (band, band), 1)
    rowid = lax.broadcasted_iota(jnp.int32, (LANES, band), 0)
    own_ok = key_i <= qry_i
    prev_ok = key_i >= qry_i
    for r in range(nres):
        for c in range(nsub):
            o_parts, l_parts = [], []
            for hp in range(WIDTH_G // LANES):
                cols = slice(hp * LANES, (hp + 1) * LANES)
                k_own = k_ref[r, c * band:(c + 1) * band, cols]
                k_prev = kp_ref[r, :, cols] if c == 0 else k_ref[r, (c - 1) * band:c * band, cols]
                q2 = q_ref[r, c, cols, :]
                for h in range(2):
                    qh = jnp.where((rowid >= HEAD_DIM * h) & (rowid < HEAD_DIM * (h + 1)), q2, jnp.zeros_like(q2))
                    pmask = (prev_ok & (t > 0)) if c == 0 else prev_ok
                    s_own = jnp.where(own_ok, _dot(k_own, qh), -jnp.inf)
                    s_prev = jnp.where(pmask, _dot(k_prev, qh), -jnp.inf)
                    m = jnp.maximum(jnp.max(s_own, axis=0, keepdims=True), jnp.max(s_prev, axis=0, keepdims=True))
                    p_own = jnp.exp(s_own - m)
                    p_prev = jnp.exp(s_prev - m)
                    l = jnp.sum(p_own, axis=0, keepdims=True) + jnp.sum(p_prev, axis=0, keepdims=True)
                    rows = slice(hp * LANES + h * HEAD_DIM, hp * LANES + (h + 1) * HEAD_DIM)
                    v_own = v_ref[r, c, rows, :]
                    v_prev = vp_ref[r, rows, :] if c == 0 else v_ref[r, c - 1, rows, :]
                    oT = _dot(v_own, p_own.astype(BF16)) + _dot(v_prev, p_prev.astype(BF16))
                    o_parts.append(oT / l)
                    l_parts.append(jnp.broadcast_to(m + jnp.log(l), (HEAD_DIM, band)))
            o_ref[r, c * band:(c + 1) * band, :] = jnp.concatenate(o_parts, axis=0).T
            lse_ref[r, c * band:(c + 1) * band, :] = jnp.concatenate(l_parts, axis=0).T


def _dilated(qT, k, vT):
    B, d, nblk, wg, band = qT.shape
    L = nblk * band
    tl = min(DIL_TL, L)
    nsub = tl // band
    nres = min(d, max(1, DIL_TL // L))
    prev = lambda t: jnp.maximum(t * nsub - 1, 0)
    return pl.pallas_call(
        _dil_kernel,
        grid=(B, d // nres, L // tl),
        in_specs=[
            pl.BlockSpec((None, nres, nsub, wg, band), lambda b, r, t: (b, r, t, 0, 0)),
            pl.BlockSpec((None, nres, tl, wg), lambda b, r, t: (b, r, t, 0)),
            pl.BlockSpec((None, nres, band, wg), lambda b, r, t: (b, r, prev(t), 0)),
            pl.BlockSpec((None, nres, nsub, wg, band), lambda b, r, t: (b, r, t, 0, 0)),
            pl.BlockSpec((None, nres, None, wg, band), lambda b, r, t: (b, r, prev(t), 0, 0)),
        ],
        out_specs=[
            pl.BlockSpec((None, nres, tl, wg), lambda b, r, t: (b, r, t, 0)),
            pl.BlockSpec((None, nres, tl, wg), lambda b, r, t: (b, r, t, 0)),
        ],
        out_shape=[jax.ShapeDtypeStruct((B, d, L, wg), F32)] * 2,
        compiler_params=_cparams(("arbitrary",) * 3),
        name=f"dilated_d{d}",
    )(qT, k, k, vT, vT)


def _interleave(ref, scr):
    d, per, w = ref.shape
    if d == 1:
        return ref[0]
    for r in range(d):
        for sl in range(w // LANES):
            scr[sl, pl.ds(r, per, stride=d), :] = ref[r, :, sl * LANES:(sl + 1) * LANES]
    return jnp.concatenate([scr[sl] for sl in range(w // LANES)], axis=1)


def _post_kernel(x_ref, oa_ref, o1_ref, o2_ref, o3_ref, l1_ref, l2_ref, l3_ref, gmix_ref, wg_ref, bga_ref,
                 bgb_ref, woa_ref, wob_ref, wout_ref, gffn_ref, wr_ref, br_ref,
                 x1_ref, h2_ref, idx_ref, rank_ref, pnat_ref, cnt_ref, carry_scr, il_scr):
    i = pl.program_id(0)
    tm, D = x_ref.shape
    ne = wr_ref.shape[0]

    @pl.when(i == 0)
    def _():
        carry_scr[...] = jnp.zeros_like(carry_scr)

    x = x_ref[...]
    h = _rms(x, gmix_ref[...]).astype(BF16)
    gates = _dot(h, wg_ref[...])
    ga = gates[:, :D] + bga_ref[...]
    gb = gates[:, D:] + bgb_ref[...]
    l1, l2, l3 = [_interleave(r, il_scr.at[n]) for n, r in enumerate((l1_ref, l2_ref, l3_ref))]
    o1, o2, o3 = [_interleave(r, il_scr.at[3 + n]) for n, r in enumerate((o1_ref, o2_ref, o3_ref))]
    mx = jnp.maximum(jnp.maximum(l1, l2), l3)
    e1, e2, e3 = jnp.exp(l1 - mx), jnp.exp(l2 - mx), jnp.exp(l3 - mx)
    ob = (e1 * o1 + e2 * o2 + e3 * o3) / (e1 + e2 + e3)
    ya = _dot(oa_ref[...], woa_ref[...])
    yb = _dot(ob.astype(BF16), wob_ref[...])
    mix = jax.nn.sigmoid(ga) * ya + jax.nn.sigmoid(gb) * yb
    x1 = x + _dot(mix.astype(BF16), wout_ref[...])
    x1_ref[...] = x1
    h2 = _rms(x1, gffn_ref[...])
    _rows_to_tiles(h2_ref, h2)

    wr = wr_ref[...]
    wr_hi = wr.astype(BF16)
    wr_lo = (wr - wr_hi.astype(F32)).astype(BF16)
    h2_hi = h2.astype(BF16)
    h2_lo = (h2 - h2_hi.astype(F32)).astype(BF16)
    logits = _dot_nt(wr_hi, h2_hi) + (_dot_nt(wr_hi, h2_lo) + _dot_nt(wr_lo, h2_hi)) + br_ref[...]
    eid = lax.broadcasted_iota(jnp.int32, (ne, tm), 0)
    g = logits
    vals, idxs, picks = [], [], []
    for _ in range(TOP_K):
        m = jnp.max(g, axis=0, keepdims=True)
        first = jnp.min(jnp.where(g == m, eid, ne), axis=0, keepdims=True)
        pick = eid == first
        vals.append(m)
        idxs.append(first)
        picks.append(pick)
        g = jnp.where(pick, -jnp.inf, g)
    es = [jnp.exp(v - vals[0]) for v in vals]
    den = es[0] + es[1] + es[2] + es[3]
    probs = [e / den for e in es]

    onehot = jnp.zeros((ne, tm), F32)
    for pick in picks:
        onehot = onehot + pick.astype(F32)
    earlier = (lax.broadcasted_iota(jnp.int32, (tm, tm), 0) < lax.broadcasted_iota(jnp.int32, (tm, tm), 1))
    prefix = _dot(onehot.astype(BF16), earlier.astype(BF16)) + carry_scr[:, 0:1]
    ranks = [jnp.sum(jnp.where(pick, prefix, 0.0), axis=0, keepdims=True) for pick in picks]
    carry_scr[...] = carry_scr[...] + jnp.sum(onehot, axis=1, keepdims=True)
    cnt_ref[...] = carry_scr[...]

    zi = jnp.zeros((8 - TOP_K, tm), jnp.int32)
    idx_ref[...] = jnp.concatenate(idxs + [zi], axis=0)
    rank_ref[...] = jnp.concatenate([r.astype(jnp.int32) for r in ranks] + [zi], axis=0)
    pnat_ref[...] = jnp.concatenate(probs + [jnp.zeros((LANES - TOP_K, tm), F32)], axis=0).T


def _post(x2, oa, obs, lses, gmix, wg, bga, bgb, woa, wob, wout, gffn, wrT, br):
    T, D = x2.shape
    tm = POST_TM
    ne = wrT.shape[0]
    row = lambda w: pl.BlockSpec((tm, w), lambda i: (i, 0))

    def grouped(a):
        _, d, L, w = a.shape
        nt = L * d // tm
        return pl.BlockSpec((None, d, tm // d, w), lambda i: (i // nt, 0, i % nt, 0))

    full = lambda a: pl.BlockSpec(a.shape, lambda i: (0,) * a.ndim)
    col = pl.BlockSpec((8, tm), lambda i: (0, i))
    return pl.pallas_call(
        _post_kernel,
        grid=(T // tm,),
        in_specs=[row(D), row(WIDTH_A)] + [grouped(a) for a in (*obs, *lses)]
                 + [full(a) for a in (gmix, wg, bga, bgb, woa, wob, wout, gffn, wrT, br)],
        out_specs=[row(D), pl.BlockSpec((tm * SUBLANES, LANES), lambda i: (i, 0)), col, col, row(LANES),
                   pl.BlockSpec((ne, LANES), lambda i: (0, 0))],
        out_shape=[
            jax.ShapeDtypeStruct((T, D), F32), jax.ShapeDtypeStruct((T * SUBLANES, LANES), F32),
            jax.ShapeDtypeStruct((8, T), jnp.int32), jax.ShapeDtypeStruct((8, T), jnp.int32),
            jax.ShapeDtypeStruct((T, LANES), F32),
            jax.ShapeDtypeStruct((ne, LANES), F32),
        ],
        scratch_shapes=[pltpu.VMEM((ne, LANES), F32), pltpu.VMEM((6, WIDTH_G // LANES, tm, LANES), F32)],
        compiler_params=_cparams(("arbitrary",)),
        name="post_mix_router",
    )(x2, oa, *obs, *lses, gmix, wg, bga, bgb, woa, wob, wout, gffn, wrT, br)


def _dispatch_kernel(pend_ref, padded_ref, dest_ref, h_ref, xr_ref, zero_scr, sem):
    i = pl.program_id(0)
    tm = h_ref.shape[0] // SUBLANES
    br = zero_scr.shape[0]

    def zero_copy(blk_start):
        return pltpu.make_async_copy(zero_scr, xr_ref.at[pl.ds(pl.multiple_of(blk_start * SUBLANES, br), br), :], sem)

    @pl.when(i == 0)
    def _():
        zero_scr[...] = jnp.zeros_like(zero_scr)
        n_used = pend_ref[N_EXPERTS - 1] // EXPERT_ROWS
        n_blk = xr_ref.shape[0] // br

        def tail(fn):
            def go(e, c):
                @pl.when(padded_ref[e] > 0)
                def _():
                    fn(zero_copy(pend_ref[e] - EXPERT_ROWS))
                return c
            return go

        def unused(fn):
            def go(b, c):
                fn(zero_copy(b * EXPERT_ROWS))
                return c
            return go

        lax.fori_loop(0, N_EXPERTS, tail(lambda cp: cp.start()), 0)
        lax.fori_loop(n_used, n_blk, unused(lambda cp: cp.start()), 0)
        lax.fori_loop(0, N_EXPERTS, tail(lambda cp: cp.wait()), 0)
        lax.fori_loop(n_used, n_blk, unused(lambda cp: cp.wait()), 0)

    def row_copy(t, k):
        src = h_ref.at[pl.ds(pl.multiple_of(t * SUBLANES, SUBLANES), SUBLANES), :]
        dst = xr_ref.at[pl.ds(pl.multiple_of(dest_ref[k, t] * SUBLANES, SUBLANES), SUBLANES), :]
        return pltpu.make_async_copy(src, dst, sem)

    def start(tb, c):
        for u in range(ROW_UNROLL):
            for k in range(TOP_K):
                row_copy(tb * ROW_UNROLL + u, k).start(priority=(u * TOP_K + k) % 2)
        return c

    def wait(tb, c):
        for u in range(ROW_UNROLL):
            for k in range(TOP_K):
                row_copy(tb * ROW_UNROLL + u, k).wait()
        return c

    lax.fori_loop(0, tm // ROW_UNROLL, start, 0)
    lax.fori_loop(0, tm // ROW_UNROLL, wait, 0)


def _dispatch(pend, padded, dest3, h2t, n_rows):
    tm = DISPATCH_TM
    return pl.pallas_call(
        _dispatch_kernel,
        grid_spec=pltpu.PrefetchScalarGridSpec(
            num_scalar_prefetch=2,
            grid=(h2t.shape[0] // (tm * SUBLANES),),
            in_specs=[
                pl.BlockSpec((None, TOP_K, tm), lambda i, pe, pa: (i, 0, 0), memory_space=pltpu.SMEM),
                pl.BlockSpec((tm * SUBLANES, LANES), lambda i, pe, pa: (i, 0)),
            ],
            out_specs=pl.BlockSpec(memory_space=pl.ANY),
            scratch_shapes=[pltpu.VMEM((EXPERT_ROWS * SUBLANES, LANES), F32), pltpu.SemaphoreType.DMA(())],
        ),
        out_shape=jax.ShapeDtypeStruct((n_rows * SUBLANES, LANES), F32),
        compiler_params=_cparams(("arbitrary",)),
        name="moe_dispatch",
    )(pend, padded, dest3, h2t)


def _expert_kernel(pstart_ref, padded_ref, x_hbm, wg_ref, bg_ref, wu_ref, bu_ref, wd_ref, bd_ref, y_hbm,
                   w_scr, xbuf, ybuf, xsem, ysem):
    e = pl.program_id(0)
    tb = EXPERT_ROWS * SUBLANES
    nblk = padded_ref[e] // EXPERT_ROWS
    npair = nblk // EXPERT_PAIR
    row0 = pstart_ref[e]

    def rows(first_blk, nb_):
        return pl.ds(pl.multiple_of((row0 + first_blk * EXPERT_ROWS) * SUBLANES, tb), nb_ * tb)

    def x_copy(first_blk, nb_, slot):
        return pltpu.make_async_copy(x_hbm.at[rows(first_blk, nb_), :], xbuf.at[slot, pl.ds(0, nb_ * tb), :],
                                     xsem.at[slot])

    def y_copy(first_blk, nb_, slot):
        return pltpu.make_async_copy(ybuf.at[slot, pl.ds(0, nb_ * tb), :], y_hbm.at[rows(first_blk, nb_), :],
                                     ysem.at[slot])

    def ffn(slot, nb_):
        n = nb_ * EXPERT_ROWS
        x = _tiles_to_rows(xbuf.at[slot, pl.ds(0, nb_ * tb), :], n).astype(BF16)
        g = _dot(x, w_scr[0]) + bg_ref[...]
        u = _dot(x, w_scr[1]) + bu_ref[...]
        g = jnp.minimum(g, SWIGLU_LIMIT)
        u = jnp.clip(u, -SWIGLU_LIMIT, SWIGLU_LIMIT)
        a = g * jax.nn.sigmoid(SWIGLU_ALPHA * g) * (u + 1.0)
        _rows_to_tiles(ybuf.at[slot, pl.ds(0, nb_ * tb), :], _dot(a.astype(BF16), w_scr[2]) + bd_ref[...])

    @pl.when(nblk > 0)
    def _():
        has_tail = nblk > npair * EXPERT_PAIR
        nstep = npair + has_tail.astype(jnp.int32)

        def start_x(s, slot):
            @pl.when(s < npair)
            def _():
                x_copy(s * EXPERT_PAIR, EXPERT_PAIR, slot).start()

            @pl.when((s == npair) & has_tail)
            def _():
                x_copy(s * EXPERT_PAIR, 1, slot).start()

        def wait_y(s, slot):
            @pl.when(s < npair)
            def _():
                y_copy(s * EXPERT_PAIR, EXPERT_PAIR, slot).wait()

            @pl.when((s == npair) & has_tail)
            def _():
                y_copy(s * EXPERT_PAIR, 1, slot).wait()

        start_x(0, 0)
        w_scr[0] = wg_ref[...].astype(BF16)
        w_scr[1] = wu_ref[...].astype(BF16)
        w_scr[2] = wd_ref[...].astype(BF16)

        def body(s, c):
            slot = s % 2
            start_x(s + 1, 1 - slot)
            x_copy(s * EXPERT_PAIR, EXPERT_PAIR, slot).wait()

            @pl.when(s >= 2)
            def _():
                y_copy((s - 2) * EXPERT_PAIR, EXPERT_PAIR, slot).wait()

            ffn(slot, EXPERT_PAIR)
            y_copy(s * EXPERT_PAIR, EXPERT_PAIR, slot).start()
            return c

        lax.fori_loop(0, npair, body, 0)

        @pl.when(has_tail)
        def _():
            slot = npair % 2
            x_copy(npair * EXPERT_PAIR, 1, slot).wait()

            @pl.when(npair >= 2)
            def _():
                y_copy((npair - 2) * EXPERT_PAIR, EXPERT_PAIR, slot).wait()

            ffn(slot, 1)
            y_copy(npair * EXPERT_PAIR, 1, slot).start()

        @pl.when(nstep >= 2)
        def _():
            wait_y(nstep - 2, nstep % 2)

        wait_y(nstep - 1, (nstep - 1) % 2)

    @pl.when(e == pl.num_programs(0) - 1)
    def _():
        n_used = (pstart_ref[e] + padded_ref[e]) // EXPERT_ROWS
        n_all = y_hbm.shape[0] // tb
        ybuf[0, pl.ds(0, tb), :] = jnp.zeros((tb, LANES), F32)

        def zero_copy(b):
            return pltpu.make_async_copy(ybuf.at[0, pl.ds(0, tb), :],
                                         y_hbm.at[pl.ds(pl.multiple_of(b * tb, tb), tb), :], ysem.at[0])

        lax.fori_loop(n_used, n_all, lambda b, c: (zero_copy(b).start(), c)[1], 0)
        lax.fori_loop(n_used, n_all, lambda b, c: (zero_copy(b).wait(), c)[1], 0)


def _experts(pstart, padded, x_rows, w_gate, b_gate, w_up, b_up, w_down, b_down):
    E, D, F = w_gate.shape
    tb = EXPERT_PAIR * EXPERT_ROWS * SUBLANES
    wspec = lambda shape: pl.BlockSpec((None,) + shape, lambda e, ps, pa: (e, 0, 0))
    return pl.pallas_call(
        _expert_kernel,
        grid_spec=pltpu.PrefetchScalarGridSpec(
            num_scalar_prefetch=2,
            grid=(E,),
            in_specs=[
                pl.BlockSpec(memory_space=pl.ANY),
                wspec((D, F)), wspec((1, F)), wspec((D, F)), wspec((1, F)), wspec((F, D)), wspec((1, D)),
            ],
            out_specs=pl.BlockSpec(memory_space=pl.ANY),
            scratch_shapes=[pltpu.VMEM((3, D, F), BF16), pltpu.VMEM((2, tb, LANES), F32),
                            pltpu.VMEM((2, tb, LANES), F32), pltpu.SemaphoreType.DMA((2,)),
                            pltpu.SemaphoreType.DMA((2,))],
        ),
        out_shape=jax.ShapeDtypeStruct(x_rows.shape, F32),
        compiler_params=_cparams(("arbitrary",)),
        name="moe_experts",
    )(pstart, padded, x_rows, w_gate, b_gate.reshape(E, 1, F), w_up, b_up.reshape(E, 1, F),
      w_down, b_down.reshape(E, 1, D))


def _combine_kernel(dest_ref, dnext_ref, y_ref, x1_ref, p_ref, g_ref, o_ref, ybuf, sem):
    i = pl.program_id(0)
    tm = x1_ref.shape[0]
    slot = i % 2

    def row_copy(d_ref, sl, t, k):
        src = y_ref.at[pl.ds(pl.multiple_of(d_ref[k, t] * SUBLANES, SUBLANES), SUBLANES), :]
        dst = ybuf.at[sl, k, pl.ds(pl.multiple_of(t * SUBLANES, SUBLANES), SUBLANES), :]
        return pltpu.make_async_copy(src, dst, sem.at[sl])

    def request(d_ref, sl):
        def go(tb, c):
            for u in range(ROW_UNROLL):
                for k in range(TOP_K):
                    row_copy(d_ref, sl, tb * ROW_UNROLL + u, k).start(priority=(u * TOP_K + k) % 2)
            return c
        lax.fori_loop(0, tm // ROW_UNROLL, go, 0)

    @pl.when(i == 0)
    def _():
        request(dest_ref, 0)

    @pl.when(i + 1 < pl.num_programs(0))
    def _():
        request(dnext_ref, 1 - slot)

    def wait(tb, c):
        for u in range(ROW_UNROLL):
            for k in range(TOP_K):
                row_copy(dest_ref, slot, tb * ROW_UNROLL + u, k).wait()
        return c

    lax.fori_loop(0, tm // ROW_UNROLL, wait, 0)
    p = p_ref[...]
    y = x1_ref[...]
    for k in range(TOP_K):
        y = y + p[:, k:k + 1] * _tiles_to_rows(ybuf.at[slot, k], tm)
    o_ref[...] = _rms(y, g_ref[...])


def _combine(dest3, y_rows, x1, pnat, g_final):
    T, D = x1.shape
    tm = COMBINE_TM
    nt = T // tm
    return pl.pallas_call(
        _combine_kernel,
        grid=(nt,),
        in_specs=[
            pl.BlockSpec((None, TOP_K, tm), lambda i: (i, 0, 0), memory_space=pltpu.SMEM),
            pl.BlockSpec((None, TOP_K, tm), lambda i: (jnp.minimum(i + 1, nt - 1), 0, 0), memory_space=pltpu.SMEM),
            pl.BlockSpec(memory_space=pl.ANY),
            pl.BlockSpec((tm, D), lambda i: (i, 0)),
            pl.BlockSpec((tm, LANES), lambda i: (i, 0)),
            pl.BlockSpec((1, D), lambda i: (0, 0)),
        ],
        out_specs=pl.BlockSpec((tm, D), lambda i: (i, 0)),
        out_shape=jax.ShapeDtypeStruct((T, D), F32),
        scratch_shapes=[pltpu.VMEM((2, TOP_K, tm * SUBLANES, LANES), F32), pltpu.SemaphoreType.DMA((2,))],
        compiler_params=_cparams(("arbitrary",)),
        name="moe_combine",
    )(dest3, dest3, y_rows, x1, pnat, g_final)


def _rope_tables(S, groups):
    ng = len(groups)

    def cos_sin(shape, ax, f):
        rho = lax.broadcasted_iota(jnp.int32, shape, ax)
        gid = lax.broadcasted_iota(jnp.int32, shape, 0)
        pos = jnp.zeros(shape, jnp.int32)
        for n, (d, tm) in enumerate(groups):
            per = tm // d
            w = rho % tm
            pos = jnp.where(gid == n, ((rho // tm) * per + w % per) * d + w // per, pos)
        inv = jnp.float32(ROPE_THETA) ** (-f.astype(F32) / ROT_HALF)
        ang = pos.astype(F32) * inv
        return jnp.cos(ang), jnp.sin(ang)

    sh = (ng, ROT_HALF, S)
    cq, sq = cos_sin(sh, 2, lax.broadcasted_iota(jnp.int32, sh, 1))
    sh = (ng, S, ROT_HALF)
    cos, sin = cos_sin(sh, 1, lax.broadcasted_iota(jnp.int32, sh, 2))
    lane = lax.broadcasted_iota(jnp.int32, (ROT_HALF, LANES), 1) % HEAD_DIM
    f = lax.broadcasted_iota(jnp.int32, (ROT_HALF, LANES), 0)
    lo = ((lane == f)).astype(F32)
    hi = ((lane == f + ROT_HALF)).astype(F32)
    spread = lambda a, m: jnp.dot(a, m, precision=lax.Precision.HIGHEST)
    rest = (lax.broadcasted_iota(jnp.int32, (1, 1, LANES), 2) % HEAD_DIM >= 2 * ROT_HALF).astype(F32)
    ck = spread(cos, lo + hi) + rest
    s1 = spread(-sin, lo)
    s2 = spread(sin, hi)
    return cq, sq, ck, s1, s2


def _layer(x, ln_mix_g, w_in, b_gate_a, b_gate_b, w_o_a, w_o_b, w_out, ln_ffn_g, w_router, b_router,
           w_gate, b_gate, w_up, b_up, w_down, b_down, ln_out_g):
    B, S, D = x.shape
    T = B * S
    qa0, ka0, va0 = 0, WIDTH_A, 2 * WIDTH_A
    qb0, kb0, vb0 = 3 * WIDTH_A, 3 * WIDTH_A + WIDTH_B, 3 * WIDTH_A + 2 * WIDTH_B
    g0 = 3 * WIDTH_A + 3 * WIDTH_B
    cols = lambda s, w: w_in[:, s:s + w]
    gmix = ln_mix_g.reshape(1, D)

    tabs = _rope_tables(S, [(d, _proj_tile(d)) for _, d in DIL_PAIRS])
    obs, lses = [], []
    o_a = None
    for gi, (_, d) in enumerate(DIL_PAIRS):
        off = gi * WIDTH_G
        wq, wk, wv = cols(qb0 + off, WIDTH_G), cols(kb0 + off, WIDTH_G), cols(vb0 + off, WIDTH_G)
        na = 0
        if gi == 0:
            na = WIDTH_A
            wq = jnp.concatenate([cols(qa0, WIDTH_A), wq], axis=1)
            wk = jnp.concatenate([cols(ka0, WIDTH_A), wk], axis=1)
            wv = jnp.concatenate([cols(va0, WIDTH_A), wv], axis=1)
        outs = _project(x, gmix, wq.T.astype(BF16), wk.astype(BF16), wv.T.astype(BF16), tabs, gi, d, na)
        if gi == 0:
            qTa, vTa, ka, kmean = outs[:4]
            outs = outs[4:]
            o_a = _moba(qTa, ka, vTa, kmean.reshape(B, S // MOBA_BLOCK, WIDTH_A))
        qTb, vTb, kb = outs
        o_g, lse_g = _dilated(qTb, kb, vTb)
        obs.append(o_g)
        lses.append(lse_g)

    x1, h2, idxT, rankT, pnat, cnt = _post(
        x.reshape(T, D), o_a.reshape(T, WIDTH_A), obs, lses, gmix,
        cols(g0, 2 * D).astype(BF16), b_gate_a.reshape(1, D), b_gate_b.reshape(1, D),
        w_o_a.astype(BF16), w_o_b.astype(BF16), w_out.astype(BF16), ln_ffn_g.reshape(1, D),
        w_router.T, b_router.reshape(N_EXPERTS, 1))

    br = EXPERT_ROWS
    counts = cnt[:, 0].astype(jnp.int32)
    padded = (counts + br - 1) // br * br
    pend = jnp.cumsum(padded)
    pstart = pend - padded
    eids = jnp.arange(N_EXPERTS, dtype=jnp.int32)[:, None, None]
    dest = jnp.sum(jnp.where(idxT[None, :TOP_K] == eids, pstart[:, None, None], 0), axis=0) + rankT[:TOP_K]
    n_rows = T * TOP_K + N_EXPERTS * br

    dest_d = dest.reshape(TOP_K, T // DISPATCH_TM, DISPATCH_TM).transpose(1, 0, 2)
    x_rows = _dispatch(pend.astype(jnp.int32), padded.astype(jnp.int32), dest_d, h2, n_rows)
    y_rows = _experts(pstart.astype(jnp.int32), padded.astype(jnp.int32), x_rows, w_gate, b_gate, w_up, b_up,
                      w_down, b_down)
    dest_c = dest.reshape(TOP_K, T // COMBINE_TM, COMBINE_TM).transpose(1, 0, 2)
    out = _combine(dest_c, y_rows, x1, pnat, ln_out_g.reshape(1, D))
    return out.reshape(B, S, D)


def kernel(x, ln_mix_g, w_in, b_gate_a, b_gate_b, w_o_a, w_o_b, w_out, ln_ffn_g, w_router, b_router,
           w_gate, b_gate, w_up, b_up, w_down, b_down, ln_final_g):
    depth = ln_mix_g.shape[0]
    assert depth == 1, "the final RMSNorm is fused into the last layer's combine"
    return _layer(x, ln_mix_g[0], w_in[0], b_gate_a[0], b_gate_b[0], w_o_a[0], w_o_b[0], w_out[0],
                  ln_ffn_g[0], w_router[0], b_router[0], w_gate[0], b_gate[0], w_up[0], b_up[0],
                  w_down[0], b_down[0], ln_final_g)
```

```python
import functools

import jax
import jax.numpy as jnp
from jax import lax
from jax.experimental import pallas as pl
from jax.experimental.pallas import tpu as pltpu

D_MODEL = 1024
HEAD_DIM = 64
ROT_HALF = HEAD_DIM // 8
ROPE_THETA = 500000.0
N_HEADS_A = 8
MOBA_BLOCK = 256
MOBA_TOPK = 3
MOBA_QBLOCKS = 2
MOBA_UNROLL = 4
MOBA_GATE_BLOCKS = 8
DIL_PAIRS = ((128, 1), (512, 4), (2048, 16))
DIL_BAND = 128
HEADS_PER_GROUP_B = 4
WIDTH_A = N_HEADS_A * HEAD_DIM
WIDTH_G = HEADS_PER_GROUP_B * HEAD_DIM
WIDTH_B = WIDTH_G * len(DIL_PAIRS)
N_EXPERTS = 32
TOP_K = 4
SWIGLU_LIMIT = 7.0
SWIGLU_ALPHA = 1.702
NORM_EPS = 1e-5
SCALE = HEAD_DIM ** -0.5
LOG2E = 1.4426950408889634

LANES = 128
PROJ_TM = 1024
PROJ_TM_DILATED = 2048
PERM_TILE = 512
DIL_TL = 2048
POST_TM = 512
EXPERT_ROWS = 256
EXPERT_PAIR = 2
DISPATCH_TM = 512
COMBINE_TM = 256
SUBLANES = 8
ROW_UNROLL = 8
VMEM_LIMIT = 56 * 1024 * 1024
NEG = -1e30

BF16 = jnp.bfloat16
F32 = jnp.float32


def _dot(a, b):
    return jnp.dot(a, b, preferred_element_type=F32)


def _dot_nt(a, b):
    return lax.dot_general(a, b, (((1,), (1,)), ((), ())), preferred_element_type=F32)


def _rms(x, g):
    ms = jnp.mean(x * x, axis=-1, keepdims=True)
    return x * lax.rsqrt(ms + NORM_EPS) * g


def _rows_to_tiles(ref, val):
    n = val.shape[0]
    for c in range(val.shape[1] // LANES):
        ref[pl.ds(c, n, stride=SUBLANES), :] = val[:, c * LANES:(c + 1) * LANES]


def _tiles_to_rows(ref, n):
    return jnp.concatenate([ref[pl.ds(c, n, stride=SUBLANES), :] for c in range(SUBLANES)], axis=1)


def _cparams(sem):
    return pltpu.CompilerParams(dimension_semantics=sem, vmem_limit_bytes=VMEM_LIMIT)


def _deinterleave(h, d):
    tm = h.shape[0]
    sub = min(tm, PERM_TILE)
    cs = sub // d
    ri = lax.broadcasted_iota(jnp.int32, (sub, sub), 0)
    ui = lax.broadcasted_iota(jnp.int32, (sub, sub), 1)
    perm = (ui == (ri % cs) * d + ri // cs).astype(BF16)
    slabs = [_dot(perm, h[s0:s0 + sub]).astype(BF16) for s0 in range(0, tm, sub)]
    return jnp.concatenate([sl[r * cs:(r + 1) * cs] for r in range(d) for sl in slabs], axis=0)


def _proj_kernel(x_ref, g_ref, wq_ref, wk_ref, wv_ref, cq_ref, sq_ref, ck_ref, s1_ref, s2_ref, *outs, na, d):
    tm = x_ref.shape[0]
    h = _rms(x_ref[...], g_ref[...]).astype(BF16)
    if d > 1:
        h = _deinterleave(h, d)
    nq = wq_ref.shape[0]
    heads = nq // HEAD_DIM

    qT = _dot_nt(wq_ref[...], h)
    q3 = qT.reshape(heads, HEAD_DIM, tm)
    c = cq_ref[...][None]
    s = sq_ref[...][None]
    x1 = q3[:, 0:ROT_HALF]
    x2 = q3[:, ROT_HALF:2 * ROT_HALF]
    q3 = jnp.concatenate([x1 * c - x2 * s, x2 * c + x1 * s, q3[:, 2 * ROT_HALF:]], axis=1)
    hid = lax.broadcasted_iota(jnp.int32, (heads, 1, 1), 0)
    q3 = q3 * jnp.where(hid < na // HEAD_DIM, SCALE * LOG2E, SCALE)
    qT = q3.reshape(nq, tm).astype(BF16)

    vT = _dot_nt(wv_ref[...], h).astype(BF16)

    kk = _dot(h, wk_ref[...])
    ck, s1, s2 = ck_ref[...], s1_ref[...], s2_ref[...]
    kparts = []
    for gi in range(kk.shape[1] // LANES):
        kg = kk[:, gi * LANES:(gi + 1) * LANES]
        kparts.append(kg * ck + pltpu.roll(kg, LANES - ROT_HALF, 1) * s1 + pltpu.roll(kg, ROT_HALF, 1) * s2)

    if na:
        qa_ref, va_ref, ka_ref, km_ref, qb_ref, vb_ref, kb_ref = outs
        for blk in range(tm // MOBA_BLOCK):
            sl = slice(blk * MOBA_BLOCK, (blk + 1) * MOBA_BLOCK)
            qa_ref[blk] = qT[0:na, sl]
            va_ref[blk] = vT[0:na, sl]
        lane = lax.broadcasted_iota(jnp.int32, (tm, LANES), 1)
        row = lax.broadcasted_iota(jnp.int32, (tm, LANES), 0)
        blkid = pl.program_id(1) * (tm // MOBA_BLOCK) + row // MOBA_BLOCK
        for gi in range(na // LANES):
            kg = kparts[gi]
            for e in range(2):
                in_head = (lane >= HEAD_DIM * e) & (lane < HEAD_DIM * (e + 1))
                onehot = (lane - HEAD_DIM * (1 - e)) == blkid
                col = (2 * gi + e) * LANES
                ka_ref[:, col:col + LANES] = jnp.where(in_head, kg, onehot.astype(F32)).astype(BF16)
            km = kg.reshape(tm // MOBA_BLOCK, MOBA_BLOCK, LANES).sum(axis=1) * (1.0 / MOBA_BLOCK)
            km_ref[:, gi * LANES:(gi + 1) * LANES] = km
    else:
        qb_ref, vb_ref, kb_ref = outs
    per = tm // d
    for r in range(d):
        for cb in range(per // DIL_BAND):
            sl = slice(r * per + cb * DIL_BAND, r * per + (cb + 1) * DIL_BAND)
            qb_ref[r, cb] = qT[na:na + WIDTH_G, sl]
            vb_ref[r, cb] = vT[na:na + WIDTH_G, sl]
        for gi in range(WIDTH_G // LANES):
            kb_ref[r, :, gi * LANES:(gi + 1) * LANES] = kparts[na // LANES + gi][r * per:(r + 1) * per].astype(BF16)


def _proj_tile(d):
    return PROJ_TM if d == 1 else max(PROJ_TM_DILATED, d * DIL_BAND)


def _project(x, g, wqT, wk, wvT, tabs, gi, d, na):
    B, S, D = x.shape
    L = S // d
    tm = _proj_tile(d)
    per = tm // d
    cq, sq, ck, s1, s2 = tabs
    grid = (B, S // tm)
    full = lambda a: pl.BlockSpec(a.shape, lambda b, i: (0,) * a.ndim)
    in_specs = [
        pl.BlockSpec((None, tm, D), lambda b, i: (b, i, 0)),
        full(g), full(wqT), full(wk), full(wvT),
        pl.BlockSpec((None, ROT_HALF, tm), lambda b, i: (gi, 0, i)),
        pl.BlockSpec((None, ROT_HALF, tm), lambda b, i: (gi, 0, i)),
        pl.BlockSpec((None, tm, LANES), lambda b, i: (gi, i, 0)),
        pl.BlockSpec((None, tm, LANES), lambda b, i: (gi, i, 0)),
        pl.BlockSpec((None, tm, LANES), lambda b, i: (gi, i, 0)),
    ]
    nsub = per // DIL_BAND
    out_shape = [
        jax.ShapeDtypeStruct((B, d, L // DIL_BAND, WIDTH_G, DIL_BAND), BF16),
        jax.ShapeDtypeStruct((B, d, L // DIL_BAND, WIDTH_G, DIL_BAND), BF16),
        jax.ShapeDtypeStruct((B, d, L, WIDTH_G), BF16),
    ]
    out_specs = [
        pl.BlockSpec((None, d, nsub, WIDTH_G, DIL_BAND), lambda b, i: (b, 0, i, 0, 0)),
        pl.BlockSpec((None, d, nsub, WIDTH_G, DIL_BAND), lambda b, i: (b, 0, i, 0, 0)),
        pl.BlockSpec((None, d, per, WIDTH_G), lambda b, i: (b, 0, i, 0)),
    ]
    if na:
        nblk = tm // MOBA_BLOCK
        out_shape = [
            jax.ShapeDtypeStruct((B, S // MOBA_BLOCK, na, MOBA_BLOCK), BF16),
            jax.ShapeDtypeStruct((B, S // MOBA_BLOCK, na, MOBA_BLOCK), BF16),
            jax.ShapeDtypeStruct((B, S, 2 * na), BF16),
            jax.ShapeDtypeStruct((B, S // tm, nblk, na), F32),
        ] + out_shape
        out_specs = [
            pl.BlockSpec((None, nblk, na, MOBA_BLOCK), lambda b, i: (b, i, 0, 0)),
            pl.BlockSpec((None, nblk, na, MOBA_BLOCK), lambda b, i: (b, i, 0, 0)),
            pl.BlockSpec((None, tm, 2 * na), lambda b, i: (b, i, 0)),
            pl.BlockSpec((None, None, nblk, na), lambda b, i: (b, i, 0, 0)),
        ] + out_specs
    return pl.pallas_call(
        functools.partial(_proj_kernel, na=na, d=d),
        grid=grid, in_specs=in_specs, out_specs=out_specs, out_shape=out_shape,
        compiler_params=_cparams(("arbitrary",) * 2),
        name=f"proj_d{d}",
    )(x, g, wqT, wk, wvT, cq, sq, ck, s1, s2)


def _moba_kernel(q_ref, k_ref, v_ref, km_ref, o_ref, qa_scr, s_scr, p_scr, sel_scr):
    n_super = q_ref.shape[0] // MOBA_QBLOCKS
    _moba_select(q_ref, km_ref, sel_scr)
    _moba_prepare(q_ref, k_ref, sel_scr, qa_scr, s_scr, 0)

    def trip(g, c):
        _moba_attend(k_ref, v_ref, o_ref, qa_scr, s_scr, p_scr, g)
        _moba_prepare(q_ref, k_ref, sel_scr, qa_scr, s_scr, jnp.minimum(g + 1, n_super - 1))
        return c

    lax.fori_loop(0, n_super, trip, 0)


def _moba_select(q_ref, km_ref, sel_scr):
    nb, _, blk = q_ref.shape
    cb = MOBA_GATE_BLOCKS
    wq = cb * blk
    bid = lax.broadcasted_iota(jnp.int32, (nb, wq), 0)
    km = km_ref[...].astype(BF16)
    zq = jnp.zeros((HEAD_DIM, wq), BF16)

    def chunk(c, carry):
        qblk = cb * c + lax.broadcasted_iota(jnp.int32, (nb, wq), 1) // blk
        for h in range(2):
            qh = jnp.concatenate([q_ref[cb * c + i, HEAD_DIM * h:HEAD_DIM * (h + 1), :] for i in range(cb)], axis=1)
            q_plain = jnp.concatenate([qh, zq] if h == 0 else [zq, qh], axis=0)
            gt = jnp.where(bid < qblk, _dot(km, q_plain), -jnp.inf)
            sel = jnp.zeros(gt.shape, jnp.bool_)
            for _ in range(MOBA_TOPK):
                mx = jnp.max(gt, axis=0, keepdims=True)
                first = jnp.min(jnp.where((gt == mx) & (mx > -jnp.inf), bid, nb), axis=0, keepdims=True)
                pick = bid == first
                sel = sel | pick
                gt = jnp.where(pick, -jnp.inf, gt)
            sel_scr[h, :, pl.ds(pl.multiple_of(c * wq, wq), wq)] = sel.astype(F32)
        return carry

    lax.fori_loop(0, nb // cb, chunk, 0)


def _moba_prepare(q_ref, k_ref, sel_scr, qa_scr, s_scr, g):
    nb = q_ref.shape[0]
    blk = MOBA_BLOCK
    tq = MOBA_QBLOCKS * blk
    q2 = jnp.concatenate([q_ref[MOBA_QBLOCKS * g + i] for i in range(MOBA_QBLOCKS)], axis=1)
    bid = lax.broadcasted_iota(jnp.int32, (nb, tq), 0)
    qblk = MOBA_QBLOCKS * g + lax.broadcasted_iota(jnp.int32, (nb, tq), 1) // blk
    key_i = lax.broadcasted_iota(jnp.int32, (blk, tq), 0)
    qry_i = lax.broadcasted_iota(jnp.int32, (blk, tq), 1)
    zb = jnp.zeros((HEAD_DIM - nb, tq), BF16)

    for h in range(2):
        hl = slice(LANES * h, LANES * (h + 1))
        qh = q2[HEAD_DIM * h:HEAD_DIM * (h + 1)]
        sel = sel_scr[h, :, pl.ds(pl.multiple_of(g * tq, tq), tq)] > 0.5
        bias = jnp.where(sel, 0.0, NEG).astype(BF16)
        qa_scr[h] = jnp.concatenate([qh, bias, zb] if h == 0 else [bias, zb, qh], axis=0)
        bias_d = jnp.where(sel | (bid == qblk), 0.0, NEG).astype(BF16)
        q_diag = jnp.concatenate([qh, bias_d, zb] if h == 0 else [bias_d, zb, qh], axis=0)
        for i in range(MOBA_QBLOCKS):
            j = MOBA_QBLOCKS * g + i
            sd = _dot(k_ref[pl.ds(pl.multiple_of(j * blk, blk), blk), hl], q_diag)
            own = (qry_i >= i * blk) & (qry_i < (i + 1) * blk)
            s_scr[h, i] = jnp.where(own & (key_i > qry_i - i * blk), NEG, sd)


def _moba_attend(k_ref, v_ref, o_ref, qa_scr, s_scr, p_scr, g):
    nb = v_ref.shape[0]
    blk = MOBA_BLOCK
    tq = MOBA_QBLOCKS * blk

    def score_blocks(c):
        return [jnp.where(c < g, MOBA_QBLOCKS * c + i, nb - 1) for i in range(MOBA_QBLOCKS)]

    def value_blocks(it):
        return [jnp.where(it <= 1, MOBA_QBLOCKS * g + i, MOBA_QBLOCKS * (it - 2) + i) for i in range(MOBA_QBLOCKS)]

    def values(it, h, alpha, acc):
        vl = slice(HEAD_DIM * h, HEAD_DIM * (h + 1))
        acc = alpha * acc
        for i, j in enumerate(value_blocks(it)):
            acc = acc + _dot(v_ref[j, vl, :], p_scr[h, i])
        return acc

    p_scr[...] = jnp.zeros(p_scr.shape, BF16)

    def body(it, carry):
        alphas, ms, ls, accs = carry
        a_new, m_new, l_new, acc_new = [], [], [], []
        for h in range(2):
            hl = slice(LANES * h, LANES * (h + 1))
            qa = qa_scr[h]
            s_next = [_dot(k_ref[pl.ds(pl.multiple_of(j * blk, blk), blk), hl], qa) for j in score_blocks(it)]
            s_cur = [s_scr[h, i] for i in range(MOBA_QBLOCKS)]
            acc_new.append(values(it, h, alphas[h], accs[h]))
            mn = ms[h]
            for sc in s_cur:
                mn = jnp.maximum(mn, jnp.max(sc, axis=0, keepdims=True))
            alpha = jnp.exp2(ms[h] - mn)
            l = alpha * ls[h]
            for i in range(MOBA_QBLOCKS):
                p = jnp.exp2(s_cur[i] - mn)
                l = l + jnp.sum(p, axis=0, keepdims=True)
                s_scr[h, i] = s_next[i]
                p_scr[h, i] = p.astype(BF16)
            m_new.append(mn)
            l_new.append(l)
            a_new.append(alpha)
        return (tuple(a_new), tuple(m_new), tuple(l_new), tuple(acc_new))

    one = jnp.ones((1, tq), F32)
    neg = jnp.full((1, tq), NEG, F32)
    zl = jnp.zeros((1, tq), F32)
    za = jnp.zeros((HEAD_DIM, tq), F32)
    n_trips = (g + MOBA_UNROLL) // MOBA_UNROLL

    def trip(t, c):
        for u in range(MOBA_UNROLL):
            c = body(MOBA_UNROLL * t + u, c)
        return c

    alphas, _, ls, accs = lax.fori_loop(0, n_trips, trip, ((one, one), (neg, neg), (zl, zl), (za, za)))
    accs = [values(MOBA_UNROLL * n_trips, h, alphas[h], accs[h]) for h in range(2)]
    oT = jnp.concatenate([accs[0] / ls[0], accs[1] / ls[1]], axis=0)
    o_ref[pl.ds(pl.multiple_of(g * tq, tq), tq), :] = oT.T.astype(o_ref.dtype)


def _moba(qT, k, vT, kmean):
    B, nb, wa, blk = qT.shape
    S = nb * blk
    hp = wa // LANES
    nq = MOBA_QBLOCKS
    assert nb % 16 == 0 and nb <= HEAD_DIM, "block-mask rows must fit the spare half of a head pair"
    assert (nb // nq) % MOBA_UNROLL == 0, "neutral pipeline iterations must stay inside the sequence"
    return pl.pallas_call(
        _moba_kernel,
        grid=(B, hp),
        in_specs=[
            pl.BlockSpec((None, nb, LANES, blk), lambda b, p: (b, 0, p, 0)),
            pl.BlockSpec((None, S, 2 * LANES), lambda b, p: (b, 0, p)),
            pl.BlockSpec((None, nb, LANES, blk), lambda b, p: (b, 0, p, 0)),
            pl.BlockSpec((None, nb, LANES), lambda b, p: (b, 0, p)),
        ],
        out_specs=pl.BlockSpec((None, S, LANES), lambda b, p: (b, 0, p)),
        out_shape=jax.ShapeDtypeStruct((B, S, wa), BF16),
        scratch_shapes=[pltpu.VMEM((2, LANES, nq * blk), BF16), pltpu.VMEM((2, nq, blk, nq * blk), F32),
                        pltpu.VMEM((2, nq, blk, nq * blk), BF16), pltpu.VMEM((2, nb, S), F32)],
        compiler_params=_cparams(("arbitrary",) * 2),
        name="moba_attn",
    )(qT, k, vT, kmean)


def _dil_kernel(q_ref, k_ref, kp_ref, v_ref, vp_ref, o_ref, lse_ref):
    t = pl.program_id(2)
    band = DIL_BAND
    nres, nsub = q_ref.shape[0], q_ref.shape[1]
    key_i = lax.broadcasted_iota(jnp.int32, (band, band), 0)
    qry_i = lax.broadcasted_iota(jnp.int32, (band, band), 1)
    rowid = lax.broadcasted_iota(jnp.int32, (LANES, band), 0)
    own_ok = key_i <= qry_i
    prev_ok = key_i >= qry_i
    for r in range(nres):
        for c in range(nsub):
            o_parts, l_parts = [], []
            for hp in range(WIDTH_G // LANES):
                cols = slice(hp * LANES, (hp + 1) * LANES)
                k_own = k_ref[r, c * band:(c + 1) * band, cols]
                k_prev = kp_ref[r, :, cols] if c == 0 else k_ref[r, (c - 1) * band:c * band, cols]
                q2 = q_ref[r, c, cols, :]
                for h in range(2):
                    qh = jnp.where((rowid >= HEAD_DIM * h) & (rowid < HEAD_DIM * (h + 1)), q2, jnp.zeros_like(q2))
                    pmask = (prev_ok & (t > 0)) if c == 0 else prev_ok
                    s_own = jnp.where(own_ok, _dot(k_own, qh), -jnp.inf)
                    s_prev = jnp.where(pmask, _dot(k_prev, qh), -jnp.inf)
                    m = jnp.maximum(jnp.max(s_own, axis=0, keepdims=True), jnp.max(s_prev, axis=0, keepdims=True))
                    p_own = jnp.exp(s_own - m)
                    p_prev = jnp.exp(s_prev - m)
                    l = jnp.sum(p_own, axis=0, keepdims=True) + jnp.sum(p_prev, axis=0, keepdims=True)
                    rows = slice(hp * LANES + h * HEAD_DIM, hp * LANES + (h + 1) * HEAD_DIM)
                    v_own = v_ref[r, c, rows, :]
                    v_prev = vp_ref[r, rows, :] if c == 0 else v_ref[r, c - 1, rows, :]
                    oT = _dot(v_own, p_own.astype(BF16)) + _dot(v_prev, p_prev.astype(BF16))
                    o_parts.append(oT / l)
                    l_parts.append(jnp.broadcast_to(m + jnp.log(l), (HEAD_DIM, band)))
            o_ref[r, c * band:(c + 1) * band, :] = jnp.concatenate(o_parts, axis=0).T
            lse_ref[r, c * band:(c + 1) * band, :] = jnp.concatenate(l_parts, axis=0).T


def _dilated(qT, k, vT):
    B, d, nblk, wg, band = qT.shape
    L = nblk * band
    tl = min(DIL_TL, L)
    nsub = tl // band
    nres = min(d, max(1, DIL_TL // L))
    prev = lambda t: jnp.maximum(t * nsub - 1, 0)
    return pl.pallas_call(
        _dil_kernel,
        grid=(B, d // nres, L // tl),
        in_specs=[
            pl.BlockSpec((None, nres, nsub, wg, band), lambda b, r, t: (b, r, t, 0, 0)),
            pl.BlockSpec((None, nres, tl, wg), lambda b, r, t: (b, r, t, 0)),
            pl.BlockSpec((None, nres, band, wg), lambda b, r, t: (b, r, prev(t), 0)),
            pl.BlockSpec((None, nres, nsub, wg, band), lambda b, r, t: (b, r, t, 0, 0)),
            pl.BlockSpec((None, nres, None, wg, band), lambda b, r, t: (b, r, prev(t), 0, 0)),
        ],
        out_specs=[
            pl.BlockSpec((None, nres, tl, wg), lambda b, r, t: (b, r, t, 0)),
            pl.BlockSpec((None, nres, tl, wg), lambda b, r, t: (b, r, t, 0)),
        ],
        out_shape=[jax.ShapeDtypeStruct((B, d, L, wg), F32)] * 2,
        compiler_params=_cparams(("arbitrary",) * 3),
        name=f"dilated_d{d}",
    )(qT, k, k, vT, vT)


def _interleave(ref, scr):
    d, per, w = ref.shape
    if d == 1:
        return ref[0]
    for r in range(d):
        for sl in range(w // LANES):
            scr[sl, pl.ds(r, per, stride=d), :] = ref[r, :, sl * LANES:(sl + 1) * LANES]
    return jnp.concatenate([scr[sl] for sl in range(w // LANES)], axis=1)


def _post_kernel(x_ref, oa_ref, o1_ref, o2_ref, o3_ref, l1_ref, l2_ref, l3_ref, gmix_ref, wg_ref, bga_ref,
                 bgb_ref, woa_ref, wob_ref, wout_ref, gffn_ref, wr_ref, br_ref,
                 x1_ref, h2_ref, idx_ref, rank_ref, pnat_ref, cnt_ref, carry_scr, il_scr):
    i = pl.program_id(0)
    tm, D = x_ref.shape
    ne = wr_ref.shape[0]

    @pl.when(i == 0)
    def _():
        carry_scr[...] = jnp.zeros_like(carry_scr)

    x = x_ref[...]
    h = _rms(x, gmix_ref[...]).astype(BF16)
    gates = _dot(h, wg_ref[...])
    ga = gates[:, :D] + bga_ref[...]
    gb = gates[:, D:] + bgb_ref[...]
    l1, l2, l3 = [_interleave(r, il_scr.at[n]) for n, r in enumerate((l1_ref, l2_ref, l3_ref))]
    o1, o2, o3 = [_interleave(r, il_scr.at[3 + n]) for n, r in enumerate((o1_ref, o2_ref, o3_ref))]
    mx = jnp.maximum(jnp.maximum(l1, l2), l3)
    e1, e2, e3 = jnp.exp(l1 - mx), jnp.exp(l2 - mx), jnp.exp(l3 - mx)
    ob = (e1 * o1 + e2 * o2 + e3 * o3) / (e1 + e2 + e3)
    ya = _dot(oa_ref[...], woa_ref[...])
    yb = _dot(ob.astype(BF16), wob_ref[...])
    mix = jax.nn.sigmoid(ga) * ya + jax.nn.sigmoid(gb) * yb
    x1 = x + _dot(mix.astype(BF16), wout_ref[...])
    x1_ref[...] = x1
    h2 = _rms(x1, gffn_ref[...])
    _rows_to_tiles(h2_ref, h2)

    wr = wr_ref[...]
    wr_hi = wr.astype(BF16)
    wr_lo = (wr - wr_hi.astype(F32)).astype(BF16)
    h2_hi = h2.astype(BF16)
    h2_lo = (h2 - h2_hi.astype(F32)).astype(BF16)
    logits = _dot_nt(wr_hi, h2_hi) + (_dot_nt(wr_hi, h2_lo) + _dot_nt(wr_lo, h2_hi)) + br_ref[...]
    eid = lax.broadcasted_iota(jnp.int32, (ne, tm), 0)
    g = logits
    vals, idxs, picks = [], [], []
    for _ in range(TOP_K):
        m = jnp.max(g, axis=0, keepdims=True)
        first = jnp.min(jnp.where(g == m, eid, ne), axis=0, keepdims=True)
        pick = eid == first
        vals.append(m)
        idxs.append(first)
        picks.append(pick)
        g = jnp.where(pick, -jnp.inf, g)
    es = [jnp.exp(v - vals[0]) for v in vals]
    den = es[0] + es[1] + es[2] + es[3]
    probs = [e / den for e in es]

    onehot = jnp.zeros((ne, tm), F32)
    for pick in picks:
        onehot = onehot + pick.astype(F32)
    earlier = (lax.broadcasted_iota(jnp.int32, (tm, tm), 0) < lax.broadcasted_iota(jnp.int32, (tm, tm), 1))
    prefix = _dot(onehot.astype(BF16), earlier.astype(BF16)) + carry_scr[:, 0:1]
    ranks = [jnp.sum(jnp.where(pick, prefix, 0.0), axis=0, keepdims=True) for pick in picks]
    carry_scr[...] = carry_scr[...] + jnp.sum(onehot, axis=1, keepdims=True)
    cnt_ref[...] = carry_scr[...]

    zi = jnp.zeros((8 - TOP_K, tm), jnp.int32)
    idx_ref[...] = jnp.concatenate(idxs + [zi], axis=0)
    rank_ref[...] = jnp.concatenate([r.astype(jnp.int32) for r in ranks] + [zi], axis=0)
    pnat_ref[...] = jnp.concatenate(probs + [jnp.zeros((LANES - TOP_K, tm), F32)], axis=0).T


def _post(x2, oa, obs, lses, gmix, wg, bga, bgb, woa, wob, wout, gffn, wrT, br):
    T, D = x2.shape
    tm = POST_TM
    ne = wrT.shape[0]
    row = lambda w: pl.BlockSpec((tm, w), lambda i: (i, 0))

    def grouped(a):
        _, d, L, w = a.shape
        nt = L * d // tm
        return pl.BlockSpec((None, d, tm // d, w), lambda i: (i // nt, 0, i % nt, 0))

    full = lambda a: pl.BlockSpec(a.shape, lambda i: (0,) * a.ndim)
    col = pl.BlockSpec((8, tm), lambda i: (0, i))
    return pl.pallas_call(
        _post_kernel,
        grid=(T // tm,),
        in_specs=[row(D), row(WIDTH_A)] + [grouped(a) for a in (*obs, *lses)]
                 + [full(a) for a in (gmix, wg, bga, bgb, woa, wob, wout, gffn, wrT, br)],
        out_specs=[row(D), pl.BlockSpec((tm * SUBLANES, LANES), lambda i: (i, 0)), col, col, row(LANES),
                   pl.BlockSpec((ne, LANES), lambda i: (0, 0))],
        out_shape=[
            jax.ShapeDtypeStruct((T, D), F32), jax.ShapeDtypeStruct((T * SUBLANES, LANES), F32),
            jax.ShapeDtypeStruct((8, T), jnp.int32), jax.ShapeDtypeStruct((8, T), jnp.int32),
            jax.ShapeDtypeStruct((T, LANES), F32),
            jax.ShapeDtypeStruct((ne, LANES), F32),
        ],
        scratch_shapes=[pltpu.VMEM((ne, LANES), F32), pltpu.VMEM((6, WIDTH_G // LANES, tm, LANES), F32)],
        compiler_params=_cparams(("arbitrary",)),
        name="post_mix_router",
    )(x2, oa, *obs, *lses, gmix, wg, bga, bgb, woa, wob, wout, gffn, wrT, br)


def _dispatch_kernel(pend_ref, padded_ref, dest_ref, h_ref, xr_ref, zero_scr, sem):
    i = pl.program_id(0)
    tm = h_ref.shape[0] // SUBLANES
    br = zero_scr.shape[0]

    def zero_copy(blk_start):
        return pltpu.make_async_copy(zero_scr, xr_ref.at[pl.ds(pl.multiple_of(blk_start * SUBLANES, br), br), :], sem)

    @pl.when(i == 0)
    def _():
        zero_scr[...] = jnp.zeros_like(zero_scr)
        n_used = pend_ref[N_EXPERTS - 1] // EXPERT_ROWS
        n_blk = xr_ref.shape[0] // br

        def tail(fn):
            def go(e, c):
                @pl.when(padded_ref[e] > 0)
                def _():
                    fn(zero_copy(pend_ref[e] - EXPERT_ROWS))
                return c
            return go

        def unused(fn):
            def go(b, c):
                fn(zero_copy(b * EXPERT_ROWS))
                return c
            return go

        lax.fori_loop(0, N_EXPERTS, tail(lambda cp: cp.start()), 0)
        lax.fori_loop(n_used, n_blk, unused(lambda cp: cp.start()), 0)
        lax.fori_loop(0, N_EXPERTS, tail(lambda cp: cp.wait()), 0)
        lax.fori_loop(n_used, n_blk, unused(lambda cp: cp.wait()), 0)

    def row_copy(t, k):
        src = h_ref.at[pl.ds(pl.multiple_of(t * SUBLANES, SUBLANES), SUBLANES), :]
        dst = xr_ref.at[pl.ds(pl.multiple_of(dest_ref[k, t] * SUBLANES, SUBLANES), SUBLANES), :]
        return pltpu.make_async_copy(src, dst, sem)

    def start(tb, c):
        for u in range(ROW_UNROLL):
            for k in range(TOP_K):
                row_copy(tb * ROW_UNROLL + u, k).start(priority=(u * TOP_K + k) % 2)
        return c

    def wait(tb, c):
        for u in range(ROW_UNROLL):
            for k in range(TOP_K):
                row_copy(tb * ROW_UNROLL + u, k).wait()
        return c

    lax.fori_loop(0, tm // ROW_UNROLL, start, 0)
    lax.fori_loop(0, tm // ROW_UNROLL, wait, 0)


def _dispatch(pend, padded, dest3, h2t, n_rows):
    tm = DISPATCH_TM
    return pl.pallas_call(
        _dispatch_kernel,
        grid_spec=pltpu.PrefetchScalarGridSpec(
            num_scalar_prefetch=2,
            grid=(h2t.shape[0] // (tm * SUBLANES),),
            in_specs=[
                pl.BlockSpec((None, TOP_K, tm), lambda i, pe, pa: (i, 0, 0), memory_space=pltpu.SMEM),
                pl.BlockSpec((tm * SUBLANES, LANES), lambda i, pe, pa: (i, 0)),
            ],
            out_specs=pl.BlockSpec(memory_space=pl.ANY),
            scratch_shapes=[pltpu.VMEM((EXPERT_ROWS * SUBLANES, LANES), F32), pltpu.SemaphoreType.DMA(())],
        ),
        out_shape=jax.ShapeDtypeStruct((n_rows * SUBLANES, LANES), F32),
        compiler_params=_cparams(("arbitrary",)),
        name="moe_dispatch",
    )(pend, padded, dest3, h2t)


def _expert_kernel(pstart_ref, padded_ref, x_hbm, wg_ref, bg_ref, wu_ref, bu_ref, wd_ref, bd_ref, y_hbm,
                   w_scr, xbuf, ybuf, xsem, ysem):
    e = pl.program_id(0)
    tb = EXPERT_ROWS * SUBLANES
    nblk = padded_ref[e] // EXPERT_ROWS
    npair = nblk // EXPERT_PAIR
    row0 = pstart_ref[e]

    def rows(first_blk, nb_):
        return pl.ds(pl.multiple_of((row0 + first_blk * EXPERT_ROWS) * SUBLANES, tb), nb_ * tb)

    def x_copy(first_blk, nb_, slot):
        return pltpu.make_async_copy(x_hbm.at[rows(first_blk, nb_), :], xbuf.at[slot, pl.ds(0, nb_ * tb), :],
                                     xsem.at[slot])

    def y_copy(first_blk, nb_, slot):
        return pltpu.make_async_copy(ybuf.at[slot, pl.ds(0, nb_ * tb), :], y_hbm.at[rows(first_blk, nb_), :],
                                     ysem.at[slot])

    def ffn(slot, nb_):
        n = nb_ * EXPERT_ROWS
        x = _tiles_to_rows(xbuf.at[slot, pl.ds(0, nb_ * tb), :], n).astype(BF16)
        g = _dot(x, w_scr[0]) + bg_ref[...]
        u = _dot(x, w_scr[1]) + bu_ref[...]
        g = jnp.minimum(g, SWIGLU_LIMIT)
        u = jnp.clip(u, -SWIGLU_LIMIT, SWIGLU_LIMIT)
        a = g * jax.nn.sigmoid(SWIGLU_ALPHA * g) * (u + 1.0)
        _rows_to_tiles(ybuf.at[slot, pl.ds(0, nb_ * tb), :], _dot(a.astype(BF16), w_scr[2]) + bd_ref[...])

    @pl.when(nblk > 0)
    def _():
        has_tail = nblk > npair * EXPERT_PAIR
        nstep = npair + has_tail.astype(jnp.int32)

        def start_x(s, slot):
            @pl.when(s < npair)
            def _():
                x_copy(s * EXPERT_PAIR, EXPERT_PAIR, slot).start()

            @pl.when((s == npair) & has_tail)
            def _():
                x_copy(s * EXPERT_PAIR, 1, slot).start()

        def wait_y(s, slot):
            @pl.when(s < npair)
            def _():
                y_copy(s * EXPERT_PAIR, EXPERT_PAIR, slot).wait()

            @pl.when((s == npair) & has_tail)
            def _():
                y_copy(s * EXPERT_PAIR, 1, slot).wait()

        start_x(0, 0)
        w_scr[0] = wg_ref[...].astype(BF16)
        w_scr[1] = wu_ref[...].astype(BF16)
        w_scr[2] = wd_ref[...].astype(BF16)

        def body(s, c):
            slot = s % 2
            start_x(s + 1, 1 - slot)
            x_copy(s * EXPERT_PAIR, EXPERT_PAIR, slot).wait()

            @pl.when(s >= 2)
            def _():
                y_copy((s - 2) * EXPERT_PAIR, EXPERT_PAIR, slot).wait()

            ffn(slot, EXPERT_PAIR)
            y_copy(s * EXPERT_PAIR, EXPERT_PAIR, slot).start()
            return c

        lax.fori_loop(0, npair, body, 0)

        @pl.when(has_tail)
        def _():
            slot = npair % 2
            x_copy(npair * EXPERT_PAIR, 1, slot).wait()

            @pl.when(npair >= 2)
            def _():
                y_copy((npair - 2) * EXPERT_PAIR, EXPERT_PAIR, slot).wait()

            ffn(slot, 1)
            y_copy(npair * EXPERT_PAIR, 1, slot).start()

        @pl.when(nstep >= 2)
        def _():
            wait_y(nstep - 2, nstep % 2)

        wait_y(nstep - 1, (nstep - 1) % 2)

    @pl.when(e == pl.num_programs(0) - 1)
    def _():
        n_used = (pstart_ref[e] + padded_ref[e]) // EXPERT_ROWS
        n_all = y_hbm.shape[0] // tb
        ybuf[0, pl.ds(0, tb), :] = jnp.zeros((tb, LANES), F32)

        def zero_copy(b):
            return pltpu.make_async_copy(ybuf.at[0, pl.ds(0, tb), :],
                                         y_hbm.at[pl.ds(pl.multiple_of(b * tb, tb), tb), :], ysem.at[0])

        lax.fori_loop(n_used, n_all, lambda b, c: (zero_copy(b).start(), c)[1], 0)
        lax.fori_loop(n_used, n_all, lambda b, c: (zero_copy(b).wait(), c)[1], 0)


def _experts(pstart, padded, x_rows, w_gate, b_gate, w_up, b_up, w_down, b_down):
    E, D, F = w_gate.shape
    tb = EXPERT_PAIR * EXPERT_ROWS * SUBLANES
    wspec = lambda shape: pl.BlockSpec((None,) + shape, lambda e, ps, pa: (e, 0, 0))
    return pl.pallas_call(
        _expert_kernel,
        grid_spec=pltpu.PrefetchScalarGridSpec(
            num_scalar_prefetch=2,
            grid=(E,),
            in_specs=[
                pl.BlockSpec(memory_space=pl.ANY),
                wspec((D, F)), wspec((1, F)), wspec((D, F)), wspec((1, F)), wspec((F, D)), wspec((1, D)),
            ],
            out_specs=pl.BlockSpec(memory_space=pl.ANY),
            scratch_shapes=[pltpu.VMEM((3, D, F), BF16), pltpu.VMEM((2, tb, LANES), F32),
                            pltpu.VMEM((2, tb, LANES), F32), pltpu.SemaphoreType.DMA((2,)),
                            pltpu.SemaphoreType.DMA((2,))],
        ),
        out_shape=jax.ShapeDtypeStruct(x_rows.shape, F32),
        compiler_params=_cparams(("arbitrary",)),
        name="moe_experts",
    )(pstart, padded, x_rows, w_gate, b_gate.reshape(E, 1, F), w_up, b_up.reshape(E, 1, F),
      w_down, b_down.reshape(E, 1, D))


def _combine_kernel(dest_ref, dnext_ref, y_ref, x1_ref, p_ref, g_ref, o_ref, ybuf, sem):
    i = pl.program_id(0)
    tm = x1_ref.shape[0]
    slot = i % 2

    def row_copy(d_ref, sl, t, k):
        src = y_ref.at[pl.ds(pl.multiple_of(d_ref[k, t] * SUBLANES, SUBLANES), SUBLANES), :]
        dst = ybuf.at[sl, k, pl.ds(pl.multiple_of(t * SUBLANES, SUBLANES), SUBLANES), :]
        return pltpu.make_async_copy(src, dst, sem.at[sl])

    def request(d_ref, sl):
        def go(tb, c):
            for u in range(ROW_UNROLL):
                for k in range(TOP_K):
                    row_copy(d_ref, sl, tb * ROW_UNROLL + u, k).start(priority=(u * TOP_K + k) % 2)
            return c
        lax.fori_loop(0, tm // ROW_UNROLL, go, 0)

    @pl.when(i == 0)
    def _():
        request(dest_ref, 0)

    @pl.when(i + 1 < pl.num_programs(0))
    def _():
        request(dnext_ref, 1 - slot)

    def wait(tb, c):
        for u in range(ROW_UNROLL):
            for k in range(TOP_K):
                row_copy(dest_ref, slot, tb * ROW_UNROLL + u, k).wait()
        return c

    lax.fori_loop(0, tm // ROW_UNROLL, wait, 0)
    p = p_ref[...]
    y = x1_ref[...]
    for k in range(TOP_K):
        y = y + p[:, k:k + 1] * _tiles_to_rows(ybuf.at[slot, k], tm)
    o_ref[...] = _rms(y, g_ref[...])


def _combine(dest3, y_rows, x1, pnat, g_final):
    T, D = x1.shape
    tm = COMBINE_TM
    nt = T // tm
    return pl.pallas_call(
        _combine_kernel,
        grid=(nt,),
        in_specs=[
            pl.BlockSpec((None, TOP_K, tm), lambda i: (i, 0, 0), memory_space=pltpu.SMEM),
            pl.BlockSpec((None, TOP_K, tm), lambda i: (jnp.minimum(i + 1, nt - 1), 0, 0), memory_space=pltpu.SMEM),
            pl.BlockSpec(memory_space=pl.ANY),
            pl.BlockSpec((tm, D), lambda i: (i, 0)),
            pl.BlockSpec((tm, LANES), lambda i: (i, 0)),
            pl.BlockSpec((1, D), lambda i: (0, 0)),
        ],
        out_specs=pl.BlockSpec((tm, D), lambda i: (i, 0)),
        out_shape=jax.ShapeDtypeStruct((T, D), F32),
        scratch_shapes=[pltpu.VMEM((2, TOP_K, tm * SUBLANES, LANES), F32), pltpu.SemaphoreType.DMA((2,))],
        compiler_params=_cparams(("arbitrary",)),
        name="moe_combine",
    )(dest3, dest3, y_rows, x1, pnat, g_final)


def _rope_tables(S, groups):
    ng = len(groups)

    def cos_sin(shape, ax, f):
        rho = lax.broadcasted_iota(jnp.int32, shape, ax)
        gid = lax.broadcasted_iota(jnp.int32, shape, 0)
        pos = jnp.zeros(shape, jnp.int32)
        for n, (d, tm) in enumerate(groups):
            per = tm // d
            w = rho % tm
            pos = jnp.where(gid == n, ((rho // tm) * per + w % per) * d + w // per, pos)
        inv = jnp.float32(ROPE_THETA) ** (-f.astype(F32) / ROT_HALF)
        ang = pos.astype(F32) * inv
        return jnp.cos(ang), jnp.sin(ang)

    sh = (ng, ROT_HALF, S)
    cq, sq = cos_sin(sh, 2, lax.broadcasted_iota(jnp.int32, sh, 1))
    sh = (ng, S, ROT_HALF)
    cos, sin = cos_sin(sh, 1, lax.broadcasted_iota(jnp.int32, sh, 2))
    lane = lax.broadcasted_iota(jnp.int32, (ROT_HALF, LANES), 1) % HEAD_DIM
    f = lax.broadcasted_iota(jnp.int32, (ROT_HALF, LANES), 0)
    lo = ((lane == f)).astype(F32)
    hi = ((lane == f + ROT_HALF)).astype(F32)
    spread = lambda a, m: jnp.dot(a, m, precision=lax.Precision.HIGHEST)
    rest = (lax.broadcasted_iota(jnp.int32, (1, 1, LANES), 2) % HEAD_DIM >= 2 * ROT_HALF).astype(F32)
    ck = spread(cos, lo + hi) + rest
    s1 = spread(-sin, lo)
    s2 = spread(sin, hi)
    return cq, sq, ck, s1, s2


def _layer(x, ln_mix_g, w_in, b_gate_a, b_gate_b, w_o_a, w_o_b, w_out, ln_ffn_g, w_router, b_router,
           w_gate, b_gate, w_up, b_up, w_down, b_down, ln_out_g):
    B, S, D = x.shape
    T = B * S
    qa0, ka0, va0 = 0, WIDTH_A, 2 * WIDTH_A
    qb0, kb0, vb0 = 3 * WIDTH_A, 3 * WIDTH_A + WIDTH_B, 3 * WIDTH_A + 2 * WIDTH_B
    g0 = 3 * WIDTH_A + 3 * WIDTH_B
    cols = lambda s, w: w_in[:, s:s + w]
    gmix = ln_mix_g.reshape(1, D)

    tabs = _rope_tables(S, [(d, _proj_tile(d)) for _, d in DIL_PAIRS])
    obs, lses = [], []
    o_a = None
    for gi, (_, d) in enumerate(DIL_PAIRS):
        off = gi * WIDTH_G
        wq, wk, wv = cols(qb0 + off, WIDTH_G), cols(kb0 + off, WIDTH_G), cols(vb0 + off, WIDTH_G)
        na = 0
        if gi == 0:
            na = WIDTH_A
            wq = jnp.concatenate([cols(qa0, WIDTH_A), wq], axis=1)
            wk = jnp.concatenate([cols(ka0, WIDTH_A), wk], axis=1)
            wv = jnp.concatenate([cols(va0, WIDTH_A), wv], axis=1)
        outs = _project(x, gmix, wq.T.astype(BF16), wk.astype(BF16), wv.T.astype(BF16), tabs, gi, d, na)
        if gi == 0:
            qTa, vTa, ka, kmean = outs[:4]
            outs = outs[4:]
            o_a = _moba(qTa, ka, vTa, kmean.reshape(B, S // MOBA_BLOCK, WIDTH_A))
        qTb, vTb, kb = outs
        o_g, lse_g = _dilated(qTb, kb, vTb)
        obs.append(o_g)
        lses.append(lse_g)

    x1, h2, idxT, rankT, pnat, cnt = _post(
        x.reshape(T, D), o_a.reshape(T, WIDTH_A), obs, lses, gmix,
        cols(g0, 2 * D).astype(BF16), b_gate_a.reshape(1, D), b_gate_b.reshape(1, D),
        w_o_a.astype(BF16), w_o_b.astype(BF16), w_out.astype(BF16), ln_ffn_g.reshape(1, D),
        w_router.T, b_router.reshape(N_EXPERTS, 1))

    br = EXPERT_ROWS
    counts = cnt[:, 0].astype(jnp.int32)
    padded = (counts + br - 1) // br * br
    pend = jnp.cumsum(padded)
    pstart = pend - padded
    eids = jnp.arange(N_EXPERTS, dtype=jnp.int32)[:, None, None]
    dest = jnp.sum(jnp.where(idxT[None, :TOP_K] == eids, pstart[:, None, None], 0), axis=0) + rankT[:TOP_K]
    n_rows = T * TOP_K + N_EXPERTS * br

    dest_d = dest.reshape(TOP_K, T // DISPATCH_TM, DISPATCH_TM).transpose(1, 0, 2)
    x_rows = _dispatch(pend.astype(jnp.int32), padded.astype(jnp.int32), dest_d, h2, n_rows)
    y_rows = _experts(pstart.astype(jnp.int32), padded.astype(jnp.int32), x_rows, w_gate, b_gate, w_up, b_up,
                      w_down, b_down)
    dest_c = dest.reshape(TOP_K, T // COMBINE_TM, COMBINE_TM).transpose(1, 0, 2)
    out = _combine(dest_c, y_rows, x1, pnat, ln_out_g.reshape(1, D))
    return out.reshape(B, S, D)


def kernel(x, ln_mix_g, w_in, b_gate_a, b_gate_b, w_o_a, w_o_b, w_out, ln_ffn_g, w_router, b_router,
           w_gate, b_gate, w_up, b_up, w_down, b_down, ln_final_g):
    depth = ln_mix_g.shape[0]
    assert depth == 1, "the final RMSNorm is fused into the last layer's combine"
    return _layer(x, ln_mix_g[0], w_in[0], b_gate_a[0], b_gate_b[0], w_o_a[0], w_o_b[0], w_out[0],
                  ln_ffn_g[0], w_router[0], b_router[0], w_gate[0], b_gate[0], w_up[0], b_up[0],
                  w_down[0], b_down[0], ln_final_g)
```

```python
import functools

import jax
import jax.numpy as jnp
from jax import lax
from jax.experimental import pallas as pl
from jax.experimental.pallas import tpu as pltpu

D_MODEL = 1024
HEAD_DIM = 64
ROT_HALF = HEAD_DIM // 8
ROPE_THETA = 500000.0
N_HEADS_A = 8
MOBA_BLOCK = 256
MOBA_TOPK = 3
MOBA_QBLOCKS = 2
MOBA_UNROLL = 4
MOBA_GATE_BLOCKS = 8
DIL_PAIRS = ((128, 1), (512, 4), (2048, 16))
DIL_BAND = 128
HEADS_PER_GROUP_B = 4
WIDTH_A = N_HEADS_A * HEAD_DIM
WIDTH_G = HEADS_PER_GROUP_B * HEAD_DIM
WIDTH_B = WIDTH_G * len(DIL_PAIRS)
N_EXPERTS = 32
TOP_K = 4
SWIGLU_LIMIT = 7.0
SWIGLU_ALPHA = 1.702
NORM_EPS = 1e-5
SCALE = HEAD_DIM ** -0.5
LOG2E = 1.4426950408889634

LANES = 128
PROJ_TM = 1024
PROJ_TM_DILATED = 2048
PERM_TILE = 512
DIL_TL = 2048
POST_TM = 512
EXPERT_ROWS = 128
EXPERT_GROUP = 4
DISPATCH_TM = 512
COMBINE_TM = 256
SUBLANES = 8
ROW_UNROLL = 8
VMEM_LIMIT = 56 * 1024 * 1024
NEG = -1e30

BF16 = jnp.bfloat16
F32 = jnp.float32


def _dot(a, b):
    return jnp.dot(a, b, preferred_element_type=F32)


def _dot_nt(a, b):
    return lax.dot_general(a, b, (((1,), (1,)), ((), ())), preferred_element_type=F32)


def _rms(x, g):
    ms = jnp.mean(x * x, axis=-1, keepdims=True)
    return x * lax.rsqrt(ms + NORM_EPS) * g


def _rows_to_tiles(ref, val):
    n = val.shape[0]
    for c in range(val.shape[1] // LANES):
        ref[pl.ds(c, n, stride=SUBLANES), :] = val[:, c * LANES:(c + 1) * LANES]


def _tiles_to_rows(ref, n):
    return jnp.concatenate([ref[pl.ds(c, n, stride=SUBLANES), :] for c in range(SUBLANES)], axis=1)


def _cparams(sem):
    return pltpu.CompilerParams(dimension_semantics=sem, vmem_limit_bytes=VMEM_LIMIT)


def _deinterleave(h, d):
    tm = h.shape[0]
    sub = min(tm, PERM_TILE)
    cs = sub // d
    ri = lax.broadcasted_iota(jnp.int32, (sub, sub), 0)
    ui = lax.broadcasted_iota(jnp.int32, (sub, sub), 1)
    perm = (ui == (ri % cs) * d + ri // cs).astype(BF16)
    slabs = [_dot(perm, h[s0:s0 + sub]).astype(BF16) for s0 in range(0, tm, sub)]
    return jnp.concatenate([sl[r * cs:(r + 1) * cs] for r in range(d) for sl in slabs], axis=0)


def _proj_kernel(x_ref, g_ref, wq_ref, wk_ref, wv_ref, cq_ref, sq_ref, ck_ref, s1_ref, s2_ref, *outs, na, d):
    tm = x_ref.shape[0]
    h = _rms(x_ref[...], g_ref[...]).astype(BF16)
    if d > 1:
        h = _deinterleave(h, d)
    nq = wq_ref.shape[0]
    heads = nq // HEAD_DIM

    qT = _dot_nt(wq_ref[...], h)
    q3 = qT.reshape(heads, HEAD_DIM, tm)
    c = cq_ref[...][None]
    s = sq_ref[...][None]
    x1 = q3[:, 0:ROT_HALF]
    x2 = q3[:, ROT_HALF:2 * ROT_HALF]
    q3 = jnp.concatenate([x1 * c - x2 * s, x2 * c + x1 * s, q3[:, 2 * ROT_HALF:]], axis=1)
    hid = lax.broadcasted_iota(jnp.int32, (heads, 1, 1), 0)
    q3 = q3 * jnp.where(hid < na // HEAD_DIM, SCALE * LOG2E, SCALE)
    qT = q3.reshape(nq, tm).astype(BF16)

    vT = _dot_nt(wv_ref[...], h).astype(BF16)

    kk = _dot(h, wk_ref[...])
    ck, s1, s2 = ck_ref[...], s1_ref[...], s2_ref[...]
    kparts = []
    for gi in range(kk.shape[1] // LANES):
        kg = kk[:, gi * LANES:(gi + 1) * LANES]
        kparts.append(kg * ck + pltpu.roll(kg, LANES - ROT_HALF, 1) * s1 + pltpu.roll(kg, ROT_HALF, 1) * s2)

    if na:
        qa_ref, va_ref, ka_ref, km_ref, qb_ref, vb_ref, kb_ref = outs
        for blk in range(tm // MOBA_BLOCK):
            sl = slice(blk * MOBA_BLOCK, (blk + 1) * MOBA_BLOCK)
            qa_ref[blk] = qT[0:na, sl]
            va_ref[blk] = vT[0:na, sl]
        lane = lax.broadcasted_iota(jnp.int32, (tm, LANES), 1)
        row = lax.broadcasted_iota(jnp.int32, (tm, LANES), 0)
        blkid = pl.program_id(1) * (tm // MOBA_BLOCK) + row // MOBA_BLOCK
        for gi in range(na // LANES):
            kg = kparts[gi]
            for e in range(2):
                in_head = (lane >= HEAD_DIM * e) & (lane < HEAD_DIM * (e + 1))
                onehot = (lane - HEAD_DIM * (1 - e)) == blkid
                col = (2 * gi + e) * LANES
                ka_ref[:, col:col + LANES] = jnp.where(in_head, kg, onehot.astype(F32)).astype(BF16)
            km = kg.reshape(tm // MOBA_BLOCK, MOBA_BLOCK, LANES).sum(axis=1) * (1.0 / MOBA_BLOCK)
            km_ref[:, gi * LANES:(gi + 1) * LANES] = km
    else:
        qb_ref, vb_ref, kb_ref = outs
    per = tm // d
    for r in range(d):
        for cb in range(per // DIL_BAND):
            sl = slice(r * per + cb * DIL_BAND, r * per + (cb + 1) * DIL_BAND)
            qb_ref[r, cb] = qT[na:na + WIDTH_G, sl]
            vb_ref[r, cb] = vT[na:na + WIDTH_G, sl]
        for gi in range(WIDTH_G // LANES):
            kb_ref[r, :, gi * LANES:(gi + 1) * LANES] = kparts[na // LANES + gi][r * per:(r + 1) * per].astype(BF16)


def _proj_tile(d):
    return PROJ_TM if d == 1 else max(PROJ_TM_DILATED, d * DIL_BAND)


def _project(x, g, wqT, wk, wvT, tabs, gi, d, na):
    B, S, D = x.shape
    L = S // d
    tm = _proj_tile(d)
    per = tm // d
    cq, sq, ck, s1, s2 = tabs
    grid = (B, S // tm)
    full = lambda a: pl.BlockSpec(a.shape, lambda b, i: (0,) * a.ndim)
    in_specs = [
        pl.BlockSpec((None, tm, D), lambda b, i: (b, i, 0)),
        full(g), full(wqT), full(wk), full(wvT),
        pl.BlockSpec((None, ROT_HALF, tm), lambda b, i: (gi, 0, i)),
        pl.BlockSpec((None, ROT_HALF, tm), lambda b, i: (gi, 0, i)),
        pl.BlockSpec((None, tm, LANES), lambda b, i: (gi, i, 0)),
        pl.BlockSpec((None, tm, LANES), lambda b, i: (gi, i, 0)),
        pl.BlockSpec((None, tm, LANES), lambda b, i: (gi, i, 0)),
    ]
    nsub = per // DIL_BAND
    out_shape = [
        jax.ShapeDtypeStruct((B, d, L // DIL_BAND, WIDTH_G, DIL_BAND), BF16),
        jax.ShapeDtypeStruct((B, d, L // DIL_BAND, WIDTH_G, DIL_BAND), BF16),
        jax.ShapeDtypeStruct((B, d, L, WIDTH_G), BF16),
    ]
    out_specs = [
        pl.BlockSpec((None, d, nsub, WIDTH_G, DIL_BAND), lambda b, i: (b, 0, i, 0, 0)),
        pl.BlockSpec((None, d, nsub, WIDTH_G, DIL_BAND), lambda b, i: (b, 0, i, 0, 0)),
        pl.BlockSpec((None, d, per, WIDTH_G), lambda b, i: (b, 0, i, 0)),
    ]
    if na:
        nblk = tm // MOBA_BLOCK
        out_shape = [
            jax.ShapeDtypeStruct((B, S // MOBA_BLOCK, na, MOBA_BLOCK), BF16),
            jax.ShapeDtypeStruct((B, S // MOBA_BLOCK, na, MOBA_BLOCK), BF16),
            jax.ShapeDtypeStruct((B, S, 2 * na), BF16),
            jax.ShapeDtypeStruct((B, S // tm, nblk, na), F32),
        ] + out_shape
        out_specs = [
            pl.BlockSpec((None, nblk, na, MOBA_BLOCK), lambda b, i: (b, i, 0, 0)),
            pl.BlockSpec((None, nblk, na, MOBA_BLOCK), lambda b, i: (b, i, 0, 0)),
            pl.BlockSpec((None, tm, 2 * na), lambda b, i: (b, i, 0)),
            pl.BlockSpec((None, None, nblk, na), lambda b, i: (b, i, 0, 0)),
        ] + out_specs
    return pl.pallas_call(
        functools.partial(_proj_kernel, na=na, d=d),
        grid=grid, in_specs=in_specs, out_specs=out_specs, out_shape=out_shape,
        compiler_params=_cparams(("arbitrary",) * 2),
        name=f"proj_d{d}",
    )(x, g, wqT, wk, wvT, cq, sq, ck, s1, s2)


def _moba_kernel(q_ref, k_ref, v_ref, km_ref, o_ref, qa_scr, s_scr, p_scr, sel_scr):
    n_super = q_ref.shape[0] // MOBA_QBLOCKS
    _moba_select(q_ref, km_ref, sel_scr)
    _moba_prepare(q_ref, k_ref, sel_scr, qa_scr, s_scr, 0)

    def trip(g, c):
        _moba_attend(k_ref, v_ref, o_ref, qa_scr, s_scr, p_scr, g)
        _moba_prepare(q_ref, k_ref, sel_scr, qa_scr, s_scr, jnp.minimum(g + 1, n_super - 1))
        return c

    lax.fori_loop(0, n_super, trip, 0)


def _moba_select(q_ref, km_ref, sel_scr):
    nb, _, blk = q_ref.shape
    cb = MOBA_GATE_BLOCKS
    wq = cb * blk
    bid = lax.broadcasted_iota(jnp.int32, (nb, wq), 0)
    km = km_ref[...].astype(BF16)
    zq = jnp.zeros((HEAD_DIM, wq), BF16)

    def chunk(c, carry):
        qblk = cb * c + lax.broadcasted_iota(jnp.int32, (nb, wq), 1) // blk
        for h in range(2):
            qh = jnp.concatenate([q_ref[cb * c + i, HEAD_DIM * h:HEAD_DIM * (h + 1), :] for i in range(cb)], axis=1)
            q_plain = jnp.concatenate([qh, zq] if h == 0 else [zq, qh], axis=0)
            gt = jnp.where(bid < qblk, _dot(km, q_plain), -jnp.inf)
            sel = jnp.zeros(gt.shape, jnp.bool_)
            for _ in range(MOBA_TOPK):
                mx = jnp.max(gt, axis=0, keepdims=True)
                first = jnp.min(jnp.where((gt == mx) & (mx > -jnp.inf), bid, nb), axis=0, keepdims=True)
                pick = bid == first
                sel = sel | pick
                gt = jnp.where(pick, -jnp.inf, gt)
            sel_scr[h, :, pl.ds(pl.multiple_of(c * wq, wq), wq)] = sel.astype(F32)
        return carry

    lax.fori_loop(0, nb // cb, chunk, 0)


def _moba_prepare(q_ref, k_ref, sel_scr, qa_scr, s_scr, g):
    nb = q_ref.shape[0]
    blk = MOBA_BLOCK
    tq = MOBA_QBLOCKS * blk
    q2 = jnp.concatenate([q_ref[MOBA_QBLOCKS * g + i] for i in range(MOBA_QBLOCKS)], axis=1)
    bid = lax.broadcasted_iota(jnp.int32, (nb, tq), 0)
    qblk = MOBA_QBLOCKS * g + lax.broadcasted_iota(jnp.int32, (nb, tq), 1) // blk
    key_i = lax.broadcasted_iota(jnp.int32, (blk, tq), 0)
    qry_i = lax.broadcasted_iota(jnp.int32, (blk, tq), 1)
    zb = jnp.zeros((HEAD_DIM - nb, tq), BF16)

    for h in range(2):
        hl = slice(LANES * h, LANES * (h + 1))
        qh = q2[HEAD_DIM * h:HEAD_DIM * (h + 1)]
        sel = sel_scr[h, :, pl.ds(pl.multiple_of(g * tq, tq), tq)] > 0.5
        bias = jnp.where(sel, 0.0, NEG).astype(BF16)
        qa_scr[h] = jnp.concatenate([qh, bias, zb] if h == 0 else [bias, zb, qh], axis=0)
        bias_d = jnp.where(sel | (bid == qblk), 0.0, NEG).astype(BF16)
        q_diag = jnp.concatenate([qh, bias_d, zb] if h == 0 else [bias_d, zb, qh], axis=0)
        for i in range(MOBA_QBLOCKS):
            j = MOBA_QBLOCKS * g + i
            sd = _dot(k_ref[pl.ds(pl.multiple_of(j * blk, blk), blk), hl], q_diag)
            own = (qry_i >= i * blk) & (qry_i < (i + 1) * blk)
            s_scr[h, i] = jnp.where(own & (key_i > qry_i - i * blk), NEG, sd)


def _moba_attend(k_ref, v_ref, o_ref, qa_scr, s_scr, p_scr, g):
    nb = v_ref.shape[0]
    blk = MOBA_BLOCK
    tq = MOBA_QBLOCKS * blk

    def score_blocks(c):
        return [jnp.where(c < g, MOBA_QBLOCKS * c + i, nb - 1) for i in range(MOBA_QBLOCKS)]

    def value_blocks(it):
        return [jnp.where(it <= 1, MOBA_QBLOCKS * g + i, MOBA_QBLOCKS * (it - 2) + i) for i in range(MOBA_QBLOCKS)]

    def values(it, h, alpha, acc):
        vl = slice(HEAD_DIM * h, HEAD_DIM * (h + 1))
        acc = alpha * acc
        for i, j in enumerate(value_blocks(it)):
            acc = acc + _dot(v_ref[j, vl, :], p_scr[h, i])
        return acc

    p_scr[...] = jnp.zeros(p_scr.shape, BF16)

    def body(it, carry):
        alphas, ms, ls, accs = carry
        a_new, m_new, l_new, acc_new = [], [], [], []
        for h in range(2):
            hl = slice(LANES * h, LANES * (h + 1))
            qa = qa_scr[h]
            s_next = [_dot(k_ref[pl.ds(pl.multiple_of(j * blk, blk), blk), hl], qa) for j in score_blocks(it)]
            s_cur = [s_scr[h, i] for i in range(MOBA_QBLOCKS)]
            acc_new.append(values(it, h, alphas[h], accs[h]))
            mn = ms[h]
            for sc in s_cur:
                mn = jnp.maximum(mn, jnp.max(sc, axis=0, keepdims=True))
            alpha = jnp.exp2(ms[h] - mn)
            l = alpha * ls[h]
            for i in range(MOBA_QBLOCKS):
                p = jnp.exp2(s_cur[i] - mn)
                l = l + jnp.sum(p, axis=0, keepdims=True)
                s_scr[h, i] = s_next[i]
                p_scr[h, i] = p.astype(BF16)
            m_new.append(mn)
            l_new.append(l)
            a_new.append(alpha)
        return (tuple(a_new), tuple(m_new), tuple(l_new), tuple(acc_new))

    one = jnp.ones((1, tq), F32)
    neg = jnp.full((1, tq), NEG, F32)
    zl = jnp.zeros((1, tq), F32)
    za = jnp.zeros((HEAD_DIM, tq), F32)
    n_it = g + 1
    n_main = n_it // MOBA_UNROLL

    def trip(t, c):
        for u in range(MOBA_UNROLL):
            c = body(MOBA_UNROLL * t + u, c)
        return c

    carry = lax.fori_loop(0, n_main, trip, ((one, one), (neg, neg), (zl, zl), (za, za)))
    alphas, _, ls, accs = lax.fori_loop(MOBA_UNROLL * n_main, n_it, body, carry)
    accs = [values(n_it, h, alphas[h], accs[h]) for h in range(2)]
    oT = jnp.concatenate([accs[0] / ls[0], accs[1] / ls[1]], axis=0)
    o_ref[pl.ds(pl.multiple_of(g * tq, tq), tq), :] = oT.T.astype(o_ref.dtype)


def _moba(qT, k, vT, kmean):
    B, nb, wa, blk = qT.shape
    S = nb * blk
    hp = wa // LANES
    nq = MOBA_QBLOCKS
    assert nb % 16 == 0 and nb <= HEAD_DIM, "block-mask rows must fit the spare half of a head pair"
    return pl.pallas_call(
        _moba_kernel,
        grid=(B, hp),
        in_specs=[
            pl.BlockSpec((None, nb, LANES, blk), lambda b, p: (b, 0, p, 0)),
            pl.BlockSpec((None, S, 2 * LANES), lambda b, p: (b, 0, p)),
            pl.BlockSpec((None, nb, LANES, blk), lambda b, p: (b, 0, p, 0)),
            pl.BlockSpec((None, nb, LANES), lambda b, p: (b, 0, p)),
        ],
        out_specs=pl.BlockSpec((None, S, LANES), lambda b, p: (b, 0, p)),
        out_shape=jax.ShapeDtypeStruct((B, S, wa), BF16),
        scratch_shapes=[pltpu.VMEM((2, LANES, nq * blk), BF16), pltpu.VMEM((2, nq, blk, nq * blk), F32),
                        pltpu.VMEM((2, nq, blk, nq * blk), BF16), pltpu.VMEM((2, nb, S), F32)],
        compiler_params=_cparams(("arbitrary",) * 2),
        name="moba_attn",
    )(qT, k, vT, kmean)


def _dil_kernel(q_ref, k_ref, kp_ref, v_ref, vp_ref, o_ref, lse_ref):
    t = pl.program_id(2)
    band = DIL_BAND
    nres, nsub = q_ref.shape[0], q_ref.shape[1]
    key_i = lax.broadcasted_iota(jnp.int32, (band, band), 0)
    qry_i = lax.broadcasted_iota(jnp.int32, (band, band), 1)
    rowid = lax.broadcasted_iota(jnp.int32, (LANES, band), 0)
    own_ok = key_i <= qry_i
    prev_ok = key_i >= qry_i
    for r in range(nres):
        for c in range(nsub):
            o_parts, l_parts = [], []
            for hp in range(WIDTH_G // LANES):
                cols = slice(hp * LANES, (hp + 1) * LANES)
                k_own = k_ref[r, c * band:(c + 1) * band, cols]
                k_prev = kp_ref[r, :, cols] if c == 0 else k_ref[r, (c - 1) * band:c * band, cols]
                q2 = q_ref[r, c, cols, :]
                for h in range(2):
                    qh = jnp.where((rowid >= HEAD_DIM * h) & (rowid < HEAD_DIM * (h + 1)), q2, jnp.zeros_like(q2))
                    pmask = (prev_ok & (t > 0)) if c == 0 else prev_ok
                    s_own = jnp.where(own_ok, _dot(k_own, qh), -jnp.inf)
                    s_prev = jnp.where(pmask, _dot(k_prev, qh), -jnp.inf)
                    m = jnp.maximum(jnp.max(s_own, axis=0, keepdims=True), jnp.max(s_prev, axis=0, keepdims=True))
                    p_own = jnp.exp(s_own - m)
                    p_prev = jnp.exp(s_prev - m)
                    l = jnp.sum(p_own, axis=0, keepdims=True) + jnp.sum(p_prev, axis=0, keepdims=True)
                    rows = slice(hp * LANES + h * HEAD_DIM, hp * LANES + (h + 1) * HEAD_DIM)
                    v_own = v_ref[r, c, rows, :]
                    v_prev = vp_ref[r, rows, :] if c == 0 else v_ref[r, c - 1, rows, :]
                    oT = _dot(v_own, p_own.astype(BF16)) + _dot(v_prev, p_prev.astype(BF16))
                    o_parts.append(oT / l)
                    l_parts.append(jnp.broadcast_to(m + jnp.log(l), (HEAD_DIM, band)))
            o_ref[r, c * band:(c + 1) * band, :] = jnp.concatenate(o_parts, axis=0).T
            lse_ref[r, c * band:(c + 1) * band, :] = jnp.concatenate(l_parts, axis=0).T


def _dilated(qT, k, vT):
    B, d, nblk, wg, band = qT.shape
    L = nblk * band
    tl = min(DIL_TL, L)
    nsub = tl // band
    nres = min(d, max(1, DIL_TL // L))
    prev = lambda t: jnp.maximum(t * nsub - 1, 0)
    return pl.pallas_call(
        _dil_kernel,
        grid=(B, d // nres, L // tl),
        in_specs=[
            pl.BlockSpec((None, nres, nsub, wg, band), lambda b, r, t: (b, r, t, 0, 0)),
            pl.BlockSpec((None, nres, tl, wg), lambda b, r, t: (b, r, t, 0)),
            pl.BlockSpec((None, nres, band, wg), lambda b, r, t: (b, r, prev(t), 0)),
            pl.BlockSpec((None, nres, nsub, wg, band), lambda b, r, t: (b, r, t, 0, 0)),
            pl.BlockSpec((None, nres, None, wg, band), lambda b, r, t: (b, r, prev(t), 0, 0)),
        ],
        out_specs=[
            pl.BlockSpec((None, nres, tl, wg), lambda b, r, t: (b, r, t, 0)),
            pl.BlockSpec((None, nres, tl, wg), lambda b, r, t: (b, r, t, 0)),
        ],
        out_shape=[jax.ShapeDtypeStruct((B, d, L, wg), F32)] * 2,
        compiler_params=_cparams(("arbitrary",) * 3),
        name=f"dilated_d{d}",
    )(qT, k, k, vT, vT)


def _interleave(ref, scr):
    d, per, w = ref.shape
    if d == 1:
        return ref[0]
    for r in range(d):
        for sl in range(w // LANES):
            scr[sl, pl.ds(r, per, stride=d), :] = ref[r, :, sl * LANES:(sl + 1) * LANES]
    return jnp.concatenate([scr[sl] for sl in range(w // LANES)], axis=1)


def _post_kernel(x_ref, oa_ref, o1_ref, o2_ref, o3_ref, l1_ref, l2_ref, l3_ref, gmix_ref, wg_ref, bga_ref,
                 bgb_ref, woa_ref, wob_ref, wout_ref, gffn_ref, wr_ref, br_ref,
                 x1_ref, h2_ref, idx_ref, rank_ref, pnat_ref, cnt_ref, carry_scr, il_scr):
    i = pl.program_id(0)
    tm, D = x_ref.shape
    ne = wr_ref.shape[0]

    @pl.when(i == 0)
    def _():
        carry_scr[...] = jnp.zeros_like(carry_scr)

    x = x_ref[...]
    h = _rms(x, gmix_ref[...]).astype(BF16)
    gates = _dot(h, wg_ref[...])
    ga = gates[:, :D] + bga_ref[...]
    gb = gates[:, D:] + bgb_ref[...]
    l1, l2, l3 = [_interleave(r, il_scr.at[n]) for n, r in enumerate((l1_ref, l2_ref, l3_ref))]
    o1, o2, o3 = [_interleave(r, il_scr.at[3 + n]) for n, r in enumerate((o1_ref, o2_ref, o3_ref))]
    mx = jnp.maximum(jnp.maximum(l1, l2), l3)
    e1, e2, e3 = jnp.exp(l1 - mx), jnp.exp(l2 - mx), jnp.exp(l3 - mx)
    ob = (e1 * o1 + e2 * o2 + e3 * o3) / (e1 + e2 + e3)
    ya = _dot(oa_ref[...], woa_ref[...])
    yb = _dot(ob.astype(BF16), wob_ref[...])
    mix = jax.nn.sigmoid(ga) * ya + jax.nn.sigmoid(gb) * yb
    x1 = x + _dot(mix.astype(BF16), wout_ref[...])
    x1_ref[...] = x1
    h2 = _rms(x1, gffn_ref[...])
    _rows_to_tiles(h2_ref, h2)

    wr = wr_ref[...]
    wr_hi = wr.astype(BF16)
    wr_lo = (wr - wr_hi.astype(F32)).astype(BF16)
    h2_hi = h2.astype(BF16)
    h2_lo = (h2 - h2_hi.astype(F32)).astype(BF16)
    logits = _dot_nt(wr_hi, h2_hi) + (_dot_nt(wr_hi, h2_lo) + _dot_nt(wr_lo, h2_hi)) + br_ref[...]
    eid = lax.broadcasted_iota(jnp.int32, (ne, tm), 0)
    g = logits
    vals, idxs, picks = [], [], []
    for _ in range(TOP_K):
        m = jnp.max(g, axis=0, keepdims=True)
        first = jnp.min(jnp.where(g == m, eid, ne), axis=0, keepdims=True)
        pick = eid == first
        vals.append(m)
        idxs.append(first)
        picks.append(pick)
        g = jnp.where(pick, -jnp.inf, g)
    es = [jnp.exp(v - vals[0]) for v in vals]
    den = es[0] + es[1] + es[2] + es[3]
    probs = [e / den for e in es]

    onehot = jnp.zeros((ne, tm), F32)
    for pick in picks:
        onehot = onehot + pick.astype(F32)
    earlier = (lax.broadcasted_iota(jnp.int32, (tm, tm), 0) < lax.broadcasted_iota(jnp.int32, (tm, tm), 1))
    prefix = _dot(onehot.astype(BF16), earlier.astype(BF16)) + carry_scr[:, 0:1]
    ranks = [jnp.sum(jnp.where(pick, prefix, 0.0), axis=0, keepdims=True) for pick in picks]
    carry_scr[...] = carry_scr[...] + jnp.sum(onehot, axis=1, keepdims=True)
    cnt_ref[...] = carry_scr[...]

    zi = jnp.zeros((8 - TOP_K, tm), jnp.int32)
    idx_ref[...] = jnp.concatenate(idxs + [zi], axis=0)
    rank_ref[...] = jnp.concatenate([r.astype(jnp.int32) for r in ranks] + [zi], axis=0)
    pnat_ref[...] = jnp.concatenate(probs + [jnp.zeros((LANES - TOP_K, tm), F32)], axis=0).T


def _post(x2, oa, obs, lses, gmix, wg, bga, bgb, woa, wob, wout, gffn, wrT, br):
    T, D = x2.shape
    tm = POST_TM
    ne = wrT.shape[0]
    row = lambda w: pl.BlockSpec((tm, w), lambda i: (i, 0))

    def grouped(a):
        _, d, L, w = a.shape
        nt = L * d // tm
        return pl.BlockSpec((None, d, tm // d, w), lambda i: (i // nt, 0, i % nt, 0))

    full = lambda a: pl.BlockSpec(a.shape, lambda i: (0,) * a.ndim)
    col = pl.BlockSpec((8, tm), lambda i: (0, i))
    return pl.pallas_call(
        _post_kernel,
        grid=(T // tm,),
        in_specs=[row(D), row(WIDTH_A)] + [grouped(a) for a in (*obs, *lses)]
                 + [full(a) for a in (gmix, wg, bga, bgb, woa, wob, wout, gffn, wrT, br)],
        out_specs=[row(D), pl.BlockSpec((tm * SUBLANES, LANES), lambda i: (i, 0)), col, col, row(LANES),
                   pl.BlockSpec((ne, LANES), lambda i: (0, 0))],
        out_shape=[
            jax.ShapeDtypeStruct((T, D), F32), jax.ShapeDtypeStruct((T * SUBLANES, LANES), F32),
            jax.ShapeDtypeStruct((8, T), jnp.int32), jax.ShapeDtypeStruct((8, T), jnp.int32),
            jax.ShapeDtypeStruct((T, LANES), F32),
            jax.ShapeDtypeStruct((ne, LANES), F32),
        ],
        scratch_shapes=[pltpu.VMEM((ne, LANES), F32), pltpu.VMEM((6, WIDTH_G // LANES, tm, LANES), F32)],
        compiler_params=_cparams(("arbitrary",)),
        name="post_mix_router",
    )(x2, oa, *obs, *lses, gmix, wg, bga, bgb, woa, wob, wout, gffn, wrT, br)


def _dispatch_kernel(pend_ref, padded_ref, dest_ref, h_ref, xr_ref, zero_scr, sem):
    i = pl.program_id(0)
    tm = h_ref.shape[0] // SUBLANES
    br = zero_scr.shape[0]

    def zero_copy(blk_start):
        return pltpu.make_async_copy(zero_scr, xr_ref.at[pl.ds(pl.multiple_of(blk_start * SUBLANES, br), br), :], sem)

    @pl.when(i == 0)
    def _():
        zero_scr[...] = jnp.zeros_like(zero_scr)
        n_used = pend_ref[N_EXPERTS - 1] // EXPERT_ROWS
        n_blk = xr_ref.shape[0] // br

        def tail(fn):
            def go(e, c):
                @pl.when(padded_ref[e] > 0)
                def _():
                    fn(zero_copy(pend_ref[e] - EXPERT_ROWS))
                return c
            return go

        def unused(fn):
            def go(b, c):
                fn(zero_copy(b * EXPERT_ROWS))
                return c
            return go

        lax.fori_loop(0, N_EXPERTS, tail(lambda cp: cp.start()), 0)
        lax.fori_loop(n_used, n_blk, unused(lambda cp: cp.start()), 0)
        lax.fori_loop(0, N_EXPERTS, tail(lambda cp: cp.wait()), 0)
        lax.fori_loop(n_used, n_blk, unused(lambda cp: cp.wait()), 0)

    def row_copy(t, k):
        src = h_ref.at[pl.ds(pl.multiple_of(t * SUBLANES, SUBLANES), SUBLANES), :]
        dst = xr_ref.at[pl.ds(pl.multiple_of(dest_ref[k, t] * SUBLANES, SUBLANES), SUBLANES), :]
        return pltpu.make_async_copy(src, dst, sem)

    def start(tb, c):
        for u in range(ROW_UNROLL):
            for k in range(TOP_K):
                row_copy(tb * ROW_UNROLL + u, k).start(priority=(u * TOP_K + k) % 2)
        return c

    def wait(tb, c):
        for u in range(ROW_UNROLL):
            for k in range(TOP_K):
                row_copy(tb * ROW_UNROLL + u, k).wait()
        return c

    lax.fori_loop(0, tm // ROW_UNROLL, start, 0)
    lax.fori_loop(0, tm // ROW_UNROLL, wait, 0)


def _dispatch(pend, padded, dest3, h2t, n_rows):
    tm = DISPATCH_TM
    return pl.pallas_call(
        _dispatch_kernel,
        grid_spec=pltpu.PrefetchScalarGridSpec(
            num_scalar_prefetch=2,
            grid=(h2t.shape[0] // (tm * SUBLANES),),
            in_specs=[
                pl.BlockSpec((None, TOP_K, tm), lambda i, pe, pa: (i, 0, 0), memory_space=pltpu.SMEM),
                pl.BlockSpec((tm * SUBLANES, LANES), lambda i, pe, pa: (i, 0)),
            ],
            out_specs=pl.BlockSpec(memory_space=pl.ANY),
            scratch_shapes=[pltpu.VMEM((EXPERT_ROWS * SUBLANES, LANES), F32), pltpu.SemaphoreType.DMA(())],
        ),
        out_shape=jax.ShapeDtypeStruct((n_rows * SUBLANES, LANES), F32),
        compiler_params=_cparams(("arbitrary",)),
        name="moe_dispatch",
    )(pend, padded, dest3, h2t)


def _expert_kernel(pstart_ref, padded_ref, x_hbm, wg_ref, bg_ref, wu_ref, bu_ref, wd_ref, bd_ref, y_hbm,
                   w_scr, xbuf, ybuf, xsem, ysem):
    e = pl.program_id(0)
    tb = EXPERT_ROWS * SUBLANES
    nblk = padded_ref[e] // EXPERT_ROWS
    ngrp = nblk // EXPERT_GROUP
    rem = nblk - ngrp * EXPERT_GROUP
    row0 = pstart_ref[e]

    def rows(first_blk, nb_):
        return pl.ds(pl.multiple_of((row0 + first_blk * EXPERT_ROWS) * SUBLANES, tb), nb_ * tb)

    def x_copy(first_blk, nb_, slot):
        return pltpu.make_async_copy(x_hbm.at[rows(first_blk, nb_), :], xbuf.at[slot, pl.ds(0, nb_ * tb), :],
                                     xsem.at[slot])

    def y_copy(first_blk, nb_, slot):
        return pltpu.make_async_copy(ybuf.at[slot, pl.ds(0, nb_ * tb), :], y_hbm.at[rows(first_blk, nb_), :],
                                     ysem.at[slot])

    def ffn(slot, nb_):
        n = nb_ * EXPERT_ROWS
        x = _tiles_to_rows(xbuf.at[slot, pl.ds(0, nb_ * tb), :], n).astype(BF16)
        g = _dot(x, w_scr[0]) + bg_ref[...]
        u = _dot(x, w_scr[1]) + bu_ref[...]
        g = jnp.minimum(g, SWIGLU_LIMIT)
        u = jnp.clip(u, -SWIGLU_LIMIT, SWIGLU_LIMIT)
        a = g * jax.nn.sigmoid(SWIGLU_ALPHA * g) * (u + 1.0)
        _rows_to_tiles(ybuf.at[slot, pl.ds(0, nb_ * tb), :], _dot(a.astype(BF16), w_scr[2]) + bd_ref[...])

    def for_step(s, full, short):
        @pl.when(s < ngrp)
        def _():
            full()
        for r in range(1, EXPERT_GROUP):
            @pl.when((s == ngrp) & (rem == r))
            def _():
                short(r)

    @pl.when(nblk > 0)
    def _():
        nstep = ngrp + (rem > 0).astype(jnp.int32)

        def start_x(s, slot):
            for_step(s, lambda: x_copy(s * EXPERT_GROUP, EXPERT_GROUP, slot).start(),
                     lambda r: x_copy(s * EXPERT_GROUP, r, slot).start())

        def wait_y(s, slot):
            for_step(s, lambda: y_copy(s * EXPERT_GROUP, EXPERT_GROUP, slot).wait(),
                     lambda r: y_copy(s * EXPERT_GROUP, r, slot).wait())

        start_x(0, 0)
        w_scr[0] = wg_ref[...].astype(BF16)
        w_scr[1] = wu_ref[...].astype(BF16)
        w_scr[2] = wd_ref[...].astype(BF16)

        def body(s, c):
            slot = s % 2
            start_x(s + 1, 1 - slot)
            x_copy(s * EXPERT_GROUP, EXPERT_GROUP, slot).wait()

            @pl.when(s >= 2)
            def _():
                y_copy((s - 2) * EXPERT_GROUP, EXPERT_GROUP, slot).wait()

            ffn(slot, EXPERT_GROUP)
            y_copy(s * EXPERT_GROUP, EXPERT_GROUP, slot).start()
            return c

        lax.fori_loop(0, ngrp, body, 0)

        for r in range(1, EXPERT_GROUP):
            @pl.when(rem == r)
            def _():
                slot = ngrp % 2
                x_copy(ngrp * EXPERT_GROUP, r, slot).wait()

                @pl.when(ngrp >= 2)
                def _():
                    y_copy((ngrp - 2) * EXPERT_GROUP, EXPERT_GROUP, slot).wait()

                ffn(slot, r)
                y_copy(ngrp * EXPERT_GROUP, r, slot).start()

        @pl.when(nstep >= 2)
        def _():
            wait_y(nstep - 2, nstep % 2)

        wait_y(nstep - 1, (nstep - 1) % 2)

    @pl.when(e == pl.num_programs(0) - 1)
    def _():
        n_used = (pstart_ref[e] + padded_ref[e]) // EXPERT_ROWS
        n_all = y_hbm.shape[0] // tb
        ybuf[0, pl.ds(0, tb), :] = jnp.zeros((tb, LANES), F32)

        def zero_copy(b):
            return pltpu.make_async_copy(ybuf.at[0, pl.ds(0, tb), :],
                                         y_hbm.at[pl.ds(pl.multiple_of(b * tb, tb), tb), :], ysem.at[0])

        lax.fori_loop(n_used, n_all, lambda b, c: (zero_copy(b).start(), c)[1], 0)
        lax.fori_loop(n_used, n_all, lambda b, c: (zero_copy(b).wait(), c)[1], 0)


def _experts(pstart, padded, x_rows, w_gate, b_gate, w_up, b_up, w_down, b_down):
    E, D, F = w_gate.shape
    tb = EXPERT_GROUP * EXPERT_ROWS * SUBLANES
    wspec = lambda shape: pl.BlockSpec((None,) + shape, lambda e, ps, pa: (e, 0, 0))
    return pl.pallas_call(
        _expert_kernel,
        grid_spec=pltpu.PrefetchScalarGridSpec(
            num_scalar_prefetch=2,
            grid=(E,),
            in_specs=[
                pl.BlockSpec(memory_space=pl.ANY),
                wspec((D, F)), wspec((1, F)), wspec((D, F)), wspec((1, F)), wspec((F, D)), wspec((1, D)),
            ],
            out_specs=pl.BlockSpec(memory_space=pl.ANY),
            scratch_shapes=[pltpu.VMEM((3, D, F), BF16), pltpu.VMEM((2, tb, LANES), F32),
                            pltpu.VMEM((2, tb, LANES), F32), pltpu.SemaphoreType.DMA((2,)),
                            pltpu.SemaphoreType.DMA((2,))],
        ),
        out_shape=jax.ShapeDtypeStruct(x_rows.shape, F32),
        compiler_params=_cparams(("arbitrary",)),
        name="moe_experts",
    )(pstart, padded, x_rows, w_gate, b_gate.reshape(E, 1, F), w_up, b_up.reshape(E, 1, F),
      w_down, b_down.reshape(E, 1, D))


def _combine_kernel(dest_ref, dnext_ref, y_ref, x1_ref, p_ref, g_ref, o_ref, ybuf, sem):
    i = pl.program_id(0)
    tm = x1_ref.shape[0]
    slot = i % 2

    def row_copy(d_ref, sl, t, k):
        src = y_ref.at[pl.ds(pl.multiple_of(d_ref[k, t] * SUBLANES, SUBLANES), SUBLANES), :]
        dst = ybuf.at[sl, k, pl.ds(pl.multiple_of(t * SUBLANES, SUBLANES), SUBLANES), :]
        return pltpu.make_async_copy(src, dst, sem.at[sl])

    def request(d_ref, sl):
        def go(tb, c):
            for u in range(ROW_UNROLL):
                for k in range(TOP_K):
                    row_copy(d_ref, sl, tb * ROW_UNROLL + u, k).start(priority=(u * TOP_K + k) % 2)
            return c
        lax.fori_loop(0, tm // ROW_UNROLL, go, 0)

    @pl.when(i == 0)
    def _():
        request(dest_ref, 0)

    @pl.when(i + 1 < pl.num_programs(0))
    def _():
        request(dnext_ref, 1 - slot)

    def wait(tb, c):
        for u in range(ROW_UNROLL):
            for k in range(TOP_K):
                row_copy(dest_ref, slot, tb * ROW_UNROLL + u, k).wait()
        return c

    lax.fori_loop(0, tm // ROW_UNROLL, wait, 0)
    p = p_ref[...]
    y = x1_ref[...]
    for k in range(TOP_K):
        y = y + p[:, k:k + 1] * _tiles_to_rows(ybuf.at[slot, k], tm)
    o_ref[...] = _rms(y, g_ref[...])


def _combine(dest3, y_rows, x1, pnat, g_final):
    T, D = x1.shape
    tm = COMBINE_TM
    nt = T // tm
    return pl.pallas_call(
        _combine_kernel,
        grid=(nt,),
        in_specs=[
            pl.BlockSpec((None, TOP_K, tm), lambda i: (i, 0, 0), memory_space=pltpu.SMEM),
            pl.BlockSpec((None, TOP_K, tm), lambda i: (jnp.minimum(i + 1, nt - 1), 0, 0), memory_space=pltpu.SMEM),
            pl.BlockSpec(memory_space=pl.ANY),
            pl.BlockSpec((tm, D), lambda i: (i, 0)),
            pl.BlockSpec((tm, LANES), lambda i: (i, 0)),
            pl.BlockSpec((1, D), lambda i: (0, 0)),
        ],
        out_specs=pl.BlockSpec((tm, D), lambda i: (i, 0)),
        out_shape=jax.ShapeDtypeStruct((T, D), F32),
        scratch_shapes=[pltpu.VMEM((2, TOP_K, tm * SUBLANES, LANES), F32), pltpu.SemaphoreType.DMA((2,))],
        compiler_params=_cparams(("arbitrary",)),
        name="moe_combine",
    )(dest3, dest3, y_rows, x1, pnat, g_final)


def _rope_tables(S, groups):
    ng = len(groups)

    def cos_sin(shape, ax, f):
        rho = lax.broadcasted_iota(jnp.int32, shape, ax)
        gid = lax.broadcasted_iota(jnp.int32, shape, 0)
        pos = jnp.zeros(shape, jnp.int32)
        for n, (d, tm) in enumerate(groups):
            per = tm // d
            w = rho % tm
            pos = jnp.where(gid == n, ((rho // tm) * per + w % per) * d + w // per, pos)
        inv = jnp.float32(ROPE_THETA) ** (-f.astype(F32) / ROT_HALF)
        ang = pos.astype(F32) * inv
        return jnp.cos(ang), jnp.sin(ang)

    sh = (ng, ROT_HALF, S)
    cq, sq = cos_sin(sh, 2, lax.broadcasted_iota(jnp.int32, sh, 1))
    sh = (ng, S, ROT_HALF)
    cos, sin = cos_sin(sh, 1, lax.broadcasted_iota(jnp.int32, sh, 2))
    lane = lax.broadcasted_iota(jnp.int32, (ROT_HALF, LANES), 1) % HEAD_DIM
    f = lax.broadcasted_iota(jnp.int32, (ROT_HALF, LANES), 0)
    lo = ((lane == f)).astype(F32)
    hi = ((lane == f + ROT_HALF)).astype(F32)
    spread = lambda a, m: jnp.dot(a, m, precision=lax.Precision.HIGHEST)
    rest = (lax.broadcasted_iota(jnp.int32, (1, 1, LANES), 2) % HEAD_DIM >= 2 * ROT_HALF).astype(F32)
    ck = spread(cos, lo + hi) + rest
    s1 = spread(-sin, lo)
    s2 = spread(sin, hi)
    return cq, sq, ck, s1, s2


def _layer(x, ln_mix_g, w_in, b_gate_a, b_gate_b, w_o_a, w_o_b, w_out, ln_ffn_g, w_router, b_router,
           w_gate, b_gate, w_up, b_up, w_down, b_down, ln_out_g):
    B, S, D = x.shape
    T = B * S
    qa0, ka0, va0 = 0, WIDTH_A, 2 * WIDTH_A
    qb0, kb0, vb0 = 3 * WIDTH_A, 3 * WIDTH_A + WIDTH_B, 3 * WIDTH_A + 2 * WIDTH_B
    g0 = 3 * WIDTH_A + 3 * WIDTH_B
    cols = lambda s, w: w_in[:, s:s + w]
    gmix = ln_mix_g.reshape(1, D)

    tabs = _rope_tables(S, [(d, _proj_tile(d)) for _, d in DIL_PAIRS])
    obs, lses = [], []
    o_a = None
    for gi, (_, d) in enumerate(DIL_PAIRS):
        off = gi * WIDTH_G
        wq, wk, wv = cols(qb0 + off, WIDTH_G), cols(kb0 + off, WIDTH_G), cols(vb0 + off, WIDTH_G)
        na = 0
        if gi == 0:
            na = WIDTH_A
            wq = jnp.concatenate([cols(qa0, WIDTH_A), wq], axis=1)
            wk = jnp.concatenate([cols(ka0, WIDTH_A), wk], axis=1)
            wv = jnp.concatenate([cols(va0, WIDTH_A), wv], axis=1)
        outs = _project(x, gmix, wq.T.astype(BF16), wk.astype(BF16), wv.T.astype(BF16), tabs, gi, d, na)
        if gi == 0:
            qTa, vTa, ka, kmean = outs[:4]
            outs = outs[4:]
            o_a = _moba(qTa, ka, vTa, kmean.reshape(B, S // MOBA_BLOCK, WIDTH_A))
        qTb, vTb, kb = outs
        o_g, lse_g = _dilated(qTb, kb, vTb)
        obs.append(o_g)
        lses.append(lse_g)

    x1, h2, idxT, rankT, pnat, cnt = _post(
        x.reshape(T, D), o_a.reshape(T, WIDTH_A), obs, lses, gmix,
        cols(g0, 2 * D).astype(BF16), b_gate_a.reshape(1, D), b_gate_b.reshape(1, D),
        w_o_a.astype(BF16), w_o_b.astype(BF16), w_out.astype(BF16), ln_ffn_g.reshape(1, D),
        w_router.T, b_router.reshape(N_EXPERTS, 1))

    br = EXPERT_ROWS
    counts = cnt[:, 0].astype(jnp.int32)
    padded = (counts + br - 1) // br * br
    pend = jnp.cumsum(padded)
    pstart = pend - padded
    eids = jnp.arange(N_EXPERTS, dtype=jnp.int32)[:, None, None]
    dest = jnp.sum(jnp.where(idxT[None, :TOP_K] == eids, pstart[:, None, None], 0), axis=0) + rankT[:TOP_K]
    n_rows = T * TOP_K + N_EXPERTS * br

    dest_d = dest.reshape(TOP_K, T // DISPATCH_TM, DISPATCH_TM).transpose(1, 0, 2)
    x_rows = _dispatch(pend.astype(jnp.int32), padded.astype(jnp.int32), dest_d, h2, n_rows)
    y_rows = _experts(pstart.astype(jnp.int32), padded.astype(jnp.int32), x_rows, w_gate, b_gate, w_up, b_up,
                      w_down, b_down)
    dest_c = dest.reshape(TOP_K, T // COMBINE_TM, COMBINE_TM).transpose(1, 0, 2)
    out = _combine(dest_c, y_rows, x1, pnat, ln_out_g.reshape(1, D))
    return out.reshape(B, S, D)


def kernel(x, ln_mix_g, w_in, b_gate_a, b_gate_b, w_o_a, w_o_b, w_out, ln_ffn_g, w_router, b_router,
           w_gate, b_gate, w_up, b_up, w_down, b_down, ln_final_g):
    depth = ln_mix_g.shape[0]
    assert depth == 1, "the final RMSNorm is fused into the last layer's combine"
    return _layer(x, ln_mix_g[0], w_in[0], b_gate_a[0], b_gate_b[0], w_o_a[0], w_o_b[0], w_out[0],
                  ln_ffn_g[0], w_router[0], b_router[0], w_gate[0], b_gate[0], w_up[0], b_up[0],
                  w_down[0], b_down[0], ln_final_g)
```

```python
import functools

import jax
import jax.numpy as jnp
from jax import lax
from jax.experimental import pallas as pl
from jax.experimental.pallas import tpu as pltpu

D_MODEL = 1024
HEAD_DIM = 64
ROT_HALF = HEAD_DIM // 8
ROPE_THETA = 500000.0
N_HEADS_A = 8
MOBA_BLOCK = 256
MOBA_TOPK = 3
MOBA_QBLOCKS = 2
MOBA_UNROLL = 4
MOBA_GATE_BLOCKS = 8
DIL_PAIRS = ((128, 1), (512, 4), (2048, 16))
DIL_BAND = 128
HEADS_PER_GROUP_B = 4
WIDTH_A = N_HEADS_A * HEAD_DIM
WIDTH_G = HEADS_PER_GROUP_B * HEAD_DIM
WIDTH_B = WIDTH_G * len(DIL_PAIRS)
N_EXPERTS = 32
TOP_K = 4
SWIGLU_LIMIT = 7.0
SWIGLU_ALPHA = 1.702
NORM_EPS = 1e-5
SCALE = HEAD_DIM ** -0.5
LOG2E = 1.4426950408889634

LANES = 128
PROJ_TM = 1024
PROJ_TM_DILATED = 2048
PERM_TILE = 512
DIL_TL = 2048
POST_TM = 512
EXPERT_ROWS = 128
EXPERT_GROUP = 4
DISPATCH_TM = 512
COMBINE_TM = 256
SUBLANES = 8
ROW_UNROLL = 8
VMEM_LIMIT = 56 * 1024 * 1024
NEG = -1e30

BF16 = jnp.bfloat16
F32 = jnp.float32


def _dot(a, b):
    return jnp.dot(a, b, preferred_element_type=F32)


def _dot_nt(a, b):
    return lax.dot_general(a, b, (((1,), (1,)), ((), ())), preferred_element_type=F32)


def _rms(x, g):
    ms = jnp.mean(x * x, axis=-1, keepdims=True)
    return x * lax.rsqrt(ms + NORM_EPS) * g


def _rows_to_tiles(ref, val):
    n = val.shape[0]
    for c in range(val.shape[1] // LANES):
        ref[pl.ds(c, n, stride=SUBLANES), :] = val[:, c * LANES:(c + 1) * LANES]


def _tiles_to_rows(ref, n):
    return jnp.concatenate([ref[pl.ds(c, n, stride=SUBLANES), :] for c in range(SUBLANES)], axis=1)


def _cparams(sem):
    return pltpu.CompilerParams(dimension_semantics=sem, vmem_limit_bytes=VMEM_LIMIT)


def _deinterleave(h, d):
    tm = h.shape[0]
    sub = min(tm, PERM_TILE)
    cs = sub // d
    ri = lax.broadcasted_iota(jnp.int32, (sub, sub), 0)
    ui = lax.broadcasted_iota(jnp.int32, (sub, sub), 1)
    perm = (ui == (ri % cs) * d + ri // cs).astype(BF16)
    slabs = [_dot(perm, h[s0:s0 + sub]).astype(BF16) for s0 in range(0, tm, sub)]
    return jnp.concatenate([sl[r * cs:(r + 1) * cs] for r in range(d) for sl in slabs], axis=0)


def _proj_kernel(x_ref, g_ref, wq_ref, wk_ref, wv_ref, cq_ref, sq_ref, ck_ref, s1_ref, s2_ref, *outs, na, d):
    tm = x_ref.shape[0]
    h = _rms(x_ref[...], g_ref[...]).astype(BF16)
    if d > 1:
        h = _deinterleave(h, d)
    nq = wq_ref.shape[0]
    heads = nq // HEAD_DIM

    qT = _dot_nt(wq_ref[...], h)
    q3 = qT.reshape(heads, HEAD_DIM, tm)
    c = cq_ref[...][None]
    s = sq_ref[...][None]
    x1 = q3[:, 0:ROT_HALF]
    x2 = q3[:, ROT_HALF:2 * ROT_HALF]
    q3 = jnp.concatenate([x1 * c - x2 * s, x2 * c + x1 * s, q3[:, 2 * ROT_HALF:]], axis=1)
    hid = lax.broadcasted_iota(jnp.int32, (heads, 1, 1), 0)
    q3 = q3 * jnp.where(hid < na // HEAD_DIM, SCALE * LOG2E, SCALE)
    qT = q3.reshape(nq, tm).astype(BF16)

    vT = _dot_nt(wv_ref[...], h).astype(BF16)

    kk = _dot(h, wk_ref[...])
    ck, s1, s2 = ck_ref[...], s1_ref[...], s2_ref[...]
    kparts = []
    for gi in range(kk.shape[1] // LANES):
        kg = kk[:, gi * LANES:(gi + 1) * LANES]
        kparts.append(kg * ck + pltpu.roll(kg, LANES - ROT_HALF, 1) * s1 + pltpu.roll(kg, ROT_HALF, 1) * s2)

    if na:
        qa_ref, va_ref, ka_ref, km_ref, qb_ref, vb_ref, kb_ref = outs
        for blk in range(tm // MOBA_BLOCK):
            sl = slice(blk * MOBA_BLOCK, (blk + 1) * MOBA_BLOCK)
            qa_ref[blk] = qT[0:na, sl]
            va_ref[blk] = vT[0:na, sl]
        lane = lax.broadcasted_iota(jnp.int32, (tm, LANES), 1)
        row = lax.broadcasted_iota(jnp.int32, (tm, LANES), 0)
        blkid = pl.program_id(1) * (tm // MOBA_BLOCK) + row // MOBA_BLOCK
        for gi in range(na // LANES):
            kg = kparts[gi]
            for e in range(2):
                in_head = (lane >= HEAD_DIM * e) & (lane < HEAD_DIM * (e + 1))
                onehot = (lane - HEAD_DIM * (1 - e)) == blkid
                col = (2 * gi + e) * LANES
                ka_ref[:, col:col + LANES] = jnp.where(in_head, kg, onehot.astype(F32)).astype(BF16)
            km = kg.reshape(tm // MOBA_BLOCK, MOBA_BLOCK, LANES).sum(axis=1) * (1.0 / MOBA_BLOCK)
            km_ref[:, gi * LANES:(gi + 1) * LANES] = km
    else:
        qb_ref, vb_ref, kb_ref = outs
    per = tm // d
    for r in range(d):
        for cb in range(per // DIL_BAND):
            sl = slice(r * per + cb * DIL_BAND, r * per + (cb + 1) * DIL_BAND)
            qb_ref[r, cb] = qT[na:na + WIDTH_G, sl]
            vb_ref[r, cb] = vT[na:na + WIDTH_G, sl]
        for gi in range(WIDTH_G // LANES):
            kb_ref[r, :, gi * LANES:(gi + 1) * LANES] = kparts[na // LANES + gi][r * per:(r + 1) * per].astype(BF16)


def _proj_tile(d):
    return PROJ_TM if d == 1 else max(PROJ_TM_DILATED, d * DIL_BAND)


def _project(x, g, wqT, wk, wvT, tabs, gi, d, na):
    B, S, D = x.shape
    L = S // d
    tm = _proj_tile(d)
    per = tm // d
    cq, sq, ck, s1, s2 = tabs
    grid = (B, S // tm)
    full = lambda a: pl.BlockSpec(a.shape, lambda b, i: (0,) * a.ndim)
    in_specs = [
        pl.BlockSpec((None, tm, D), lambda b, i: (b, i, 0)),
        full(g), full(wqT), full(wk), full(wvT),
        pl.BlockSpec((None, ROT_HALF, tm), lambda b, i: (gi, 0, i)),
        pl.BlockSpec((None, ROT_HALF, tm), lambda b, i: (gi, 0, i)),
        pl.BlockSpec((None, tm, LANES), lambda b, i: (gi, i, 0)),
        pl.BlockSpec((None, tm, LANES), lambda b, i: (gi, i, 0)),
        pl.BlockSpec((None, tm, LANES), lambda b, i: (gi, i, 0)),
    ]
    nsub = per // DIL_BAND
    out_shape = [
        jax.ShapeDtypeStruct((B, d, L // DIL_BAND, WIDTH_G, DIL_BAND), BF16),
        jax.ShapeDtypeStruct((B, d, L // DIL_BAND, WIDTH_G, DIL_BAND), BF16),
        jax.ShapeDtypeStruct((B, d, L, WIDTH_G), BF16),
    ]
    out_specs = [
        pl.BlockSpec((None, d, nsub, WIDTH_G, DIL_BAND), lambda b, i: (b, 0, i, 0, 0)),
        pl.BlockSpec((None, d, nsub, WIDTH_G, DIL_BAND), lambda b, i: (b, 0, i, 0, 0)),
        pl.BlockSpec((None, d, per, WIDTH_G), lambda b, i: (b, 0, i, 0)),
    ]
    if na:
        nblk = tm // MOBA_BLOCK
        out_shape = [
            jax.ShapeDtypeStruct((B, S // MOBA_BLOCK, na, MOBA_BLOCK), BF16),
            jax.ShapeDtypeStruct((B, S // MOBA_BLOCK, na, MOBA_BLOCK), BF16),
            jax.ShapeDtypeStruct((B, S, 2 * na), BF16),
            jax.ShapeDtypeStruct((B, S // tm, nblk, na), F32),
        ] + out_shape
        out_specs = [
            pl.BlockSpec((None, nblk, na, MOBA_BLOCK), lambda b, i: (b, i, 0, 0)),
            pl.BlockSpec((None, nblk, na, MOBA_BLOCK), lambda b, i: (b, i, 0, 0)),
            pl.BlockSpec((None, tm, 2 * na), lambda b, i: (b, i, 0)),
            pl.BlockSpec((None, None, nblk, na), lambda b, i: (b, i, 0, 0)),
        ] + out_specs
    return pl.pallas_call(
        functools.partial(_proj_kernel, na=na, d=d),
        grid=grid, in_specs=in_specs, out_specs=out_specs, out_shape=out_shape,
        compiler_params=_cparams(("arbitrary",) * 2),
        name=f"proj_d{d}",
    )(x, g, wqT, wk, wvT, cq, sq, ck, s1, s2)


def _moba_kernel(q_ref, k_ref, v_ref, km_ref, o_ref, qa_scr, s_scr, p_scr, sel_scr):
    n_super = q_ref.shape[0] // MOBA_QBLOCKS
    _moba_select(q_ref, km_ref, sel_scr)
    _moba_prepare(q_ref, k_ref, sel_scr, qa_scr, s_scr, 0)

    def trip(g, c):
        _moba_attend(k_ref, v_ref, o_ref, qa_scr, s_scr, p_scr, g)
        _moba_prepare(q_ref, k_ref, sel_scr, qa_scr, s_scr, jnp.minimum(g + 1, n_super - 1))
        return c

    lax.fori_loop(0, n_super, trip, 0)


def _moba_select(q_ref, km_ref, sel_scr):
    nb, _, blk = q_ref.shape
    cb = MOBA_GATE_BLOCKS
    wq = cb * blk
    bid = lax.broadcasted_iota(jnp.int32, (nb, wq), 0)
    km = km_ref[...].astype(BF16)
    zq = jnp.zeros((HEAD_DIM, wq), BF16)

    def chunk(c, carry):
        qblk = cb * c + lax.broadcasted_iota(jnp.int32, (nb, wq), 1) // blk
        for h in range(2):
            qh = jnp.concatenate([q_ref[cb * c + i, HEAD_DIM * h:HEAD_DIM * (h + 1), :] for i in range(cb)], axis=1)
            q_plain = jnp.concatenate([qh, zq] if h == 0 else [zq, qh], axis=0)
            gt = jnp.where(bid < qblk, _dot(km, q_plain), -jnp.inf)
            sel = jnp.zeros(gt.shape, jnp.bool_)
            for _ in range(MOBA_TOPK):
                mx = jnp.max(gt, axis=0, keepdims=True)
                first = jnp.min(jnp.where((gt == mx) & (mx > -jnp.inf), bid, nb), axis=0, keepdims=True)
                pick = bid == first
                sel = sel | pick
                gt = jnp.where(pick, -jnp.inf, gt)
            sel_scr[h, :, pl.ds(pl.multiple_of(c * wq, wq), wq)] = sel.astype(F32)
        return carry

    lax.fori_loop(0, nb // cb, chunk, 0)


def _moba_prepare(q_ref, k_ref, sel_scr, qa_scr, s_scr, g):
    nb = q_ref.shape[0]
    blk = MOBA_BLOCK
    tq = MOBA_QBLOCKS * blk
    q2 = jnp.concatenate([q_ref[MOBA_QBLOCKS * g + i] for i in range(MOBA_QBLOCKS)], axis=1)
    bid = lax.broadcasted_iota(jnp.int32, (nb, tq), 0)
    qblk = MOBA_QBLOCKS * g + lax.broadcasted_iota(jnp.int32, (nb, tq), 1) // blk
    zb = jnp.zeros((HEAD_DIM - nb, tq), BF16)

    for h in range(2):
        hl = slice(LANES * h, LANES * (h + 1))
        qh = q2[HEAD_DIM * h:HEAD_DIM * (h + 1)]
        sel = sel_scr[h, :, pl.ds(pl.multiple_of(g * tq, tq), tq)] > 0.5
        bias = jnp.where(sel, 0.0, NEG).astype(BF16)
        qa_scr[h] = jnp.concatenate([qh, bias, zb] if h == 0 else [bias, zb, qh], axis=0)
        bias_d = jnp.where(sel | (bid == qblk), 0.0, NEG).astype(BF16)
        q_diag = jnp.concatenate([qh, bias_d, zb] if h == 0 else [bias_d, zb, qh], axis=0)
        for i in range(MOBA_QBLOCKS):
            j = MOBA_QBLOCKS * g + i
            sd = _dot(k_ref[pl.ds(pl.multiple_of(j * blk, blk), blk), hl], q_diag[:, i * blk:])
            kk = lax.broadcasted_iota(jnp.int32, sd.shape, 0)
            qq = lax.broadcasted_iota(jnp.int32, sd.shape, 1)
            sd = jnp.where((qq < blk) & (kk > qq), NEG, sd)
            s_scr[h, i] = jnp.concatenate([jnp.full((blk, i * blk), NEG, F32), sd], axis=1) if i else sd


def _moba_attend(k_ref, v_ref, o_ref, qa_scr, s_scr, p_scr, g):
    nb = v_ref.shape[0]
    blk = MOBA_BLOCK
    tq = MOBA_QBLOCKS * blk

    def score_blocks(c):
        return [jnp.where(c < g, MOBA_QBLOCKS * c + i, nb - 1) for i in range(MOBA_QBLOCKS)]

    def value_blocks(it):
        return [jnp.where(it <= 1, MOBA_QBLOCKS * g + i, MOBA_QBLOCKS * (it - 2) + i) for i in range(MOBA_QBLOCKS)]

    def values(it, h, alpha, acc):
        vl = slice(HEAD_DIM * h, HEAD_DIM * (h + 1))
        acc = alpha * acc
        for i, j in enumerate(value_blocks(it)):
            acc = acc + _dot(v_ref[j, vl, :], p_scr[h, i])
        return acc

    p_scr[...] = jnp.zeros(p_scr.shape, BF16)

    def body(it, carry):
        alphas, ms, ls, accs = carry
        a_new, m_new, l_new, acc_new = [], [], [], []
        for h in range(2):
            hl = slice(LANES * h, LANES * (h + 1))
            qa = qa_scr[h]
            s_next = [_dot(k_ref[pl.ds(pl.multiple_of(j * blk, blk), blk), hl], qa) for j in score_blocks(it)]
            s_cur = [s_scr[h, i] for i in range(MOBA_QBLOCKS)]
            acc_new.append(values(it, h, alphas[h], accs[h]))
            mn = ms[h]
            for sc in s_cur:
                mn = jnp.maximum(mn, jnp.max(sc, axis=0, keepdims=True))
            alpha = jnp.exp2(ms[h] - mn)
            l = alpha * ls[h]
            for i in range(MOBA_QBLOCKS):
                p = jnp.exp2(s_cur[i] - mn)
                l = l + jnp.sum(p, axis=0, keepdims=True)
                s_scr[h, i] = s_next[i]
                p_scr[h, i] = p.astype(BF16)
            m_new.append(mn)
            l_new.append(l)
            a_new.append(alpha)
        return (tuple(a_new), tuple(m_new), tuple(l_new), tuple(acc_new))

    one = jnp.ones((1, tq), F32)
    neg = jnp.full((1, tq), NEG, F32)
    zl = jnp.zeros((1, tq), F32)
    za = jnp.zeros((HEAD_DIM, tq), F32)
    n_it = g + 1
    n_main = n_it // MOBA_UNROLL
    done = MOBA_UNROLL * n_main
    n_two = (n_it - done) // 2

    def trip(width, base):
        def go(t, c):
            for u in range(width):
                c = body(base + width * t + u, c)
            return c
        return go

    carry = lax.fori_loop(0, n_main, trip(MOBA_UNROLL, 0), ((one, one), (neg, neg), (zl, zl), (za, za)))
    carry = lax.fori_loop(0, n_two, trip(2, done), carry)
    alphas, _, ls, accs = lax.fori_loop(done + 2 * n_two, n_it, body, carry)
    accs = [values(n_it, h, alphas[h], accs[h]) for h in range(2)]
    oT = jnp.concatenate([accs[0] / ls[0], accs[1] / ls[1]], axis=0)
    o_ref[pl.ds(pl.multiple_of(g * tq, tq), tq), :] = oT.T.astype(o_ref.dtype)


def _moba(qT, k, vT, kmean):
    B, nb, wa, blk = qT.shape
    S = nb * blk
    hp = wa // LANES
    nq = MOBA_QBLOCKS
    assert nb % 16 == 0 and nb <= HEAD_DIM, "block-mask rows must fit the spare half of a head pair"
    return pl.pallas_call(
        _moba_kernel,
        grid=(B, hp),
        in_specs=[
            pl.BlockSpec((None, nb, LANES, blk), lambda b, p: (b, 0, p, 0)),
            pl.BlockSpec((None, S, 2 * LANES), lambda b, p: (b, 0, p)),
            pl.BlockSpec((None, nb, LANES, blk), lambda b, p: (b, 0, p, 0)),
            pl.BlockSpec((None, nb, LANES), lambda b, p: (b, 0, p)),
        ],
        out_specs=pl.BlockSpec((None, S, LANES), lambda b, p: (b, 0, p)),
        out_shape=jax.ShapeDtypeStruct((B, S, wa), BF16),
        scratch_shapes=[pltpu.VMEM((2, LANES, nq * blk), BF16), pltpu.VMEM((2, nq, blk, nq * blk), F32),
                        pltpu.VMEM((2, nq, blk, nq * blk), BF16), pltpu.VMEM((2, nb, S), F32)],
        compiler_params=_cparams(("arbitrary",) * 2),
        name="moba_attn",
    )(qT, k, vT, kmean)


def _dil_kernel(q_ref, k_ref, kp_ref, v_ref, vp_ref, o_ref, lse_ref):
    t = pl.program_id(2)
    band = DIL_BAND
    nres, nsub = q_ref.shape[0], q_ref.shape[1]
    key_i = lax.broadcasted_iota(jnp.int32, (band, band), 0)
    qry_i = lax.broadcasted_iota(jnp.int32, (band, band), 1)
    rowid = lax.broadcasted_iota(jnp.int32, (LANES, band), 0)
    own_ok = key_i <= qry_i
    prev_ok = key_i >= qry_i
    for r in range(nres):
        for c in range(nsub):
            o_parts, l_parts = [], []
            for hp in range(WIDTH_G // LANES):
                cols = slice(hp * LANES, (hp + 1) * LANES)
                k_own = k_ref[r, c * band:(c + 1) * band, cols]
                k_prev = kp_ref[r, :, cols] if c == 0 else k_ref[r, (c - 1) * band:c * band, cols]
                q2 = q_ref[r, c, cols, :]
                for h in range(2):
                    qh = jnp.where((rowid >= HEAD_DIM * h) & (rowid < HEAD_DIM * (h + 1)), q2, jnp.zeros_like(q2))
                    pmask = (prev_ok & (t > 0)) if c == 0 else prev_ok
                    s_own = jnp.where(own_ok, _dot(k_own, qh), -jnp.inf)
                    s_prev = jnp.where(pmask, _dot(k_prev, qh), -jnp.inf)
                    m = jnp.maximum(jnp.max(s_own, axis=0, keepdims=True), jnp.max(s_prev, axis=0, keepdims=True))
                    p_own = jnp.exp(s_own - m)
                    p_prev = jnp.exp(s_prev - m)
                    l = jnp.sum(p_own, axis=0, keepdims=True) + jnp.sum(p_prev, axis=0, keepdims=True)
                    rows = slice(hp * LANES + h * HEAD_DIM, hp * LANES + (h + 1) * HEAD_DIM)
                    v_own = v_ref[r, c, rows, :]
                    v_prev = vp_ref[r, rows, :] if c == 0 else v_ref[r, c - 1, rows, :]
                    oT = _dot(v_own, p_own.astype(BF16)) + _dot(v_prev, p_prev.astype(BF16))
                    o_parts.append(oT / l)
                    l_parts.append(jnp.broadcast_to(m + jnp.log(l), (HEAD_DIM, band)))
            o_ref[r, c * band:(c + 1) * band, :] = jnp.concatenate(o_parts, axis=0).T
            lse_ref[r, c * band:(c + 1) * band, :] = jnp.concatenate(l_parts, axis=0).T


def _dilated(qT, k, vT):
    B, d, nblk, wg, band = qT.shape
    L = nblk * band
    tl = min(DIL_TL, L)
    nsub = tl // band
    nres = min(d, max(1, DIL_TL // L))
    prev = lambda t: jnp.maximum(t * nsub - 1, 0)
    return pl.pallas_call(
        _dil_kernel,
        grid=(B, d // nres, L // tl),
        in_specs=[
            pl.BlockSpec((None, nres, nsub, wg, band), lambda b, r, t: (b, r, t, 0, 0)),
            pl.BlockSpec((None, nres, tl, wg), lambda b, r, t: (b, r, t, 0)),
            pl.BlockSpec((None, nres, band, wg), lambda b, r, t: (b, r, prev(t), 0)),
            pl.BlockSpec((None, nres, nsub, wg, band), lambda b, r, t: (b, r, t, 0, 0)),
            pl.BlockSpec((None, nres, None, wg, band), lambda b, r, t: (b, r, prev(t), 0, 0)),
        ],
        out_specs=[
            pl.BlockSpec((None, nres, tl, wg), lambda b, r, t: (b, r, t, 0)),
            pl.BlockSpec((None, nres, tl, wg), lambda b, r, t: (b, r, t, 0)),
        ],
        out_shape=[jax.ShapeDtypeStruct((B, d, L, wg), F32)] * 2,
        compiler_params=_cparams(("arbitrary",) * 3),
        name=f"dilated_d{d}",
    )(qT, k, k, vT, vT)


def _interleave(ref, scr):
    d, per, w = ref.shape
    if d == 1:
        return ref[0]
    for r in range(d):
        for sl in range(w // LANES):
            scr[sl, pl.ds(r, per, stride=d), :] = ref[r, :, sl * LANES:(sl + 1) * LANES]
    return jnp.concatenate([scr[sl] for sl in range(w // LANES)], axis=1)


def _post_kernel(x_ref, oa_ref, o1_ref, o2_ref, o3_ref, l1_ref, l2_ref, l3_ref, gmix_ref, wg_ref, bga_ref,
                 bgb_ref, woa_ref, wob_ref, wout_ref, gffn_ref, wr_ref, br_ref,
                 x1_ref, h2_ref, idx_ref, rank_ref, pnat_ref, cnt_ref, carry_scr, il_scr):
    i = pl.program_id(0)
    tm, D = x_ref.shape
    ne = wr_ref.shape[0]

    @pl.when(i == 0)
    def _():
        carry_scr[...] = jnp.zeros_like(carry_scr)

    x = x_ref[...]
    h = _rms(x, gmix_ref[...]).astype(BF16)
    gates = _dot(h, wg_ref[...])
    ga = gates[:, :D] + bga_ref[...]
    gb = gates[:, D:] + bgb_ref[...]
    l1, l2, l3 = [_interleave(r, il_scr.at[n]) for n, r in enumerate((l1_ref, l2_ref, l3_ref))]
    o1, o2, o3 = [_interleave(r, il_scr.at[3 + n]) for n, r in enumerate((o1_ref, o2_ref, o3_ref))]
    mx = jnp.maximum(jnp.maximum(l1, l2), l3)
    e1, e2, e3 = jnp.exp(l1 - mx), jnp.exp(l2 - mx), jnp.exp(l3 - mx)
    ob = (e1 * o1 + e2 * o2 + e3 * o3) / (e1 + e2 + e3)
    ya = _dot(oa_ref[...], woa_ref[...])
    yb = _dot(ob.astype(BF16), wob_ref[...])
    mix = jax.nn.sigmoid(ga) * ya + jax.nn.sigmoid(gb) * yb
    x1 = x + _dot(mix.astype(BF16), wout_ref[...])
    x1_ref[...] = x1
    h2 = _rms(x1, gffn_ref[...])
    _rows_to_tiles(h2_ref, h2)

    wr = wr_ref[...]
    wr_hi = wr.astype(BF16)
    wr_lo = (wr - wr_hi.astype(F32)).astype(BF16)
    h2_hi = h2.astype(BF16)
    h2_lo = (h2 - h2_hi.astype(F32)).astype(BF16)
    logits = _dot_nt(wr_hi, h2_hi) + (_dot_nt(wr_hi, h2_lo) + _dot_nt(wr_lo, h2_hi)) + br_ref[...]
    eid = lax.broadcasted_iota(jnp.int32, (ne, tm), 0)
    g = logits
    vals, idxs, picks = [], [], []
    for _ in range(TOP_K):
        m = jnp.max(g, axis=0, keepdims=True)
        first = jnp.min(jnp.where(g == m, eid, ne), axis=0, keepdims=True)
        pick = eid == first
        vals.append(m)
        idxs.append(first)
        picks.append(pick)
        g = jnp.where(pick, -jnp.inf, g)
    es = [jnp.exp(v - vals[0]) for v in vals]
    den = es[0] + es[1] + es[2] + es[3]
    probs = [e / den for e in es]

    onehot = jnp.zeros((ne, tm), F32)
    for pick in picks:
        onehot = onehot + pick.astype(F32)
    earlier = (lax.broadcasted_iota(jnp.int32, (tm, tm), 0) < lax.broadcasted_iota(jnp.int32, (tm, tm), 1))
    prefix = _dot(onehot.astype(BF16), earlier.astype(BF16)) + carry_scr[:, 0:1]
    ranks = [jnp.sum(jnp.where(pick, prefix, 0.0), axis=0, keepdims=True) for pick in picks]
    carry_scr[...] = carry_scr[...] + jnp.sum(onehot, axis=1, keepdims=True)
    cnt_ref[...] = carry_scr[...]

    zi = jnp.zeros((8 - TOP_K, tm), jnp.int32)
    idx_ref[...] = jnp.concatenate(idxs + [zi], axis=0)
    rank_ref[...] = jnp.concatenate([r.astype(jnp.int32) for r in ranks] + [zi], axis=0)
    pnat_ref[...] = jnp.concatenate(probs + [jnp.zeros((LANES - TOP_K, tm), F32)], axis=0).T


def _post(x2, oa, obs, lses, gmix, wg, bga, bgb, woa, wob, wout, gffn, wrT, br):
    T, D = x2.shape
    tm = POST_TM
    ne = wrT.shape[0]
    row = lambda w: pl.BlockSpec((tm, w), lambda i: (i, 0))

    def grouped(a):
        _, d, L, w = a.shape
        nt = L * d // tm
        return pl.BlockSpec((None, d, tm // d, w), lambda i: (i // nt, 0, i % nt, 0))

    full = lambda a: pl.BlockSpec(a.shape, lambda i: (0,) * a.ndim)
    col = pl.BlockSpec((8, tm), lambda i: (0, i))
    return pl.pallas_call(
        _post_kernel,
        grid=(T // tm,),
        in_specs=[row(D), row(WIDTH_A)] + [grouped(a) for a in (*obs, *lses)]
                 + [full(a) for a in (gmix, wg, bga, bgb, woa, wob, wout, gffn, wrT, br)],
        out_specs=[row(D), pl.BlockSpec((tm * SUBLANES, LANES), lambda i: (i, 0)), col, col, row(LANES),
                   pl.BlockSpec((ne, LANES), lambda i: (0, 0))],
        out_shape=[
            jax.ShapeDtypeStruct((T, D), F32), jax.ShapeDtypeStruct((T * SUBLANES, LANES), F32),
            jax.ShapeDtypeStruct((8, T), jnp.int32), jax.ShapeDtypeStruct((8, T), jnp.int32),
            jax.ShapeDtypeStruct((T, LANES), F32),
            jax.ShapeDtypeStruct((ne, LANES), F32),
        ],
        scratch_shapes=[pltpu.VMEM((ne, LANES), F32), pltpu.VMEM((6, WIDTH_G // LANES, tm, LANES), F32)],
        compiler_params=_cparams(("arbitrary",)),
        name="post_mix_router",
    )(x2, oa, *obs, *lses, gmix, wg, bga, bgb, woa, wob, wout, gffn, wrT, br)


def _dispatch_kernel(pend_ref, padded_ref, dest_ref, h_ref, xr_ref, zero_scr, sem):
    i = pl.program_id(0)
    tm = h_ref.shape[0] // SUBLANES
    br = zero_scr.shape[0]

    def zero_copy(blk_start):
        return pltpu.make_async_copy(zero_scr, xr_ref.at[pl.ds(pl.multiple_of(blk_start * SUBLANES, br), br), :], sem)

    @pl.when(i == 0)
    def _():
        zero_scr[...] = jnp.zeros_like(zero_scr)
        n_used = pend_ref[N_EXPERTS - 1] // EXPERT_ROWS
        n_blk = xr_ref.shape[0] // br

        def tail(fn):
            def go(e, c):
                @pl.when(padded_ref[e] > 0)
                def _():
                    fn(zero_copy(pend_ref[e] - EXPERT_ROWS))
                return c
            return go

        def unused(fn):
            def go(b, c):
                fn(zero_copy(b * EXPERT_ROWS))
                return c
            return go

        lax.fori_loop(0, N_EXPERTS, tail(lambda cp: cp.start()), 0)
        lax.fori_loop(n_used, n_blk, unused(lambda cp: cp.start()), 0)
        lax.fori_loop(0, N_EXPERTS, tail(lambda cp: cp.wait()), 0)
        lax.fori_loop(n_used, n_blk, unused(lambda cp: cp.wait()), 0)

    def row_copy(t, k):
        src = h_ref.at[pl.ds(pl.multiple_of(t * SUBLANES, SUBLANES), SUBLANES), :]
        dst = xr_ref.at[pl.ds(pl.multiple_of(dest_ref[k, t] * SUBLANES, SUBLANES), SUBLANES), :]
        return pltpu.make_async_copy(src, dst, sem)

    def start(tb, c):
        for u in range(ROW_UNROLL):
            for k in range(TOP_K):
                row_copy(tb * ROW_UNROLL + u, k).start(priority=(u * TOP_K + k) % 2)
        return c

    def wait(tb, c):
        for u in range(ROW_UNROLL):
            for k in range(TOP_K):
                row_copy(tb * ROW_UNROLL + u, k).wait()
        return c

    lax.fori_loop(0, tm // ROW_UNROLL, start, 0)
    lax.fori_loop(0, tm // ROW_UNROLL, wait, 0)


def _dispatch(pend, padded, dest3, h2t, n_rows):
    tm = DISPATCH_TM
    return pl.pallas_call(
        _dispatch_kernel,
        grid_spec=pltpu.PrefetchScalarGridSpec(
            num_scalar_prefetch=2,
            grid=(h2t.shape[0] // (tm * SUBLANES),),
            in_specs=[
                pl.BlockSpec((None, TOP_K, tm), lambda i, pe, pa: (i, 0, 0), memory_space=pltpu.SMEM),
                pl.BlockSpec((tm * SUBLANES, LANES), lambda i, pe, pa: (i, 0)),
            ],
            out_specs=pl.BlockSpec(memory_space=pl.ANY),
            scratch_shapes=[pltpu.VMEM((EXPERT_ROWS * SUBLANES, LANES), F32), pltpu.SemaphoreType.DMA(())],
        ),
        out_shape=jax.ShapeDtypeStruct((n_rows * SUBLANES, LANES), F32),
        compiler_params=_cparams(("arbitrary",)),
        name="moe_dispatch",
    )(pend, padded, dest3, h2t)


def _expert_kernel(pstart_ref, padded_ref, x_hbm, wg_ref, bg_ref, wu_ref, bu_ref, wd_ref, bd_ref, y_hbm,
                   w_scr, xbuf, ybuf, xsem, ysem):
    e = pl.program_id(0)
    tb = EXPERT_ROWS * SUBLANES
    nblk = padded_ref[e] // EXPERT_ROWS
    ngrp = nblk // EXPERT_GROUP
    rem = nblk - ngrp * EXPERT_GROUP
    row0 = pstart_ref[e]

    def rows(first_blk, nb_):
        return pl.ds(pl.multiple_of((row0 + first_blk * EXPERT_ROWS) * SUBLANES, tb), nb_ * tb)

    def x_copy(first_blk, nb_, slot):
        return pltpu.make_async_copy(x_hbm.at[rows(first_blk, nb_), :], xbuf.at[slot, pl.ds(0, nb_ * tb), :],
                                     xsem.at[slot])

    def y_copy(first_blk, nb_, slot):
        return pltpu.make_async_copy(ybuf.at[slot, pl.ds(0, nb_ * tb), :], y_hbm.at[rows(first_blk, nb_), :],
                                     ysem.at[slot])

    def ffn(slot, nb_):
        n = nb_ * EXPERT_ROWS
        x = _tiles_to_rows(xbuf.at[slot, pl.ds(0, nb_ * tb), :], n).astype(BF16)
        g = _dot(x, w_scr[0]) + bg_ref[...]
        u = _dot(x, w_scr[1]) + bu_ref[...]
        g = jnp.minimum(g, SWIGLU_LIMIT)
        u = jnp.clip(u, -SWIGLU_LIMIT, SWIGLU_LIMIT)
        a = g * jax.nn.sigmoid(SWIGLU_ALPHA * g) * (u + 1.0)
        _rows_to_tiles(ybuf.at[slot, pl.ds(0, nb_ * tb), :], _dot(a.astype(BF16), w_scr[2]) + bd_ref[...])

    def for_step(s, full, short):
        @pl.when(s < ngrp)
        def _():
            full()
        for r in range(1, EXPERT_GROUP):
            @pl.when((s == ngrp) & (rem == r))
            def _():
                short(r)

    @pl.when(nblk > 0)
    def _():
        nstep = ngrp + (rem > 0).astype(jnp.int32)

        def start_x(s, slot):
            for_step(s, lambda: x_copy(s * EXPERT_GROUP, EXPERT_GROUP, slot).start(),
                     lambda r: x_copy(s * EXPERT_GROUP, r, slot).start())

        def wait_y(s, slot):
            for_step(s, lambda: y_copy(s * EXPERT_GROUP, EXPERT_GROUP, slot).wait(),
                     lambda r: y_copy(s * EXPERT_GROUP, r, slot).wait())

        start_x(0, 0)
        w_scr[0] = wg_ref[...].astype(BF16)
        w_scr[1] = wu_ref[...].astype(BF16)
        w_scr[2] = wd_ref[...].astype(BF16)

        def body(s, c):
            slot = s % 2
            start_x(s + 1, 1 - slot)
            x_copy(s * EXPERT_GROUP, EXPERT_GROUP, slot).wait()

            @pl.when(s >= 2)
            def _():
                y_copy((s - 2) * EXPERT_GROUP, EXPERT_GROUP, slot).wait()

            ffn(slot, EXPERT_GROUP)
            y_copy(s * EXPERT_GROUP, EXPERT_GROUP, slot).start()
            return c

        lax.fori_loop(0, ngrp, body, 0)

        for r in range(1, EXPERT_GROUP):
            @pl.when(rem == r)
            def _():
                slot = ngrp % 2
                x_copy(ngrp * EXPERT_GROUP, r, slot).wait()

                @pl.when(ngrp >= 2)
                def _():
                    y_copy((ngrp - 2) * EXPERT_GROUP, EXPERT_GROUP, slot).wait()

                ffn(slot, r)
                y_copy(ngrp * EXPERT_GROUP, r, slot).start()

        @pl.when(nstep >= 2)
        def _():
            wait_y(nstep - 2, nstep % 2)

        wait_y(nstep - 1, (nstep - 1) % 2)

    @pl.when(e == pl.num_programs(0) - 1)
    def _():
        n_used = (pstart_ref[e] + padded_ref[e]) // EXPERT_ROWS
        n_all = y_hbm.shape[0] // tb
        ybuf[0, pl.ds(0, tb), :] = jnp.zeros((tb, LANES), F32)

        def zero_copy(b):
            return pltpu.make_async_copy(ybuf.at[0, pl.ds(0, tb), :],
                                         y_hbm.at[pl.ds(pl.multiple_of(b * tb, tb), tb), :], ysem.at[0])

        lax.fori_loop(n_used, n_all, lambda b, c: (zero_copy(b).start(), c)[1], 0)
        lax.fori_loop(n_used, n_all, lambda b, c: (zero_copy(b).wait(), c)[1], 0)


def _experts(pstart, padded, x_rows, w_gate, b_gate, w_up, b_up, w_down, b_down):
    E, D, F = w_gate.shape
    tb = EXPERT_GROUP * EXPERT_ROWS * SUBLANES
    wspec = lambda shape: pl.BlockSpec((None,) + shape, lambda e, ps, pa: (e, 0, 0))
    return pl.pallas_call(
        _expert_kernel,
        grid_spec=pltpu.PrefetchScalarGridSpec(
            num_scalar_prefetch=2,
            grid=(E,),
            in_specs=[
                pl.BlockSpec(memory_space=pl.ANY),
                wspec((D, F)), wspec((1, F)), wspec((D, F)), wspec((1, F)), wspec((F, D)), wspec((1, D)),
            ],
            out_specs=pl.BlockSpec(memory_space=pl.ANY),
            scratch_shapes=[pltpu.VMEM((3, D, F), BF16), pltpu.VMEM((2, tb, LANES), F32),
                            pltpu.VMEM((2, tb, LANES), F32), pltpu.SemaphoreType.DMA((2,)),
                            pltpu.SemaphoreType.DMA((2,))],
        ),
        out_shape=jax.ShapeDtypeStruct(x_rows.shape, F32),
        compiler_params=_cparams(("arbitrary",)),
        name="moe_experts",
    )(pstart, padded, x_rows, w_gate, b_gate.reshape(E, 1, F), w_up, b_up.reshape(E, 1, F),
      w_down, b_down.reshape(E, 1, D))


def _combine_kernel(dest_ref, dnext_ref, y_ref, x1_ref, p_ref, g_ref, o_ref, ybuf, sem):
    i = pl.program_id(0)
    tm = x1_ref.shape[0]
    slot = i % 2

    def row_copy(d_ref, sl, t, k):
        src = y_ref.at[pl.ds(pl.multiple_of(d_ref[k, t] * SUBLANES, SUBLANES), SUBLANES), :]
        dst = ybuf.at[sl, k, pl.ds(pl.multiple_of(t * SUBLANES, SUBLANES), SUBLANES), :]
        return pltpu.make_async_copy(src, dst, sem.at[sl])

    def request(d_ref, sl):
        def go(tb, c):
            for u in range(ROW_UNROLL):
                for k in range(TOP_K):
                    row_copy(d_ref, sl, tb * ROW_UNROLL + u, k).start(priority=(u * TOP_K + k) % 2)
            return c
        lax.fori_loop(0, tm // ROW_UNROLL, go, 0)

    @pl.when(i == 0)
    def _():
        request(dest_ref, 0)

    @pl.when(i + 1 < pl.num_programs(0))
    def _():
        request(dnext_ref, 1 - slot)

    def wait(tb, c):
        for u in range(ROW_UNROLL):
            for k in range(TOP_K):
                row_copy(dest_ref, slot, tb * ROW_UNROLL + u, k).wait()
        return c

    lax.fori_loop(0, tm // ROW_UNROLL, wait, 0)
    p = p_ref[...]
    y = x1_ref[...]
    for k in range(TOP_K):
        y = y + p[:, k:k + 1] * _tiles_to_rows(ybuf.at[slot, k], tm)
    o_ref[...] = _rms(y, g_ref[...])


def _combine(dest3, y_rows, x1, pnat, g_final):
    T, D = x1.shape
    tm = COMBINE_TM
    nt = T // tm
    return pl.pallas_call(
        _combine_kernel,
        grid=(nt,),
        in_specs=[
            pl.BlockSpec((None, TOP_K, tm), lambda i: (i, 0, 0), memory_space=pltpu.SMEM),
            pl.BlockSpec((None, TOP_K, tm), lambda i: (jnp.minimum(i + 1, nt - 1), 0, 0), memory_space=pltpu.SMEM),
            pl.BlockSpec(memory_space=pl.ANY),
            pl.BlockSpec((tm, D), lambda i: (i, 0)),
            pl.BlockSpec((tm, LANES), lambda i: (i, 0)),
            pl.BlockSpec((1, D), lambda i: (0, 0)),
        ],
        out_specs=pl.BlockSpec((tm, D), lambda i: (i, 0)),
        out_shape=jax.ShapeDtypeStruct((T, D), F32),
        scratch_shapes=[pltpu.VMEM((2, TOP_K, tm * SUBLANES, LANES), F32), pltpu.SemaphoreType.DMA((2,))],
        compiler_params=_cparams(("arbitrary",)),
        name="moe_combine",
    )(dest3, dest3, y_rows, x1, pnat, g_final)


def _rope_tables(S, groups):
    ng = len(groups)

    def cos_sin(shape, ax, f):
        rho = lax.broadcasted_iota(jnp.int32, shape, ax)
        gid = lax.broadcasted_iota(jnp.int32, shape, 0)
        pos = jnp.zeros(shape, jnp.int32)
        for n, (d, tm) in enumerate(groups):
            per = tm // d
            w = rho % tm
            pos = jnp.where(gid == n, ((rho // tm) * per + w % per) * d + w // per, pos)
        inv = jnp.float32(ROPE_THETA) ** (-f.astype(F32) / ROT_HALF)
        ang = pos.astype(F32) * inv
        return jnp.cos(ang), jnp.sin(ang)

    sh = (ng, ROT_HALF, S)
    cq, sq = cos_sin(sh, 2, lax.broadcasted_iota(jnp.int32, sh, 1))
    sh = (ng, S, ROT_HALF)
    cos, sin = cos_sin(sh, 1, lax.broadcasted_iota(jnp.int32, sh, 2))
    lane = lax.broadcasted_iota(jnp.int32, (ROT_HALF, LANES), 1) % HEAD_DIM
    f = lax.broadcasted_iota(jnp.int32, (ROT_HALF, LANES), 0)
    lo = ((lane == f)).astype(F32)
    hi = ((lane == f + ROT_HALF)).astype(F32)
    spread = lambda a, m: jnp.dot(a, m, precision=lax.Precision.HIGHEST)
    rest = (lax.broadcasted_iota(jnp.int32, (1, 1, LANES), 2) % HEAD_DIM >= 2 * ROT_HALF).astype(F32)
    ck = spread(cos, lo + hi) + rest
    s1 = spread(-sin, lo)
    s2 = spread(sin, hi)
    return cq, sq, ck, s1, s2


def _layer(x, ln_mix_g, w_in, b_gate_a, b_gate_b, w_o_a, w_o_b, w_out, ln_ffn_g, w_router, b_router,
           w_gate, b_gate, w_up, b_up, w_down, b_down, ln_out_g):
    B, S, D = x.shape
    T = B * S
    qa0, ka0, va0 = 0, WIDTH_A, 2 * WIDTH_A
    qb0, kb0, vb0 = 3 * WIDTH_A, 3 * WIDTH_A + WIDTH_B, 3 * WIDTH_A + 2 * WIDTH_B
    g0 = 3 * WIDTH_A + 3 * WIDTH_B
    cols = lambda s, w: w_in[:, s:s + w]
    gmix = ln_mix_g.reshape(1, D)

    tabs = _rope_tables(S, [(d, _proj_tile(d)) for _, d in DIL_PAIRS])
    obs, lses = [], []
    o_a = None
    for gi, (_, d) in enumerate(DIL_PAIRS):
        off = gi * WIDTH_G
        wq, wk, wv = cols(qb0 + off, WIDTH_G), cols(kb0 + off, WIDTH_G), cols(vb0 + off, WIDTH_G)
        na = 0
        if gi == 0:
            na = WIDTH_A
            wq = jnp.concatenate([cols(qa0, WIDTH_A), wq], axis=1)
            wk = jnp.concatenate([cols(ka0, WIDTH_A), wk], axis=1)
            wv = jnp.concatenate([cols(va0, WIDTH_A), wv], axis=1)
        outs = _project(x, gmix, wq.T.astype(BF16), wk.astype(BF16), wv.T.astype(BF16), tabs, gi, d, na)
        if gi == 0:
            qTa, vTa, ka, kmean = outs[:4]
            outs = outs[4:]
            o_a = _moba(qTa, ka, vTa, kmean.reshape(B, S // MOBA_BLOCK, WIDTH_A))
        qTb, vTb, kb = outs
        o_g, lse_g = _dilated(qTb, kb, vTb)
        obs.append(o_g)
        lses.append(lse_g)

    x1, h2, idxT, rankT, pnat, cnt = _post(
        x.reshape(T, D), o_a.reshape(T, WIDTH_A), obs, lses, gmix,
        cols(g0, 2 * D).astype(BF16), b_gate_a.reshape(1, D), b_gate_b.reshape(1, D),
        w_o_a.astype(BF16), w_o_b.astype(BF16), w_out.astype(BF16), ln_ffn_g.reshape(1, D),
        w_router.T, b_router.reshape(N_EXPERTS, 1))

    br = EXPERT_ROWS
    counts = cnt[:, 0].astype(jnp.int32)
    padded = (counts + br - 1) // br * br
    pend = jnp.cumsum(padded)
    pstart = pend - padded
    eids = jnp.arange(N_EXPERTS, dtype=jnp.int32)[:, None, None]
    dest = jnp.sum(jnp.where(idxT[None, :TOP_K] == eids, pstart[:, None, None], 0), axis=0) + rankT[:TOP_K]
    n_rows = T * TOP_K + N_EXPERTS * br

    dest_d = dest.reshape(TOP_K, T // DISPATCH_TM, DISPATCH_TM).transpose(1, 0, 2)
    x_rows = _dispatch(pend.astype(jnp.int32), padded.astype(jnp.int32), dest_d, h2, n_rows)
    y_rows = _experts(pstart.astype(jnp.int32), padded.astype(jnp.int32), x_rows, w_gate, b_gate, w_up, b_up,
                      w_down, b_down)
    dest_c = dest.reshape(TOP_K, T // COMBINE_TM, COMBINE_TM).transpose(1, 0, 2)
    out = _combine(dest_c, y_rows, x1, pnat, ln_out_g.reshape(1, D))
    return out.reshape(B, S, D)


def kernel(x, ln_mix_g, w_in, b_gate_a, b_gate_b, w_o_a, w_o_b, w_out, ln_ffn_g, w_router, b_router,
           w_gate, b_gate, w_up, b_up, w_down, b_down, ln_final_g):
    depth = ln_mix_g.shape[0]
    assert depth == 1, "the final RMSNorm is fused into the last layer's combine"
    return _layer(x, ln_mix_g[0], w_in[0], b_gate_a[0], b_gate_b[0], w_o_a[0], w_o_b[0], w_out[0],
                  ln_ffn_g[0], w_router[0], b_router[0], w_gate[0], b_gate[0], w_up[0], b_up[0],
                  w_down[0], b_down[0], ln_final_g)
```
